```python
import jax, jax.numpy as jnp
from jax import lax
import numpy as np

D_MODEL = 2048
BATCH = 4
SEQ = 2048
DEPTH = 4
DEC_BATCH = 128
DEC_SEQ = 4
PAST_LEN = 8192
PAGE_SIZE = 128

N_HEADS = D_MODEL // 128
QK_NOPE = 64
QK_ROPE = 32
V_HEAD = 64
W_ATTN = N_HEADS * V_HEAD
Q_LORA = D_MODEL // 4
KV_LORA = D_MODEL // 8
ROPE_BASE = 10000.0
SM_SCALE = (QK_NOPE + QK_ROPE) ** -0.5
Q_BLOCK = 128
W_SSM = D_MODEL // 4
SSM_GROUP = 16
SSM_GROUPS = W_SSM // SSM_GROUP
SSM_STATE = 64
DT_MIN = 1e-3
DT_MAX = 1e-1
W_POOL = D_MODEL // 4
POOL_WINDOWS = (2, 4, 8, 16)
POOL_GW = W_POOL // len(POOL_WINDOWS)
POOL_PAD = max(POOL_WINDOWS) - 1
N_BRANCH = 3
MIX_WIDTH = W_ATTN + W_SSM + W_POOL
IN_COLS = Q_LORA + KV_LORA + QK_ROPE + W_SSM + W_POOL + N_BRANCH * D_MODEL
D_FF = 256 * (-(-8 * D_MODEL // (3 * 256)))
N_EXPERTS = 8
TOP_K = 2
MOE_MAX_BLOCK = 512
N_DENSE = (DEPTH + 1) // 2
N_MOE = DEPTH // 2
EPS = 1e-6

kernel_name = "hybrid_mla_s5_pool_moe_decoder_step"


def rms_norm(x, g):
    xf = x.astype(jnp.float32)
    y = xf * lax.rsqrt(jnp.mean(xf * xf, axis=-1, keepdims=True) + EPS)
    return (y * g.astype(jnp.float32)).astype(x.dtype)


def rope(x, pos):
    half = x.shape[-1] // 2
    freqs = ROPE_BASE ** (-jnp.arange(half, dtype=jnp.float32) / half)
    ang = pos.astype(jnp.float32)[:, None] * freqs[None, :]
    ang = ang.reshape(ang.shape[:1] + (1,) * (x.ndim - 3) + (half,))
    cos, sin = jnp.cos(ang), jnp.sin(ang)
    xf = x.astype(jnp.float32)
    x1, x2 = xf[..., :half], xf[..., half:]
    return jnp.concatenate([x1 * cos - x2 * sin, x1 * sin + x2 * cos], axis=-1).astype(x.dtype)


def swiglu(h, w1, w3, w2):
    return (jax.nn.silu(h @ w1) * (h @ w3)) @ w2


def moe_swiglu(h, w_router, w1, w3, w2):
    n, d = h.shape
    logits = h.astype(jnp.float32) @ w_router.astype(jnp.float32)
    top_v, top_e = lax.top_k(logits, TOP_K)
    gate = jax.nn.softmax(top_v, axis=-1)
    nk = n * TOP_K
    per_e = -(-nk // N_EXPERTS)
    blk = min(MOE_MAX_BLOCK, -(-per_e // 8) * 8)
    n_blocks = -(-nk // blk) + N_EXPERTS
    flat_e = top_e.reshape(-1)
    order = jnp.argsort(flat_e)
    sorted_e = flat_e[order]
    counts = jnp.bincount(flat_e, length=N_EXPERTS)
    padded = (counts + blk - 1) // blk * blk
    pad_end = jnp.cumsum(padded)
    dest = (pad_end - padded)[sorted_e] + jnp.arange(nk) - (jnp.cumsum(counts) - counts)[sorted_e]
    row_assign = jnp.full((n_blocks * blk,), nk, jnp.int32).at[dest].set(order.astype(jnp.int32))
    tok_of = jnp.concatenate([jnp.arange(nk, dtype=jnp.int32) // TOP_K, jnp.array([n], jnp.int32)])
    gate_of = jnp.concatenate([gate.reshape(-1), jnp.zeros((1,), jnp.float32)])
    row_tok = tok_of[row_assign]
    row_gate = gate_of[row_assign]
    block_e = jnp.minimum(jnp.searchsorted(pad_end, jnp.arange(n_blocks) * blk, side="right"), N_EXPERTS - 1)
    h_pad = jnp.concatenate([h, jnp.zeros((1, d), h.dtype)], axis=0)
    xb = h_pad[row_tok].reshape(n_blocks, blk, d)

    def expert_block(args):
        xe, e = args
        return swiglu(xe, w1[e], w3[e], w2[e])

    yb = lax.map(expert_block, (xb, block_e))
    out = jnp.zeros((n + 1, d), jnp.float32).at[row_tok].add(
        yb.reshape(-1, d).astype(jnp.float32) * row_gate[:, None])
    return out[:n].astype(h.dtype)


def mixer_inputs(h, pos, lp):
    b, l = h.shape[:2]
    z = h @ lp["w_in"]
    o1 = Q_LORA
    o2 = o1 + KV_LORA
    o3 = o2 + QK_ROPE
    o4 = o3 + W_SSM
    o5 = o4 + W_POOL
    q_c, kv_c, kpe_raw, u_ssm, u_pool, z_gate = jnp.split(z, [o1, o2, o3, o4, o5], axis=-1)
    q = (rms_norm(q_c, lp["q_norm_g"]) @ lp["w_q_up"]).reshape(b, l, N_HEADS, QK_NOPE + QK_ROPE)
    q_nope = q[..., :QK_NOPE]
    q_pe = rope(q[..., QK_NOPE:], pos)
    q_abs = jnp.einsum("blhd,chd->blhc", q_nope, lp["w_uk"])
    c = rms_norm(kv_c, lp["kv_norm_g"])
    k_pe = rope(kpe_raw, pos)
    gates = jax.nn.sigmoid(z_gate.astype(jnp.float32)).astype(h.dtype).reshape(b, l, N_BRANCH, D_MODEL)
    return q_abs, q_pe, c, k_pe, u_ssm, u_pool, gates


def mla_prompt_attention(q_abs, q_pe, c, k_pe):
    b, s, h, cd = q_abs.shape
    nb = s // Q_BLOCK
    qa = q_abs.reshape(b, nb, Q_BLOCK, h, cd).transpose(1, 0, 2, 3, 4)
    qr = q_pe.reshape(b, nb, Q_BLOCK, h, QK_ROPE).transpose(1, 0, 2, 3, 4)
    k_pos = jnp.arange(s)

    def block(args):
        qa_b, qr_b, i = args
        sc = (jnp.einsum("bqhc,bkc->bhqk", qa_b, c)
              + jnp.einsum("bqhr,bkr->bhqk", qr_b, k_pe)).astype(jnp.float32) * SM_SCALE
        q_pos = i * Q_BLOCK + jnp.arange(Q_BLOCK)
        sc = jnp.where(k_pos[None, :] <= q_pos[:, None], sc, -jnp.inf)
        p = jax.nn.softmax(sc, axis=-1).astype(c.dtype)
        return jnp.einsum("bhqk,bkc->bqhc", p, c)

    o = lax.map(block, (qa, qr, jnp.arange(nb)))
    return o.transpose(1, 0, 2, 3, 4).reshape(b, s, h, cd)


def mla_sample_attention(q_abs, q_pe, c_new, kpe_new, c_past, kpe_past):
    t = q_abs.shape[1]
    n_past = c_past.shape[1]
    s_past = (jnp.einsum("bqhc,bkc->bhqk", q_abs, c_past)
              + jnp.einsum("bqhr,bkr->bhqk", q_pe, kpe_past)).astype(jnp.float32) * SM_SCALE
    s_new = (jnp.einsum("bqhc,bkc->bhqk", q_abs, c_new)
             + jnp.einsum("bqhr,bkr->bhqk", q_pe, kpe_new)).astype(jnp.float32) * SM_SCALE
    s_new = jnp.where(jnp.tril(jnp.ones((t, t), dtype=bool)), s_new, -jnp.inf)
    p = jax.nn.softmax(jnp.concatenate([s_past, s_new], axis=-1), axis=-1).astype(c_new.dtype)
    return (jnp.einsum("bhqk,bkc->bqhc", p[..., :n_past], c_past)
            + jnp.einsum("bhqk,bkc->bqhc", p[..., n_past:], c_new))


def _lin_comb(left, right):
    return left[0] * right[0], right[0] * left[1] + right[1]


def ssm_branch(u, h0, lp):
    b, l, _ = u.shape
    f32 = jnp.float32
    lam = lax.complex(lp["lam_re"].astype(f32), lp["lam_im"].astype(f32))
    dt = jnp.exp(lp["log_dt"].astype(f32))[:, None]
    lam_bar = jnp.exp(lam * dt)
    b_c = lax.complex(lp["b_re"].astype(f32), lp["b_im"].astype(f32))
    b_bar = ((lam_bar - 1.0) / lam)[..., None] * b_c
    c_c = lax.complex(lp["c_re"].astype(f32), lp["c_im"].astype(f32))
    uf = u.astype(f32)
    ug = uf.reshape(b, l, SSM_GROUPS, SSM_GROUP).astype(jnp.complex64)
    bu = jnp.einsum("gpn,blgn->blgp", b_bar, ug)
    bu = bu.at[:, 0].add(lam_bar[None] * h0)
    a = jnp.broadcast_to(lam_bar, bu.shape)
    _, hs = lax.associative_scan(_lin_comb, (a, bu), axis=1)
    y = jnp.einsum("gnp,blgp->blgn", c_c, hs).real.reshape(b, l, W_SSM) + lp["d_skip"].astype(f32) * uf
    z = jax.nn.gelu(y)
    out = z * jax.nn.sigmoid(z @ lp["w_glu"].astype(f32))
    return out.astype(u.dtype), hs[:, -1]


def pool_branch(u_ext, pos, lp):
    b, le, _ = u_ext.shape
    l = pos.shape[0]
    uf = u_ext.astype(jnp.float32)
    cs = jnp.concatenate([jnp.zeros((b, 1, W_POOL), jnp.float32), jnp.cumsum(uf, axis=1)], axis=1)
    end = cs[:, POOL_PAD + 1:POOL_PAD + 1 + l]
    means = []
    for g, w in enumerate(POOL_WINDOWS):
        sl = slice(g * POOL_GW, (g + 1) * POOL_GW)
        tot = end[..., sl] - cs[:, POOL_PAD + 1 - w:POOL_PAD + 1 - w + l, sl]
        cnt = jnp.minimum(pos + 1, w).astype(jnp.float32)[None, :, None]
        means.append(tot / cnt)
    diff = (jnp.concatenate(means, axis=-1) - uf[:, POOL_PAD:]).reshape(b, l, len(POOL_WINDOWS), POOL_GW)
    y = jnp.einsum("blgc,gcd->blgd", diff, lp["pool_w"].astype(jnp.float32)).reshape(b, l, W_POOL)
    return (y * lp["pool_scale"].astype(jnp.float32)).astype(u_ext.dtype)


def merge_branches(o_lat, y_ssm, y_pool, gates, lp):
    b, l = o_lat.shape[:2]
    o_attn = jnp.einsum("blhc,chd->blhd", o_lat, lp["w_uv"]).reshape(b, l, W_ATTN)
    wb = lp["w_branch"]
    m = (gates[:, :, 0] * (o_attn @ wb[:W_ATTN])
         + gates[:, :, 1] * (y_ssm @ wb[W_ATTN:W_ATTN + W_SSM])
         + gates[:, :, 2] * (y_pool @ wb[W_ATTN + W_SSM:]))
    return m @ lp["w_out"]


def token_mixer(h, pos, h0, pool_prefix, past_kv, lp):
    q_abs, q_pe, c, k_pe, u_ssm, u_pool, gates = mixer_inputs(h, pos, lp)
    if past_kv is None:
        o_lat = mla_prompt_attention(q_abs, q_pe, c, k_pe)
    else:
        o_lat = mla_sample_attention(q_abs, q_pe, c, k_pe, past_kv[0], past_kv[1])
    y_ssm, h_last = ssm_branch(u_ssm, h0, lp)
    u_ext = jnp.concatenate([pool_prefix.astype(u_pool.dtype), u_pool], axis=1)
    y_pool = pool_branch(u_ext, pos, lp)
    delta = merge_branches(o_lat, y_ssm, y_pool, gates, lp)
    return delta, c, k_pe, h_last, u_ext[:, -POOL_PAD:]


def setup_inputs(seed: int = 0) -> dict:
    key = jax.random.key(seed)
    ks = iter(jax.random.split(key, 64))
    f32 = jnp.float32

    def nrm(shape, scale):
        return jax.random.normal(next(ks), shape, f32) * scale

    def gain(shape):
        return 1.0 + 0.05 * jax.random.normal(next(ks), shape, f32)

    n_pages = PAST_LEN // PAGE_SIZE
    n_used = DEC_BATCH * n_pages
    n_phys = n_used + n_used // 4
    G, P, N = SSM_GROUPS, SSM_STATE, SSM_GROUP
    x_prompt = nrm((BATCH, SEQ, D_MODEL), 1.0)
    x_sample = nrm((DEC_BATCH, DEC_SEQ, D_MODEL), 1.0)
    cache_ckv = nrm((DEPTH, n_phys, PAGE_SIZE, KV_LORA), 1.0)
    cache_kpe = nrm((DEPTH, n_phys, PAGE_SIZE, QK_ROPE), 1.0)
    state_ssm = nrm((DEPTH, DEC_BATCH, G, P, 2), 0.1)
    state_pool = nrm((DEPTH, DEC_BATCH, POOL_PAD, W_POOL), 1.0)
    page_table = jax.random.permutation(next(ks), n_phys)[:n_used].reshape(DEC_BATCH, n_pages).astype(jnp.int32)
    attn_norm_g = gain((DEPTH, D_MODEL))
    w_in = nrm((DEPTH, D_MODEL, IN_COLS), D_MODEL ** -0.5)
    q_norm_g = gain((DEPTH, Q_LORA))
    w_q_up = nrm((DEPTH, Q_LORA, N_HEADS * (QK_NOPE + QK_ROPE)), Q_LORA ** -0.5)
    kv_norm_g = gain((DEPTH, KV_LORA))
    w_uk = nrm((DEPTH, KV_LORA, N_HEADS, QK_NOPE), KV_LORA ** -0.5)
    w_uv = nrm((DEPTH, KV_LORA, N_HEADS, V_HEAD), KV_LORA ** -0.5)
    n_idx = jnp.arange(P, dtype=f32)
    ssm_lam_re = -0.5 + nrm((DEPTH, G, P), 0.01)
    ssm_lam_im = jnp.pi * n_idx + nrm((DEPTH, G, P), 0.01)
    ssm_log_dt = jax.random.uniform(next(ks), (DEPTH, G), f32, float(np.log(DT_MIN)), float(np.log(DT_MAX)))
    ssm_b_re = nrm((DEPTH, G, P, N), (2.0 * N) ** -0.5)
    ssm_b_im = nrm((DEPTH, G, P, N), (2.0 * N) ** -0.5)
    ssm_c_re = nrm((DEPTH, G, N, P), P ** -0.5)
    ssm_c_im = nrm((DEPTH, G, N, P), P ** -0.5)
    ssm_d = nrm((DEPTH, W_SSM), 1.0)
    ssm_w_glu = nrm((DEPTH, W_SSM, W_SSM), W_SSM ** -0.5)
    pool_w = nrm((DEPTH, len(POOL_WINDOWS), POOL_GW, POOL_GW), POOL_GW ** -0.5)
    pool_scale = gain((DEPTH, W_POOL))
    w_branch = jnp.concatenate([nrm((DEPTH, W_ATTN, D_MODEL), W_ATTN ** -0.5),
                                nrm((DEPTH, W_SSM, D_MODEL), W_SSM ** -0.5),
                                nrm((DEPTH, W_POOL, D_MODEL), W_POOL ** -0.5)], axis=1)
    w_out = nrm((DEPTH, D_MODEL, D_MODEL), D_MODEL ** -0.5)
    ffn_norm_g = gain((DEPTH, D_MODEL))
    ffn_w1 = nrm((N_DENSE, D_MODEL, D_FF), D_MODEL ** -0.5)
    ffn_w3 = nrm((N_DENSE, D_MODEL, D_FF), D_MODEL ** -0.5)
    ffn_w2 = nrm((N_DENSE, D_FF, D_MODEL), D_FF ** -0.5)
    moe_router = nrm((N_MOE, D_MODEL, N_EXPERTS), D_MODEL ** -0.5)
    moe_w1 = nrm((N_MOE, N_EXPERTS, D_MODEL, D_FF), D_MODEL ** -0.5)
    moe_w3 = nrm((N_MOE, N_EXPERTS, D_MODEL, D_FF), D_MODEL ** -0.5)
    moe_w2 = nrm((N_MOE, N_EXPERTS, D_FF, D_MODEL), D_FF ** -0.5)
    final_norm_g = gain((D_MODEL,))
    return {"x_prompt": x_prompt, "x_sample": x_sample, "cache_ckv": cache_ckv, "cache_kpe": cache_kpe,
            "state_ssm": state_ssm, "state_pool": state_pool, "page_table": page_table,
            "attn_norm_g": attn_norm_g, "w_in": w_in, "q_norm_g": q_norm_g, "w_q_up": w_q_up,
            "kv_norm_g": kv_norm_g, "w_uk": w_uk, "w_uv": w_uv,
            "ssm_lam_re": ssm_lam_re, "ssm_lam_im": ssm_lam_im, "ssm_log_dt": ssm_log_dt,
            "ssm_b_re": ssm_b_re, "ssm_b_im": ssm_b_im, "ssm_c_re": ssm_c_re, "ssm_c_im": ssm_c_im,
            "ssm_d": ssm_d, "ssm_w_glu": ssm_w_glu, "pool_w": pool_w, "pool_scale": pool_scale,
            "w_branch": w_branch, "w_out": w_out, "ffn_norm_g": ffn_norm_g,
            "ffn_w1": ffn_w1, "ffn_w3": ffn_w3, "ffn_w2": ffn_w2,
            "moe_router": moe_router, "moe_w1": moe_w1, "moe_w3": moe_w3, "moe_w2": moe_w2,
            "final_norm_g": final_norm_g}


def reference(x_prompt, x_sample, cache_ckv, cache_kpe, state_ssm, state_pool, page_table,
              attn_norm_g, w_in, q_norm_g, w_q_up, kv_norm_g, w_uk, w_uv,
              ssm_lam_re, ssm_lam_im, ssm_log_dt, ssm_b_re, ssm_b_im, ssm_c_re, ssm_c_im,
              ssm_d, ssm_w_glu, pool_w, pool_scale, w_branch, w_out, ffn_norm_g,
              ffn_w1, ffn_w3, ffn_w2, moe_router, moe_w1, moe_w3, moe_w2, final_norm_g):
    b, s = x_prompt.shape[:2]
    db, t = x_sample.shape[:2]
    past = page_table.shape[1] * PAGE_SIZE
    pos_p = jnp.arange(s)
    pos_s = past + jnp.arange(t)
    xp, xs = x_prompt, x_sample
    ckv_p, kpe_p, ssm_p, pool_p = [], [], [], []
    ckv_s, kpe_s, ssm_s, pool_s = [], [], [], []
    for l in range(DEPTH):
        lp = {"w_in": w_in[l], "q_norm_g": q_norm_g[l], "w_q_up": w_q_up[l], "kv_norm_g": kv_norm_g[l],
              "w_uk": w_uk[l], "w_uv": w_uv[l], "lam_re": ssm_lam_re[l], "lam_im": ssm_lam_im[l],
              "log_dt": ssm_log_dt[l], "b_re": ssm_b_re[l], "b_im": ssm_b_im[l], "c_re": ssm_c_re[l],
              "c_im": ssm_c_im[l], "d_skip": ssm_d[l], "w_glu": ssm_w_glu[l], "pool_w": pool_w[l],
              "pool_scale": pool_scale[l], "w_branch": w_branch[l], "w_out": w_out[l]}
        h0_p = jnp.zeros((b, SSM_GROUPS, SSM_STATE), jnp.complex64)
        pre_p = jnp.zeros((b, POOL_PAD, W_POOL), xp.dtype)
        d_p, c_p, k_p, hl_p, pl_p = token_mixer(rms_norm(xp, attn_norm_g[l]), pos_p, h0_p, pre_p, None, lp)
        xp = xp + d_p
        c_past = cache_ckv[l, page_table].reshape(db, past, KV_LORA)
        k_past = cache_kpe[l, page_table].reshape(db, past, QK_ROPE)
        st = state_ssm[l].astype(jnp.float32)
        h0_s = lax.complex(st[..., 0], st[..., 1])
        d_s, c_s, k_s, hl_s, pl_s = token_mixer(rms_norm(xs, attn_norm_g[l]), pos_s, h0_s, state_pool[l],
                                                (c_past, k_past), lp)
        xs = xs + d_s
        ckv_p.append(c_p)
        kpe_p.append(k_p)
        ssm_p.append(jnp.stack([hl_p.real, hl_p.imag], axis=-1).astype(state_ssm.dtype))
        pool_p.append(pl_p)
        ckv_s.append(c_s)
        kpe_s.append(k_s)
        ssm_s.append(jnp.stack([hl_s.real, hl_s.imag], axis=-1).astype(state_ssm.dtype))
        pool_s.append(pl_s)
        n_p = b * s
        flat = jnp.concatenate([xp.reshape(-1, D_MODEL), xs.reshape(-1, D_MODEL)], axis=0)
        hf = rms_norm(flat, ffn_norm_g[l])
        j = l // 2
        if l % 2 == 0:
            f = swiglu(hf, ffn_w1[j], ffn_w3[j], ffn_w2[j])
        else:
            f = moe_swiglu(hf, moe_router[j], moe_w1[j], moe_w3[j], moe_w2[j])
        xp = xp + f[:n_p].reshape(xp.shape)
        xs = xs + f[n_p:].reshape(xs.shape)
    y_prompt = rms_norm(xp, final_norm_g)
    y_sample = rms_norm(xs, final_norm_g)
    return (y_prompt, y_sample,
            jnp.stack(ckv_p), jnp.stack(kpe_p), jnp.stack(ssm_p), jnp.stack(pool_p),
            jnp.stack(ckv_s), jnp.stack(kpe_s), jnp.stack(ssm_s), jnp.stack(pool_s))
```

```python
import functools

import jax
import jax.numpy as jnp
import numpy as np
from jax import lax
from jax.experimental import pallas as pl
from jax.experimental.pallas import tpu as pltpu

F32 = jnp.float32
BF16 = jnp.bfloat16

D_MODEL = 2048
N_HEADS = 16
QK_NOPE = 64
QK_ROPE = 32
V_HEAD = 64
W_ATTN = N_HEADS * V_HEAD
Q_LORA = 512
KV_LORA = 256
ROPE_BASE = 10000.0
SM_SCALE = (QK_NOPE + QK_ROPE) ** -0.5
PAGE_SIZE = 128
W_SSM = 512
SSM_GROUP = 16
SSM_GROUPS = W_SSM // SSM_GROUP
SSM_STATE = 64
W_POOL = 512
POOL_WINDOWS = (2, 4, 8, 16)
POOL_GW = W_POOL // len(POOL_WINDOWS)
POOL_PAD = max(POOL_WINDOWS) - 1
N_BRANCH = 3
D_FF = 5632
N_EXPERTS = 8
TOP_K = 2
EPS = 1e-6

LANES = 128
SUBLANES = 8
VMEM_LIMIT_BYTES = 56 * 1024 * 1024

ROPE_LANES = LANES
Z_GATE = 0
Z_QC = Z_GATE + N_BRANCH * D_MODEL
Z_SSM = Z_QC + Q_LORA
Z_POOL = Z_SSM + W_SSM
Z_KV = Z_POOL + W_POOL
Z_KPE = Z_KV + KV_LORA
Z_COLS = Z_KPE + 2 * ROPE_LANES
QCAT = KV_LORA + ROPE_LANES
HEAD_GROUP = 4

SSM_CHUNK = 16
SSM_CW = SSM_CHUNK * SSM_GROUP


def _cparams(sem, vmem=VMEM_LIMIT_BYTES):
    return pltpu.CompilerParams(dimension_semantics=sem, vmem_limit_bytes=vmem)


def _lane_tile(x, n):
    return x if n == 1 else jnp.concatenate([x] * n, axis=1)


def _rms(x, g):
    ms = jnp.mean(x * x, axis=-1, keepdims=True)
    return x * lax.rsqrt(ms + EPS) * g


def _in_proj_kernel(x_ref, g_ref, w_ref, z_ref, h_ref, *, n_gate_tiles):
    j = pl.program_id(1)

    @pl.when(j == 0)
    def _():
        h_ref[...] = _rms(x_ref[...], g_ref[...]).astype(BF16)

    acc = jnp.dot(h_ref[...], w_ref[...], preferred_element_type=F32)

    @pl.when(j < n_gate_tiles)
    def _():
        z_ref[...] = jax.nn.sigmoid(acc)

    @pl.when(j >= n_gate_tiles)
    def _():
        z_ref[...] = acc


def in_proj(x, gain, w, *, tm, tn):
    m, d = x.shape
    n = w.shape[1]
    assert m % tm == 0 and n % tn == 0 and (N_BRANCH * D_MODEL) % tn == 0
    return pl.pallas_call(
        functools.partial(_in_proj_kernel, n_gate_tiles=(N_BRANCH * D_MODEL) // tn),
        out_shape=jax.ShapeDtypeStruct((m, n), F32),
        grid=(m // tm, n // tn),
        in_specs=[pl.BlockSpec((tm, d), lambda i, j: (i, 0)),
                  pl.BlockSpec((1, d), lambda i, j: (0, 0)),
                  pl.BlockSpec((d, tn), lambda i, j: (0, j))],
        out_specs=pl.BlockSpec((tm, tn), lambda i, j: (i, j)),
        scratch_shapes=[pltpu.VMEM((tm, d), BF16)],
        compiler_params=_cparams(("parallel", "arbitrary")),
        name="in_proj",
    )(x, gain, w)


def _qkv_kernel(qc_ref, kv_ref, kpe_ref, cos_ref, sin_ref, qg_ref, kvg_ref, wq_ref, wuk_ref,
                qcat_ref, c_ref, kpe_out_ref, kcat_ref):
    nope_w = N_HEADS * QK_NOPE
    rope_w = N_HEADS * ROPE_LANES
    cos = cos_ref[...]
    sin = sin_ref[...]
    qn = _rms(qc_ref[...], qg_ref[...]).astype(BF16)
    q = jnp.dot(qn, wq_ref[...], preferred_element_type=F32)
    cos_h = jnp.concatenate([cos] * N_HEADS, axis=1)
    sin_h = jnp.concatenate([sin] * N_HEADS, axis=1)
    q_pe = (q[:, nope_w:nope_w + rope_w] * cos_h
            + q[:, nope_w + rope_w:nope_w + 2 * rope_w] * sin_h).astype(BF16)
    q_nope = q[:, :nope_w].astype(BF16)
    gw = HEAD_GROUP * QK_NOPE
    for g in range(N_HEADS // HEAD_GROUP):
        qa = jnp.dot(q_nope[:, g * gw:(g + 1) * gw], wuk_ref[g],
                     preferred_element_type=F32).astype(BF16)
        for hl in range(HEAD_GROUP):
            h = g * HEAD_GROUP + hl
            qcat_ref[:, h * QCAT:h * QCAT + KV_LORA] = qa[:, hl * KV_LORA:(hl + 1) * KV_LORA]
            qcat_ref[:, h * QCAT + KV_LORA:(h + 1) * QCAT] = q_pe[:, h * ROPE_LANES:(h + 1) * ROPE_LANES]
    c = _rms(kv_ref[...], kvg_ref[...])
    c_ref[...] = c
    kraw = kpe_ref[...]
    kr = kraw[:, :ROPE_LANES] * cos + kraw[:, ROPE_LANES:] * sin
    kpe_out_ref[...] = kr[:, :QK_ROPE]
    kcat_ref[:, :KV_LORA] = c.astype(BF16)
    kcat_ref[:, KV_LORA:] = kr.astype(BF16)


def qkv_post(z, cos_t, sin_t, q_g, kv_g, wq, wuk, *, tm):
    m = z.shape[0]
    assert m % tm == 0
    nq = wq.shape[1]
    return pl.pallas_call(
        _qkv_kernel,
        out_shape=(jax.ShapeDtypeStruct((m, N_HEADS * QCAT), BF16),
                   jax.ShapeDtypeStruct((m, KV_LORA), F32),
                   jax.ShapeDtypeStruct((m, QK_ROPE), F32),
                   jax.ShapeDtypeStruct((m, QCAT), BF16)),
        grid=(m // tm,),
        in_specs=[pl.BlockSpec((tm, Q_LORA), lambda i: (i, Z_QC // Q_LORA)),
                  pl.BlockSpec((tm, KV_LORA), lambda i: (i, Z_KV // KV_LORA)),
                  pl.BlockSpec((tm, 2 * ROPE_LANES), lambda i: (i, Z_KPE // (2 * ROPE_LANES))),
                  pl.BlockSpec((tm, ROPE_LANES), lambda i: (i, 0)),
                  pl.BlockSpec((tm, ROPE_LANES), lambda i: (i, 0)),
                  pl.BlockSpec((1, Q_LORA), lambda i: (0, 0)),
                  pl.BlockSpec((1, KV_LORA), lambda i: (0, 0)),
                  pl.BlockSpec((Q_LORA, nq), lambda i: (0, 0)),
                  pl.BlockSpec(wuk.shape, lambda i: (0, 0, 0))],
        out_specs=(pl.BlockSpec((tm, N_HEADS * QCAT), lambda i: (i, 0)),
                   pl.BlockSpec((tm, KV_LORA), lambda i: (i, 0)),
                   pl.BlockSpec((tm, QK_ROPE), lambda i: (i, 0)),
                   pl.BlockSpec((tm, QCAT), lambda i: (i, 0))),
        compiler_params=_cparams(("parallel",)),
        name="qkv_post",
    )(z, z, z, cos_t, sin_t, q_g, kv_g, wq, wuk)


def _attn_out_proj(o, wuv_ref, rows):
    ob = o.astype(BF16)
    outs = []
    for g in range(N_HEADS // HEAD_GROUP):
        wide = jnp.concatenate(
            [ob[(g * HEAD_GROUP + hl) * rows:(g * HEAD_GROUP + hl + 1) * rows, :]
             for hl in range(HEAD_GROUP)], axis=1)
        outs.append(jnp.dot(wide, wuv_ref[g], preferred_element_type=F32))
    return jnp.concatenate(outs, axis=1)


def _prompt_attn_kernel(q_ref, k_ref, wuv_ref, o_ref, m_ref, l_ref, acc_ref, *, tq, tk):
    i = pl.program_id(1)
    rows = N_HEADS * tq
    q = jnp.concatenate([q_ref[:, h * QCAT:(h + 1) * QCAT] for h in range(N_HEADS)], axis=0)
    m_ref[...] = jnp.full(m_ref.shape, -jnp.inf, F32)
    l_ref[...] = jnp.zeros(l_ref.shape, F32)
    acc_ref[...] = jnp.zeros(acc_ref.shape, F32)

    def chunk(j, masked):
        start = pl.multiple_of(j * tk, tk)
        k = k_ref[pl.ds(start, tk), :]
        s = lax.dot_general(q, k, (((1,), (1,)), ((), ())), preferred_element_type=F32) * SM_SCALE
        if masked:
            qpos = i * tq + (lax.broadcasted_iota(jnp.int32, (rows, tk), 0) & (tq - 1))
            kpos = start + lax.broadcasted_iota(jnp.int32, (rows, tk), 1)
            s = jnp.where(kpos <= qpos, s, -jnp.inf)
        m_old = m_ref[...]
        m_new = jnp.maximum(m_old, jnp.max(s, axis=1, keepdims=True))
        alpha = jnp.exp(m_old - m_new)
        p = jnp.exp(s - _lane_tile(m_new, tk // LANES))
        l_ref[...] = alpha * l_ref[...] + jnp.sum(p, axis=1, keepdims=True)
        acc_ref[...] = (_lane_tile(alpha, KV_LORA // LANES) * acc_ref[...]
                        + jnp.dot(p.astype(BF16), k[:, :KV_LORA], preferred_element_type=F32))
        m_ref[...] = m_new

    n_full = (i * tq) // tk

    def full_body(j, carry):
        chunk(j, False)
        return carry

    lax.fori_loop(0, n_full, full_body, 0)
    chunk(n_full, True)
    o = acc_ref[...] / _lane_tile(l_ref[...], KV_LORA // LANES)
    o_ref[...] = _attn_out_proj(o, wuv_ref, tq).astype(o_ref.dtype)


def prompt_attention(qcat, kcat, wuv, *, batch, seq, tq, tk):
    assert seq % tq == 0 and seq % tk == 0 and tk % tq == 0 and (tq & (tq - 1)) == 0
    nq = seq // tq
    rows = N_HEADS * tq
    return pl.pallas_call(
        functools.partial(_prompt_attn_kernel, tq=tq, tk=tk),
        out_shape=jax.ShapeDtypeStruct((batch * seq, W_ATTN), BF16),
        grid=(batch, nq),
        in_specs=[pl.BlockSpec((tq, N_HEADS * QCAT), lambda b, i: (b * nq + i, 0)),
                  pl.BlockSpec((seq, QCAT), lambda b, i: (b, 0)),
                  pl.BlockSpec(wuv.shape, lambda b, i: (0, 0, 0))],
        out_specs=pl.BlockSpec((tq, W_ATTN), lambda b, i: (b * nq + i, 0)),
        scratch_shapes=[pltpu.VMEM((rows, LANES), F32), pltpu.VMEM((rows, LANES), F32),
                        pltpu.VMEM((rows, KV_LORA), F32)],
        compiler_params=_cparams(("parallel", "arbitrary")),
        name="prompt_attn",
    )(qcat, kcat, wuv)


Q_ROWS = SUBLANES
NEW_KEYS = LANES


def _sample_attn_kernel(pt_ref, q_ref, knew_ref, ckv_hbm, kpe_hbm, wuv_ref, o_ref,
                        ckv_buf, kpe_buf, sem, q_s, m_ref, l_ref, acc_ref,
                        *, layer, n_chunks, pc, t):
    b = pl.program_id(0)
    c = pl.program_id(1)
    total = pl.num_programs(0) * n_chunks
    step = b * n_chunks + c
    slot = step % 2

    def copies(bb, cc, sl):
        out = []
        for p in range(pc):
            page = pt_ref[bb, cc * pc + p]
            out.append(pltpu.make_async_copy(
                ckv_hbm.at[layer, page], ckv_buf.at[sl, pl.ds(p * PAGE_SIZE, PAGE_SIZE), :],
                sem.at[sl, 0]))
            out.append(pltpu.make_async_copy(
                kpe_hbm.at[layer, page], kpe_buf.at[sl, pl.ds(p * PAGE_SIZE, PAGE_SIZE), :],
                sem.at[sl, 1]))
        return out

    @pl.when(step == 0)
    def _():
        for cp in copies(b, c, slot):
            cp.start()

    @pl.when(step + 1 < total)
    def _():
        last = c + 1 == n_chunks
        nb = jnp.where(last, b + 1, b)
        nc = jnp.where(last, 0, c + 1)
        for cp in copies(nb, nc, 1 - slot):
            cp.start()

    @pl.when(c == 0)
    def _():
        q_s[...] = jnp.zeros(q_s.shape, F32)
        qb = q_ref[0].astype(F32)
        for h in range(N_HEADS):
            q_s[h * Q_ROWS:h * Q_ROWS + t, :] = qb[:, h * QCAT:(h + 1) * QCAT]
        m_ref[...] = jnp.full(m_ref.shape, -jnp.inf, F32)
        l_ref[...] = jnp.zeros(l_ref.shape, F32)
        acc_ref[...] = jnp.zeros(acc_ref.shape, F32)

    for cp in copies(b, c, slot):
        cp.wait()

    q = q_s[...].astype(BF16)
    n_keys = pc * PAGE_SIZE

    def update(s, v):
        m_old = m_ref[...]
        m_new = jnp.maximum(m_old, jnp.max(s, axis=1, keepdims=True))
        alpha = jnp.exp(m_old - m_new)
        p = jnp.exp(s - _lane_tile(m_new, s.shape[1] // LANES))
        l_ref[...] = alpha * l_ref[...] + jnp.sum(p, axis=1, keepdims=True)
        acc_ref[...] = (_lane_tile(alpha, KV_LORA // LANES) * acc_ref[...]
                        + jnp.dot(p.astype(BF16), v, preferred_element_type=F32))
        m_ref[...] = m_new

    ck = ckv_buf[slot].astype(BF16)
    kr = jnp.concatenate([kpe_buf[slot], jnp.zeros((n_keys, ROPE_LANES - QK_ROPE), F32)],
                         axis=1).astype(BF16)
    nt = (((1,), (1,)), ((), ()))
    s = (lax.dot_general(q[:, :KV_LORA], ck, nt, preferred_element_type=F32)
         + lax.dot_general(q[:, KV_LORA:], kr, nt, preferred_element_type=F32)) * SM_SCALE
    update(s, ck)

    @pl.when(c == n_chunks - 1)
    def _():
        kn = knew_ref[0]
        rows = N_HEADS * Q_ROWS
        sn = lax.dot_general(q, kn, nt, preferred_element_type=F32) * SM_SCALE
        qi = lax.broadcasted_iota(jnp.int32, (rows, NEW_KEYS), 0) & (Q_ROWS - 1)
        ki = lax.broadcasted_iota(jnp.int32, (rows, NEW_KEYS), 1)
        sn = jnp.where((ki <= qi) & (ki < t), sn, -jnp.inf)
        update(sn, kn[:, :KV_LORA])
        o = acc_ref[...] / _lane_tile(l_ref[...], KV_LORA // LANES)
        o_ref[0] = _attn_out_proj(o, wuv_ref, Q_ROWS)[:t, :].astype(o_ref.dtype)


def sample_attention(page_table, q3, knew3, cache_ckv, cache_kpe, wuv, *, layer, pc):
    db, n_pages = page_table.shape
    t = q3.shape[1]
    assert n_pages % pc == 0 and t <= Q_ROWS
    n_chunks = n_pages // pc
    n_keys = pc * PAGE_SIZE
    rows = N_HEADS * Q_ROWS
    grid_spec = pltpu.PrefetchScalarGridSpec(
        num_scalar_prefetch=1,
        grid=(db, n_chunks),
        in_specs=[pl.BlockSpec((1, t, N_HEADS * QCAT), lambda b, c, pt: (b, 0, 0)),
                  pl.BlockSpec((1, NEW_KEYS, QCAT), lambda b, c, pt: (b, 0, 0)),
                  pl.BlockSpec(memory_space=pl.ANY),
                  pl.BlockSpec(memory_space=pl.ANY),
                  pl.BlockSpec(wuv.shape, lambda b, c, pt: (0, 0, 0))],
        out_specs=pl.BlockSpec((1, t, W_ATTN), lambda b, c, pt: (b, 0, 0)),
        scratch_shapes=[pltpu.VMEM((2, n_keys, KV_LORA), F32),
                        pltpu.VMEM((2, n_keys, QK_ROPE), F32),
                        pltpu.SemaphoreType.DMA((2, 2)),
                        pltpu.VMEM((rows, QCAT), F32),
                        pltpu.VMEM((rows, LANES), F32),
                        pltpu.VMEM((rows, LANES), F32),
                        pltpu.VMEM((rows, KV_LORA), F32)])
    return pl.pallas_call(
        functools.partial(_sample_attn_kernel, layer=layer, n_chunks=n_chunks, pc=pc, t=t),
        out_shape=jax.ShapeDtypeStruct((db, t, W_ATTN), BF16),
        grid_spec=grid_spec,
        compiler_params=_cparams(("arbitrary", "arbitrary")),
        name="sample_attn",
    )(page_table, q3, knew3, cache_ckv, cache_kpe, wuv)


SSM_PAIR_U = 2 * SSM_CW
SSM_PAIR_H = 4 * SSM_STATE


def _ssm_kernel(u_ref, toep_ref, wsum_ref, wout_ref, a_ref, h0_ref, y_ref, hfin_ref,
                s_scr, hp_scr, *, nk, rb):
    u = u_ref[0]
    s_scr[...] = jnp.dot(u, wsum_ref[0], preferred_element_type=F32)
    a = a_ref[0]
    a_re = jnp.broadcast_to(a[0:1, :], (rb, LANES))
    a_im = jnp.broadcast_to(a[1:2, :], (rb, LANES))
    h0 = h0_ref[...]

    def body(k, carry):
        h_re, h_im = carry
        r0 = pl.multiple_of(k * rb, rb)
        hp_scr[pl.ds(r0, rb), :] = jnp.concatenate([h_re, h_im], axis=1)
        sk = s_scr[pl.ds(r0, rb), :]
        n_re = a_re * h_re - a_im * h_im + sk[:, :LANES]
        n_im = a_re * h_im + a_im * h_re + sk[:, LANES:]
        return n_re, n_im

    h_re, h_im = lax.fori_loop(0, nk, body, (h0[:, :LANES], h0[:, LANES:]))
    hfin_ref[...] = jnp.concatenate([h_re, h_im], axis=1)
    hp = hp_scr[...].astype(BF16)
    y_ref[0] = (jnp.concatenate(
        [jnp.dot(u[:, :SSM_CW], toep_ref[0], preferred_element_type=F32),
         jnp.dot(u[:, SSM_CW:], toep_ref[1], preferred_element_type=F32)], axis=1)
        + jnp.dot(hp, wout_ref[0], preferred_element_type=F32))


def ssm_scan(u_pairs, toep, wsum, wout, a, h0, *, nk, rb):
    n_pairs, r, _ = u_pairs.shape
    assert r == nk * rb and rb % SUBLANES == 0
    return pl.pallas_call(
        functools.partial(_ssm_kernel, nk=nk, rb=rb),
        out_shape=(jax.ShapeDtypeStruct((n_pairs, r, SSM_PAIR_U), F32),
                   jax.ShapeDtypeStruct((rb, n_pairs * SSM_PAIR_H), F32)),
        grid=(n_pairs,),
        in_specs=[pl.BlockSpec((1, r, SSM_PAIR_U), lambda j: (j, 0, 0)),
                  pl.BlockSpec((2, SSM_CW, SSM_CW), lambda j: (j, 0, 0)),
                  pl.BlockSpec((1, SSM_PAIR_U, SSM_PAIR_H), lambda j: (j, 0, 0)),
                  pl.BlockSpec((1, SSM_PAIR_H, SSM_PAIR_U), lambda j: (j, 0, 0)),
                  pl.BlockSpec((1, SUBLANES, LANES), lambda j: (j, 0, 0)),
                  pl.BlockSpec((rb, SSM_PAIR_H), lambda j: (0, j))],
        out_specs=(pl.BlockSpec((1, r, SSM_PAIR_U), lambda j: (j, 0, 0)),
                   pl.BlockSpec((rb, SSM_PAIR_H), lambda j: (0, j))),
        scratch_shapes=[pltpu.VMEM((r, SSM_PAIR_H), F32), pltpu.VMEM((r, SSM_PAIR_H), F32)],
        compiler_params=_cparams(("parallel",)),
        name="ssm_scan",
    )(u_pairs, toep, wsum, wout, a, h0)


def _gelu_tanh(x):
    return 0.5 * x * (1.0 + jnp.tanh(np.sqrt(2.0 / np.pi).astype(np.float32)
                                     * (x + np.float32(0.044715) * (x * x * x))))


def _ssm_post_kernel(y_ref, u_ref, d_ref, wg_ref, o_ref):
    y = y_ref[...] + d_ref[...] * u_ref[...]
    zz = _gelu_tanh(y)
    gate = jax.nn.sigmoid(jnp.dot(zz.astype(BF16), wg_ref[...], preferred_element_type=F32))
    o_ref[...] = (zz * gate).astype(o_ref.dtype)


def ssm_post(y_raw, z, d_skip, w_glu, *, tm):
    m = y_raw.shape[0]
    assert m % tm == 0
    return pl.pallas_call(
        _ssm_post_kernel,
        out_shape=jax.ShapeDtypeStruct((m, W_SSM), BF16),
        grid=(m // tm,),
        in_specs=[pl.BlockSpec((tm, W_SSM), lambda i: (i, 0)),
                  pl.BlockSpec((tm, W_SSM), lambda i: (i, Z_SSM // W_SSM)),
                  pl.BlockSpec((1, W_SSM), lambda i: (0, 0)),
                  pl.BlockSpec((W_SSM, W_SSM), lambda i: (0, 0))],
        out_specs=pl.BlockSpec((tm, W_SSM), lambda i: (i, 0)),
        compiler_params=_cparams(("parallel",)),
        name="ssm_post",
    )(y_raw, z, d_skip, w_glu)


POOL_HALO = 2 * SUBLANES


def _pool_mix(diff, pw_ref, ps_ref):
    outs = [jnp.dot(diff[:, g * POOL_GW:(g + 1) * POOL_GW].astype(BF16), pw_ref[g],
                    preferred_element_type=F32) for g in range(len(POOL_WINDOWS))]
    return jnp.concatenate(outs, axis=1) * ps_ref[...]


def _pool_prompt_kernel(u_ref, pw_ref, ps_ref, o_ref, ext):
    seq = u_ref.shape[0]
    ext[0:POOL_HALO, :] = jnp.zeros((POOL_HALO, W_POOL), F32)
    ext[POOL_HALO:, :] = u_ref[...]
    pos1 = lax.broadcasted_iota(jnp.int32, (seq, POOL_GW), 0) + 1
    means = []
    for g, w in enumerate(POOL_WINDOWS):
        tot = ext[POOL_HALO:POOL_HALO + seq, g * POOL_GW:(g + 1) * POOL_GW]
        for k in range(1, w):
            tot = tot + ext[POOL_HALO - k:POOL_HALO - k + seq, g * POOL_GW:(g + 1) * POOL_GW]
        means.append(tot / jnp.minimum(pos1, w).astype(F32))
    diff = jnp.concatenate(means, axis=1) - u_ref[...]
    o_ref[...] = _pool_mix(diff, pw_ref, ps_ref).astype(o_ref.dtype)


def pool_prompt(z, pool_w, pool_scale, *, batch, seq):
    return pl.pallas_call(
        _pool_prompt_kernel,
        out_shape=jax.ShapeDtypeStruct((batch * seq, W_POOL), BF16),
        grid=(batch,),
        in_specs=[pl.BlockSpec((seq, W_POOL), lambda b: (b, Z_POOL // W_POOL)),
                  pl.BlockSpec(pool_w.shape, lambda b: (0, 0, 0)),
                  pl.BlockSpec((1, W_POOL), lambda b: (0, 0))],
        out_specs=pl.BlockSpec((seq, W_POOL), lambda b: (b, 0)),
        scratch_shapes=[pltpu.VMEM((POOL_HALO + seq, W_POOL), F32)],
        compiler_params=_cparams(("parallel",)),
        name="pool_prompt",
    )(z, pool_w, pool_scale)


def _pool_sample_kernel(x_ref, cnt_ref, pw_ref, ps_ref, o_ref, *, t):
    for tt in range(t):
        means = []
        for g, w in enumerate(POOL_WINDOWS):
            sl = slice(g * POOL_GW, (g + 1) * POOL_GW)
            tot = x_ref[POOL_PAD + tt][:, sl]
            for k in range(1, w):
                tot = tot + x_ref[POOL_PAD + tt - k][:, sl]
            means.append(tot / cnt_ref[tt][:, sl])
        diff = jnp.concatenate(means, axis=1) - x_ref[POOL_PAD + tt]
        o_ref[tt] = _pool_mix(diff, pw_ref, ps_ref).astype(o_ref.dtype)


def pool_sample(x_tm, cnt, pool_w, pool_scale):
    rows, nseq, _ = x_tm.shape
    t = rows - POOL_PAD
    return pl.pallas_call(
        functools.partial(_pool_sample_kernel, t=t),
        out_shape=jax.ShapeDtypeStruct((t, nseq, W_POOL), BF16),
        grid=(1,),
        in_specs=[pl.BlockSpec(x_tm.shape, lambda i: (0, 0, 0)),
                  pl.BlockSpec(cnt.shape, lambda i: (0, 0, 0)),
                  pl.BlockSpec(pool_w.shape, lambda i: (0, 0, 0)),
                  pl.BlockSpec((1, W_POOL), lambda i: (0, 0))],
        out_specs=pl.BlockSpec((t, nseq, W_POOL), lambda i: (0, 0, 0)),
        compiler_params=_cparams(("arbitrary",)),
        name="pool_sample",
    )(x_tm, cnt, pool_w, pool_scale)


def _merge_kernel(oa_ref, ys_ref, yp_ref, g0_ref, g1_ref, g2_ref, wa_ref, ws_ref, wp_ref, m_ref):
    m = (g0_ref[...] * jnp.dot(oa_ref[...], wa_ref[...], preferred_element_type=F32)
         + g1_ref[...] * jnp.dot(ys_ref[...], ws_ref[...], preferred_element_type=F32)
         + g2_ref[...] * jnp.dot(yp_ref[...], wp_ref[...], preferred_element_type=F32))
    m_ref[...] = m.astype(m_ref.dtype)


def branch_merge(o_attn, y_ssm, y_pool, z, w_branch, *, tm, tn):
    m = o_attn.shape[0]
    assert m % tm == 0 and D_MODEL % tn == 0
    nj = D_MODEL // tn
    return pl.pallas_call(
        _merge_kernel,
        out_shape=jax.ShapeDtypeStruct((m, D_MODEL), BF16),
        grid=(m // tm, nj),
        in_specs=[pl.BlockSpec((tm, W_ATTN), lambda i, j: (i, 0)),
                  pl.BlockSpec((tm, W_SSM), lambda i, j: (i, 0)),
                  pl.BlockSpec((tm, W_POOL), lambda i, j: (i, 0)),
                  pl.BlockSpec((tm, tn), lambda i, j: (i, j)),
                  pl.BlockSpec((tm, tn), lambda i, j: (i, nj + j)),
                  pl.BlockSpec((tm, tn), lambda i, j: (i, 2 * nj + j)),
                  pl.BlockSpec((W_ATTN, tn), lambda i, j: (0, j)),
                  pl.BlockSpec((W_SSM, tn), lambda i, j: (W_ATTN // W_SSM, j)),
                  pl.BlockSpec((W_POOL, tn), lambda i, j: ((W_ATTN + W_SSM) // W_POOL, j))],
        out_specs=pl.BlockSpec((tm, tn), lambda i, j: (i, j)),
        compiler_params=_cparams(("parallel", "arbitrary")),
        name="branch_merge",
    )(o_attn, y_ssm, y_pool, z, z, z, w_branch, w_branch, w_branch)


ROUTE_LANES = LANES


def _route(hf, wr_ref):
    logits = jnp.dot(hf, wr_ref[...], precision=lax.Precision.HIGHEST, preferred_element_type=F32)
    lane = lax.broadcasted_iota(jnp.int32, logits.shape, 1)
    lg = jnp.where(lane < N_EXPERTS, logits, -jnp.inf)
    m1 = jnp.max(lg, axis=1, keepdims=True)
    i1 = jnp.min(jnp.where(lg == m1, lane, ROUTE_LANES), axis=1, keepdims=True)
    lg2 = jnp.where(lane == i1, -jnp.inf, lg)
    m2 = jnp.max(lg2, axis=1, keepdims=True)
    i2 = jnp.min(jnp.where(lg2 == m2, lane, ROUTE_LANES), axis=1, keepdims=True)
    e = jnp.exp(m2 - m1)
    g1 = 1.0 / (1.0 + e)
    g2 = e / (1.0 + e)
    return jnp.where(lane == 0, i1.astype(F32),
                     jnp.where(lane == 1, i2.astype(F32),
                               jnp.where(lane == 2, g1, jnp.where(lane == 3, g2, 0.0))))


def _out_proj_kernel(m_ref, w_ref, x_ref, g_ref, wr_ref, xn_ref, hf_ref, rt_ref):
    xn = x_ref[...] + jnp.dot(m_ref[...], w_ref[...], preferred_element_type=F32)
    xn_ref[...] = xn
    hf = _rms(xn, g_ref[...])
    hf_ref[...] = hf
    rt_ref[...] = _route(hf, wr_ref)


def out_proj(mm, w_out, x, gain, w_router, *, tm):
    m = x.shape[0]
    assert m % tm == 0
    return pl.pallas_call(
        _out_proj_kernel,
        out_shape=(jax.ShapeDtypeStruct((m, D_MODEL), F32),
                   jax.ShapeDtypeStruct((m, D_MODEL), F32),
                   jax.ShapeDtypeStruct((m, ROUTE_LANES), F32)),
        grid=(m // tm,),
        in_specs=[pl.BlockSpec((tm, D_MODEL), lambda i: (i, 0)),
                  pl.BlockSpec((D_MODEL, D_MODEL), lambda i: (0, 0)),
                  pl.BlockSpec((tm, D_MODEL), lambda i: (i, 0)),
                  pl.BlockSpec((1, D_MODEL), lambda i: (0, 0)),
                  pl.BlockSpec((D_MODEL, ROUTE_LANES), lambda i: (0, 0))],
        out_specs=(pl.BlockSpec((tm, D_MODEL), lambda i: (i, 0)),
                   pl.BlockSpec((tm, D_MODEL), lambda i: (i, 0)),
                   pl.BlockSpec((tm, ROUTE_LANES), lambda i: (i, 0))),
        compiler_params=_cparams(("parallel",)),
        name="out_proj",
    )(mm, w_out, x, gain, w_router)


def _ffn_up_kernel(te_ref, nu_ref, x_ref, w1_ref, w3_ref, g_ref):
    i = pl.program_id(1)

    @pl.when(i < nu_ref[0])
    def _():
        x = x_ref[...].astype(BF16)
        a = jnp.dot(x, w1_ref[0], preferred_element_type=F32)
        b = jnp.dot(x, w3_ref[0], preferred_element_type=F32)
        g_ref[...] = (a * jax.nn.sigmoid(a) * b).astype(g_ref.dtype)

    @pl.when(i >= nu_ref[0])
    def _():
        g_ref[...] = jnp.zeros(g_ref.shape, g_ref.dtype)


def ffn_up(tile_expert, n_used, xs, w1, w3, *, tm, tf):
    r = xs.shape[0]
    assert r % tm == 0 and D_FF % tf == 0
    n_tiles = r // tm

    def row(i, nu):
        return jnp.minimum(i, nu[0] - 1)

    grid_spec = pltpu.PrefetchScalarGridSpec(
        num_scalar_prefetch=2,
        grid=(D_FF // tf, n_tiles),
        in_specs=[pl.BlockSpec((tm, D_MODEL), lambda j, i, te, nu: (row(i, nu), 0)),
                  pl.BlockSpec((1, D_MODEL, tf), lambda j, i, te, nu: (te[row(i, nu)], 0, j)),
                  pl.BlockSpec((1, D_MODEL, tf), lambda j, i, te, nu: (te[row(i, nu)], 0, j))],
        out_specs=pl.BlockSpec((tm, tf), lambda j, i, te, nu: (i, j)))
    return pl.pallas_call(
        _ffn_up_kernel,
        out_shape=jax.ShapeDtypeStruct((r, D_FF), BF16),
        grid_spec=grid_spec,
        compiler_params=_cparams(("arbitrary", "arbitrary")),
        name="ffn_up",
    )(tile_expert, n_used, xs, w1, w3)


def _ffn_down_kernel(te_ref, nu_ref, g_ref, w2_ref, *rest, residual):
    i = pl.program_id(1)
    if residual:
        x_ref, y_ref = rest
    else:
        (y_ref,) = rest

    @pl.when(i < nu_ref[0])
    def _():
        y = jnp.dot(g_ref[...], w2_ref[0], preferred_element_type=F32)
        y_ref[...] = x_ref[...] + y if residual else y

    @pl.when(i >= nu_ref[0])
    def _():
        y_ref[...] = jnp.zeros(y_ref.shape, y_ref.dtype)


def ffn_down(tile_expert, n_used, g, w2, x_res, *, tm, tn):
    r = g.shape[0]
    assert r % tm == 0 and D_MODEL % tn == 0
    n_tiles = r // tm
    residual = x_res is not None

    def row(i, nu):
        return jnp.minimum(i, nu[0] - 1)

    in_specs = [pl.BlockSpec((tm, D_FF), lambda j, i, te, nu: (row(i, nu), 0)),
                pl.BlockSpec((1, D_FF, tn), lambda j, i, te, nu: (te[row(i, nu)], 0, j))]
    args = [tile_expert, n_used, g, w2]
    if residual:
        in_specs.append(pl.BlockSpec((tm, tn), lambda j, i, te, nu: (row(i, nu), j)))
        args.append(x_res)
    grid_spec = pltpu.PrefetchScalarGridSpec(
        num_scalar_prefetch=2,
        grid=(D_MODEL // tn, n_tiles),
        in_specs=in_specs,
        out_specs=pl.BlockSpec((tm, tn), lambda j, i, te, nu: (i, j)))
    return pl.pallas_call(
        functools.partial(_ffn_down_kernel, residual=residual),
        out_shape=jax.ShapeDtypeStruct((r, D_MODEL), F32),
        grid_spec=grid_spec,
        compiler_params=_cparams(("arbitrary", "arbitrary")),
        name="ffn_down",
    )(*args)


DMA_UNROLL = 8


def _dispatch_kernel(dest_ref, nu_ref, hf_hbm, o_ref, row_tok, sem, *, n_assign, tm):
    i = pl.program_id(0)

    @pl.when(i == 0)
    def _():
        def clear(r, carry):
            row_tok[r] = 0
            return carry

        def scatter(a, carry):
            row_tok[dest_ref[a]] = a // TOP_K
            return carry

        lax.fori_loop(0, row_tok.shape[0], clear, 0)
        lax.fori_loop(0, n_assign, scatter, 0)

    def row_copy(tok, r):
        return pltpu.make_async_copy(hf_hbm.at[pl.ds(tok, 1), :], o_ref.at[pl.ds(r, 1), :], sem)

    @pl.when(i < nu_ref[0])
    def _():
        def issue(r, carry):
            row_copy(row_tok[i * tm + r], r).start()
            return carry

        def drain(r, carry):
            row_copy(0, 0).wait()
            return carry

        lax.fori_loop(0, tm, issue, 0, unroll=DMA_UNROLL)
        lax.fori_loop(0, tm, drain, 0, unroll=DMA_UNROLL)

    @pl.when(i >= nu_ref[0])
    def _():
        o_ref[...] = jnp.zeros(o_ref.shape, o_ref.dtype)


def dispatch(dest, n_used, hf, *, n_tiles, tm):
    n_assign = dest.shape[0]
    grid_spec = pltpu.PrefetchScalarGridSpec(
        num_scalar_prefetch=2,
        grid=(n_tiles,),
        in_specs=[pl.BlockSpec(memory_space=pl.ANY)],
        out_specs=pl.BlockSpec((tm, D_MODEL), lambda i, d, nu: (i, 0)),
        scratch_shapes=[pltpu.SMEM((n_tiles * tm,), jnp.int32), pltpu.SemaphoreType.DMA(())])
    return pl.pallas_call(
        functools.partial(_dispatch_kernel, n_assign=n_assign, tm=tm),
        out_shape=jax.ShapeDtypeStruct((n_tiles * tm, D_MODEL), hf.dtype),
        grid_spec=grid_spec,
        compiler_params=_cparams(("arbitrary",)),
        name="moe_dispatch",
    )(dest, n_used, hf)


def _combine_kernel(dest_ref, x_ref, rt_ref, y_hbm, o_ref, buf, sem, *, tn):
    i = pl.program_id(0)
    n = pl.num_programs(0)
    slot = i % 2

    def row_copy(d, sl, k, rr):
        return pltpu.make_async_copy(y_hbm.at[pl.ds(d, 1), :], buf.at[sl, k, pl.ds(rr, 1), :],
                                     sem.at[sl])

    def start_tile(tile, sl):
        def issue(rr, carry):
            for k in range(TOP_K):
                row_copy(dest_ref[(tile * tn + rr) * TOP_K + k], sl, k, rr).start()
            return carry

        lax.fori_loop(0, tn, issue, 0, unroll=DMA_UNROLL)

    @pl.when(i == 0)
    def _():
        start_tile(i, slot)

    @pl.when(i + 1 < n)
    def _():
        start_tile(i + 1, 1 - slot)

    def drain(rr, carry):
        for k in range(TOP_K):
            row_copy(0, slot, k, 0).wait()
        return carry

    lax.fori_loop(0, tn, drain, 0, unroll=DMA_UNROLL)

    rt = rt_ref[...]
    o_ref[...] = (x_ref[...] + (rt[:, 2:3] * buf[slot, 0] + rt[:, 3:4] * buf[slot, 1]))


def combine(dest, x, route, y, *, tn):
    m = x.shape[0]
    assert m % tn == 0
    grid_spec = pltpu.PrefetchScalarGridSpec(
        num_scalar_prefetch=1,
        grid=(m // tn,),
        in_specs=[pl.BlockSpec((tn, D_MODEL), lambda i, d: (i, 0)),
                  pl.BlockSpec((tn, ROUTE_LANES), lambda i, d: (i, 0)),
                  pl.BlockSpec(memory_space=pl.ANY)],
        out_specs=pl.BlockSpec((tn, D_MODEL), lambda i, d: (i, 0)),
        scratch_shapes=[pltpu.VMEM((2, TOP_K, tn, D_MODEL), F32),
                        pltpu.SemaphoreType.DMA((2,))])
    return pl.pallas_call(
        functools.partial(_combine_kernel, tn=tn),
        out_shape=jax.ShapeDtypeStruct((m, D_MODEL), F32),
        grid_spec=grid_spec,
        compiler_params=_cparams(("arbitrary",)),
        name="moe_combine",
    )(dest, x, route, y)


def _norm_kernel(x_ref, g_ref, o_ref):
    o_ref[...] = _rms(x_ref[...], g_ref[...])


def final_norm(x, gain, *, tm):
    m = x.shape[0]
    assert m % tm == 0
    return pl.pallas_call(
        _norm_kernel,
        out_shape=jax.ShapeDtypeStruct((m, D_MODEL), F32),
        grid=(m // tm,),
        in_specs=[pl.BlockSpec((tm, D_MODEL), lambda i: (i, 0)),
                  pl.BlockSpec((1, D_MODEL), lambda i: (0, 0))],
        out_specs=pl.BlockSpec((tm, D_MODEL), lambda i: (i, 0)),
        compiler_params=_cparams(("parallel",)),
        name="final_norm",
    )(x, gain)


def _rot_half(w):
    half = QK_ROPE // 2
    return jnp.concatenate([-w[..., half:], w[..., :half]], axis=-1)


def _lane_pad(w):
    return jnp.pad(w, [(0, 0)] * (w.ndim - 1) + [(0, ROPE_LANES - w.shape[-1])])


def _prep_w_in(w_in):
    o1 = Q_LORA
    o2 = o1 + KV_LORA
    o3 = o2 + QK_ROPE
    o4 = o3 + W_SSM
    o5 = o4 + W_POOL
    kpe = w_in[..., o2:o3]
    parts = [w_in[..., o5:], w_in[..., :o1], w_in[..., o3:o4], w_in[..., o4:o5], w_in[..., o1:o2],
             _lane_pad(kpe), _lane_pad(_rot_half(kpe))]
    return jnp.concatenate(parts, axis=-1).astype(BF16)


def _prep_w_q(w_q_up):
    depth = w_q_up.shape[0]
    w = w_q_up.reshape(depth, Q_LORA, N_HEADS, QK_NOPE + QK_ROPE)
    nope = w[..., :QK_NOPE].reshape(depth, Q_LORA, N_HEADS * QK_NOPE)
    rope = w[..., QK_NOPE:]
    rope_p = _lane_pad(rope).reshape(depth, Q_LORA, N_HEADS * ROPE_LANES)
    rot_p = _lane_pad(_rot_half(rope)).reshape(depth, Q_LORA, N_HEADS * ROPE_LANES)
    return jnp.concatenate([nope, rope_p, rot_p], axis=-1).astype(BF16)


def _block_diag_heads(w):
    depth, _, r, c = w.shape
    ng = N_HEADS // HEAD_GROUP
    w5 = w.reshape(depth, ng, HEAD_GROUP, r, c)
    eye = jnp.eye(HEAD_GROUP, dtype=w.dtype)
    out = w5[:, :, :, :, None, :] * eye[None, None, :, None, :, None]
    return out.reshape(depth, ng, HEAD_GROUP * r, HEAD_GROUP * c)


def _rope_tables(pos):
    half = QK_ROPE // 2
    freqs = ROPE_BASE ** (-jnp.arange(half, dtype=F32) / half)
    ang = pos.astype(F32)[:, None] * freqs[None, :]
    cos, sin = jnp.cos(ang), jnp.sin(ang)
    return (_lane_pad(jnp.concatenate([cos, cos], axis=-1)),
            _lane_pad(jnp.concatenate([sin, sin], axis=-1)))


def _pair_diag(w):
    depth, g, r, c = w.shape
    w5 = w.reshape(depth, g // 2, 2, r, c)
    eye = jnp.eye(2, dtype=w.dtype)
    return w5[:, :, :, :, None, :] * eye[None, None, :, None, :, None]


def _prep_ssm(lam_re, lam_im, log_dt, b_re, b_im, c_re, c_im, steps_sample):
    hi = lax.Precision.HIGHEST
    depth = lam_re.shape[0]
    g, p, n, tc = SSM_GROUPS, SSM_STATE, SSM_GROUP, SSM_CHUNK
    lam = lax.complex(lam_re.astype(F32), lam_im.astype(F32))
    dt = jnp.exp(log_dt.astype(F32))[..., None]
    ldt = lam * dt
    pw = jnp.exp(ldt[..., None] * jnp.arange(tc + 1, dtype=F32))
    lam_bar = pw[..., 1]
    b_bar = ((lam_bar - 1.0) / lam)[..., None] * lax.complex(b_re.astype(F32), b_im.astype(F32))
    c_c = lax.complex(c_re.astype(F32), c_im.astype(F32))
    kt = jnp.real(jnp.einsum("dgmp,dgpt,dgpn->dgtmn", c_c, pw[..., :tc], b_bar, precision=hi))
    s_idx = jnp.arange(tc)[:, None]
    t_idx = jnp.arange(tc)[None, :]
    lag = jnp.clip(t_idx - s_idx, 0, tc - 1)
    toep = jnp.where((t_idx >= s_idx)[None, None, :, :, None, None], kt[:, :, lag], 0.0)
    toep = toep.transpose(0, 1, 2, 5, 3, 4).reshape(depth, g, tc * n, tc * n)

    def wsum_for(steps):
        ex = jnp.clip(steps - 1 - jnp.arange(tc), 0, tc)
        v = pw[..., ex][..., None] * b_bar[:, :, :, None, :]
        v = jnp.where((jnp.arange(tc) < steps)[None, None, None, :, None], v, 0.0)
        v = v.transpose(0, 1, 3, 4, 2)
        w = jnp.stack([jnp.real(v), jnp.imag(v)], axis=-2)
        w = _pair_diag(w.reshape(depth, g, tc * n, 2 * p))
        w = w.reshape(depth, g // 2, 2, tc * n, 2, 2, p).transpose(0, 1, 2, 3, 5, 4, 6)
        return w.reshape(depth, g // 2, SSM_PAIR_U, SSM_PAIR_H).astype(BF16)

    def a_for(steps):
        a = pw[..., steps].reshape(depth, g // 2, 2 * p)
        a = jnp.stack([jnp.real(a), jnp.imag(a)], axis=2)
        return jnp.pad(a, ((0, 0), (0, 0), (0, SUBLANES - 2), (0, 0)))

    ac = c_c.transpose(0, 1, 3, 2)[:, :, :, None, :] * pw[..., 1:tc + 1][..., None]
    wout = jnp.stack([jnp.real(ac), -jnp.imag(ac)], axis=2)
    wout = _pair_diag(wout.reshape(depth, g, 2 * p, tc * n))
    wout = wout.reshape(depth, g // 2, 2, 2, p, 2, tc * n).transpose(0, 1, 3, 2, 4, 5, 6)
    wout = wout.reshape(depth, g // 2, SSM_PAIR_H, SSM_PAIR_U).astype(BF16)
    return dict(toep=toep.astype(BF16), wout=wout,
                wsum_p=wsum_for(tc), a_p=a_for(tc),
                wsum_s=wsum_for(steps_sample), a_s=a_for(steps_sample))


def _ssm_pack(u, nb_pad):
    nb, length, _ = u.shape
    nk = length // SSM_CHUNK
    u6 = u.astype(BF16).reshape(nb, nk, SSM_CHUNK, SSM_GROUPS // 2, 2, SSM_GROUP)
    u6 = u6.transpose(3, 1, 0, 4, 2, 5)
    u6 = jnp.pad(u6, ((0, 0), (0, 0), (0, nb_pad - nb), (0, 0), (0, 0), (0, 0)))
    return u6.reshape(SSM_GROUPS // 2, nk * nb_pad, SSM_PAIR_U)


def _ssm_unpack(y, nb, nb_pad):
    nk = y.shape[1] // nb_pad
    y6 = y.reshape(SSM_GROUPS // 2, nk, nb_pad, 2, SSM_CHUNK, SSM_GROUP)[:, :, :nb]
    return y6.transpose(2, 1, 4, 0, 3, 5).reshape(nb, nk * SSM_CHUNK, W_SSM)


def _state_pack(h):
    nb = h.shape[0]
    h5 = h.astype(F32).reshape(nb, SSM_GROUPS // 2, 2, SSM_STATE, 2)
    return h5.transpose(0, 1, 4, 2, 3).reshape(nb, SSM_GROUPS * 2 * SSM_STATE)


def _state_unpack(hp, dtype):
    nb = hp.shape[0]
    h5 = hp.reshape(nb, SSM_GROUPS // 2, 2, 2, SSM_STATE).transpose(0, 1, 3, 4, 2)
    return h5.reshape(nb, SSM_GROUPS, SSM_STATE, 2).astype(dtype)


def _routing_plan(route, tm, n_tiles):
    flat_e = route[:, :TOP_K].astype(jnp.int32).reshape(-1)
    onehot = (flat_e[:, None] == jnp.arange(N_EXPERTS, dtype=jnp.int32)[None, :]).astype(jnp.int32)
    csum = jnp.cumsum(onehot, axis=0)
    rank = jnp.sum((csum - onehot) * onehot, axis=1)
    counts = csum[-1]
    padded = (counts + tm - 1) // tm * tm
    pad_end = jnp.cumsum(padded)
    pad_start = pad_end - padded
    dest = (jnp.sum(onehot * pad_start[None, :], axis=1) + rank).astype(jnp.int32)
    n_used = (pad_end[-1:] // tm).astype(jnp.int32)
    tile_start = jnp.arange(n_tiles, dtype=jnp.int32) * tm
    tile_expert = jnp.minimum(jnp.sum((tile_start[:, None] >= pad_end[None, :]).astype(jnp.int32), axis=1),
                              N_EXPERTS - 1).astype(jnp.int32)
    return dest, tile_expert, n_used


TM_BIG_OPTIONS = (1088, 544, 272, 136)
TM_MID_OPTIONS = (544, 272, 136)
TN_COMBINE_OPTIONS = (128, 64, 32, 16, 8)
TM_MOE = 512
TN_IN = 1024
TN_MERGE = 512
TF_UP = 1408
TN_DOWN = 1024
TQ_ATTN = 128
TK_ATTN = 512
PAGES_PER_CHUNK = 32


def _largest_divisor(n, options):
    for o in options:
        if n % o == 0:
            return o
    raise ValueError(f"no tile in {options} divides {n}")


def kernel(x_prompt, x_sample, cache_ckv, cache_kpe, state_ssm, state_pool, page_table, attn_norm_g, w_in, q_norm_g, w_q_up, kv_norm_g, w_uk, w_uv, ssm_lam_re, ssm_lam_im, ssm_log_dt, ssm_b_re, ssm_b_im, ssm_c_re, ssm_c_im, ssm_d, ssm_w_glu, pool_w, pool_scale, w_branch, w_out, ffn_norm_g, ffn_w1, ffn_w3, ffn_w2, moe_router, moe_w1, moe_w3, moe_w2, final_norm_g):
    bsz, seq, _ = x_prompt.shape
    db, t, _ = x_sample.shape
    depth = w_in.shape[0]
    past = page_table.shape[1] * PAGE_SIZE
    m_p = bsz * seq
    m_s = db * t
    m = m_p + m_s

    w_in2 = _prep_w_in(w_in)
    wq2 = _prep_w_q(w_q_up)
    wukbd = _block_diag_heads(w_uk.transpose(0, 2, 3, 1)).astype(BF16)
    wuvbd = _block_diag_heads(w_uv.transpose(0, 2, 1, 3)).astype(BF16)
    ssm_w = _prep_ssm(ssm_lam_re, ssm_lam_im, ssm_log_dt, ssm_b_re, ssm_b_im, ssm_c_re, ssm_c_im, t)
    w_glu = ssm_w_glu.astype(BF16)
    pool_wb = pool_w.astype(BF16)
    wb = w_branch.astype(BF16)
    wo = w_out.astype(BF16)
    fw1, fw3, fw2 = ffn_w1.astype(BF16), ffn_w3.astype(BF16), ffn_w2.astype(BF16)
    mw1, mw3, mw2 = moe_w1.astype(BF16), moe_w3.astype(BF16), moe_w2.astype(BF16)
    w_router = jnp.pad(moe_router.astype(F32), ((0, 0), (0, 0), (0, ROUTE_LANES - N_EXPERTS)))

    pos = jnp.concatenate([jnp.tile(jnp.arange(seq), bsz), past + jnp.tile(jnp.arange(t), db)])
    cos_t, sin_t = _rope_tables(pos)
    pos_s = past + jnp.arange(t)
    cnt_s = jnp.concatenate(
        [jnp.broadcast_to(jnp.minimum(pos_s + 1, w).astype(F32)[:, None, None], (t, 1, POOL_GW))
         for w in POOL_WINDOWS], axis=-1)

    TM_BIG = _largest_divisor(m, TM_BIG_OPTIONS)
    TM_MID = _largest_divisor(m, TM_MID_OPTIONS)
    TN_COMBINE = _largest_divisor(m, TN_COMBINE_OPTIONS)
    tk_attn = min(TK_ATTN, seq)
    pages_per_chunk = min(PAGES_PER_CHUNK, page_table.shape[1])
    n_dense_tiles = m // TM_MID
    te_dense = jnp.zeros((n_dense_tiles,), jnp.int32)
    nu_dense = jnp.full((1,), n_dense_tiles, jnp.int32)
    n_moe_tiles = (m * TOP_K) // TM_MOE + N_EXPERTS

    x = jnp.concatenate([x_prompt.reshape(m_p, D_MODEL), x_sample.reshape(m_s, D_MODEL)], axis=0)
    outs = [[] for _ in range(8)]
    for l in range(depth):
        z = in_proj(x, attn_norm_g[l][None], w_in2[l], tm=TM_BIG, tn=TN_IN)
        qcat, c_lat, k_pe, kcat = qkv_post(z, cos_t, sin_t, q_norm_g[l][None], kv_norm_g[l][None],
                                           wq2[l], wukbd[l], tm=TM_MID)
        oa_p = prompt_attention(qcat, kcat, wuvbd[l], batch=bsz, seq=seq, tq=TQ_ATTN, tk=tk_attn)
        q3 = qcat[m_p:].reshape(db, t, N_HEADS * QCAT)
        knew3 = jnp.pad(kcat[m_p:].reshape(db, t, QCAT), ((0, 0), (0, NEW_KEYS - t), (0, 0)))
        oa_s = sample_attention(page_table, q3, knew3, cache_ckv, cache_kpe, wuvbd[l],
                                layer=l, pc=pages_per_chunk)
        o_attn = jnp.concatenate([oa_p, oa_s.reshape(m_s, W_ATTN)], axis=0)
        u_ssm = z[:, Z_SSM:Z_SSM + W_SSM]
        up = _ssm_pack(u_ssm[:m_p].reshape(bsz, seq, W_SSM), SUBLANES)
        y_p, hfin_p = ssm_scan(up, ssm_w["toep"][l], ssm_w["wsum_p"][l], ssm_w["wout"][l], ssm_w["a_p"][l],
                               jnp.zeros((SUBLANES, SSM_GROUPS * 2 * SSM_STATE), F32),
                               nk=seq // SSM_CHUNK, rb=SUBLANES)
        us = jnp.pad(u_ssm[m_p:].reshape(db, t, W_SSM), ((0, 0), (0, SSM_CHUNK - t), (0, 0)))
        y_s, hfin_s = ssm_scan(_ssm_pack(us, db), ssm_w["toep"][l], ssm_w["wsum_s"][l], ssm_w["wout"][l],
                               ssm_w["a_s"][l], _state_pack(state_ssm[l]), nk=1, rb=db)
        y_raw = jnp.concatenate([_ssm_unpack(y_p, bsz, SUBLANES).reshape(m_p, W_SSM),
                                 _ssm_unpack(y_s, db, db)[:, :t].reshape(m_s, W_SSM)], axis=0)
        y_ssm = ssm_post(y_raw, z, ssm_d[l][None], w_glu[l], tm=TM_MID)
        yp_p = pool_prompt(z, pool_wb[l], pool_scale[l][None], batch=bsz, seq=seq)
        u_pool = z[:, Z_POOL:Z_POOL + W_POOL]
        u_ext = jnp.concatenate([state_pool[l].astype(F32), u_pool[m_p:].reshape(db, t, W_POOL)], axis=1)
        yp_s = pool_sample(u_ext.transpose(1, 0, 2), cnt_s, pool_wb[l], pool_scale[l][None])
        y_pool = jnp.concatenate([yp_p, yp_s.transpose(1, 0, 2).reshape(m_s, W_POOL)], axis=0)
        mm = branch_merge(o_attn, y_ssm, y_pool, z, wb[l], tm=TM_BIG, tn=TN_MERGE)
        j = l // 2
        is_moe = l % 2 == 1
        xn, hf, route = out_proj(mm, wo[l], x, ffn_norm_g[l][None], w_router[j], tm=TM_MID)
        if not is_moe:
            g_act = ffn_up(te_dense, nu_dense, hf, fw1[j][None], fw3[j][None], tm=TM_MID, tf=TF_UP)
            x = ffn_down(te_dense, nu_dense, g_act, fw2[j][None], xn, tm=TM_MID, tn=TN_DOWN)
        else:
            dest, tile_expert, n_used = _routing_plan(route, TM_MOE, n_moe_tiles)
            xs = dispatch(dest, n_used, hf, n_tiles=n_moe_tiles, tm=TM_MOE)
            g_act = ffn_up(tile_expert, n_used, xs, mw1[j], mw3[j], tm=TM_MOE, tf=TF_UP)
            y_sorted = ffn_down(tile_expert, n_used, g_act, mw2[j], None, tm=TM_MOE, tn=TN_DOWN)
            x = combine(dest, xn, route, y_sorted, tn=TN_COMBINE)
        outs[0].append(c_lat[:m_p].reshape(bsz, seq, KV_LORA))
        outs[1].append(k_pe[:m_p].reshape(bsz, seq, QK_ROPE))
        outs[2].append(_state_unpack(hfin_p[:bsz], state_ssm.dtype))
        outs[3].append(u_pool[:m_p].reshape(bsz, seq, W_POOL)[:, seq - POOL_PAD:])
        outs[4].append(c_lat[m_p:].reshape(db, t, KV_LORA))
        outs[5].append(k_pe[m_p:].reshape(db, t, QK_ROPE))
        outs[6].append(_state_unpack(hfin_s, state_ssm.dtype))
        outs[7].append(u_ext[:, t:])
    y = final_norm(x, final_norm_g[None], tm=TM_MID)
    return (y[:m_p].reshape(bsz, seq, D_MODEL), y[m_p:].reshape(db, t, D_MODEL),
            *[jnp.stack(o) for o in outs])
```

```python
import functools

import jax
import jax.numpy as jnp
import numpy as np
from jax import lax
from jax.experimental import pallas as pl
from jax.experimental.pallas import tpu as pltpu

F32 = jnp.float32
BF16 = jnp.bfloat16
U32 = jnp.uint32

D_MODEL = 2048
N_HEADS = 16
QK_NOPE = 64
QK_ROPE = 32
V_HEAD = 64
W_ATTN = N_HEADS * V_HEAD
Q_LORA = 512
KV_LORA = 256
ROPE_BASE = 10000.0
SM_SCALE = (QK_NOPE + QK_ROPE) ** -0.5
PAGE_SIZE = 128
W_SSM = 512
SSM_GROUP = 16
SSM_GROUPS = W_SSM // SSM_GROUP
SSM_STATE = 64
W_POOL = 512
POOL_WINDOWS = (2, 4, 8, 16)
POOL_GW = W_POOL // len(POOL_WINDOWS)
POOL_PAD = max(POOL_WINDOWS) - 1
N_BRANCH = 3
D_FF = 5632
N_EXPERTS = 8
TOP_K = 2
EPS = 1e-6

LANES = 128
SUBLANES = 8
VMEM_LIMIT_BYTES = 56 * 1024 * 1024

ROPE_LANES = LANES
Z_GATE = 0
Z_QC = Z_GATE + N_BRANCH * D_MODEL
Z_SSM = Z_QC + Q_LORA
Z_POOL = Z_SSM + W_SSM
Z_KV = Z_POOL + W_POOL
Z_KPE = Z_KV + KV_LORA
Z_COLS = Z_KPE + 2 * ROPE_LANES
QCAT = KV_LORA + ROPE_LANES
HEAD_GROUP = 4


def _cparams(sem, vmem=VMEM_LIMIT_BYTES):
    return pltpu.CompilerParams(dimension_semantics=sem, vmem_limit_bytes=vmem)


def _lane_tile(x, n):
    return x if n == 1 else jnp.concatenate([x] * n, axis=1)


def _rms(x, g):
    ms = jnp.mean(x * x, axis=-1, keepdims=True)
    return x * lax.rsqrt(ms + EPS) * g


def _packed_width(d):
    return d // 2


def _pack_rows(h):
    c = h.shape[1] // 2
    bits = pltpu.bitcast(h.astype(BF16).astype(F32), U32)
    return (bits[:, :c] >> 16) | (bits[:, c:] & jnp.uint32(0xFFFF0000))


def _unpack_rows(p):
    lo = pltpu.bitcast(p << 16, F32).astype(BF16)
    hi = pltpu.bitcast(p & jnp.uint32(0xFFFF0000), F32).astype(BF16)
    return jnp.concatenate([lo, hi], axis=1)


IN_SUB = 256


def _in_proj_kernel(x_ref, g_ref, w_ref, z_ref, h_ref, *, n_gate_tiles):
    j = pl.program_id(1)

    @pl.when(j == 0)
    def _():
        h_ref[...] = _rms(x_ref[...], g_ref[...]).astype(BF16)

    h = h_ref[...]
    is_gate = j < n_gate_tiles
    for c in range(z_ref.shape[1] // IN_SUB):
        sl = slice(c * IN_SUB, (c + 1) * IN_SUB)
        acc = jnp.dot(h, w_ref[0, :, sl], preferred_element_type=F32)
        z_ref[:, sl] = jnp.where(is_gate, jax.nn.sigmoid(acc), acc)


def in_proj(x, gain, w, *, layer, tm, tn):
    m, d = x.shape
    n = w.shape[2]
    assert m % tm == 0 and n % tn == 0 and (N_BRANCH * D_MODEL) % tn == 0 and tn % IN_SUB == 0
    return pl.pallas_call(
        functools.partial(_in_proj_kernel, n_gate_tiles=(N_BRANCH * D_MODEL) // tn),
        out_shape=jax.ShapeDtypeStruct((m, n), F32),
        grid=(m // tm, n // tn),
        in_specs=[pl.BlockSpec((tm, d), lambda i, j: (i, 0)),
                  pl.BlockSpec((1, d), lambda i, j: (0, 0)),
                  pl.BlockSpec((1, d, tn), lambda i, j: (layer, 0, j))],
        out_specs=pl.BlockSpec((tm, tn), lambda i, j: (i, j)),
        scratch_shapes=[pltpu.VMEM((tm, d), BF16)],
        compiler_params=_cparams(("parallel", "arbitrary")),
        name="in_proj",
    )(x, gain, w)


def _qkv_kernel(qc_ref, kv_ref, kpe_ref, cos_ref, sin_ref, qg_ref, kvg_ref, wq_ref, wuk_ref,
                qcat_ref, c_ref, kpe_out_ref, kcat_ref):
    nope_w = N_HEADS * QK_NOPE
    rope_w = N_HEADS * ROPE_LANES
    cos = cos_ref[...]
    sin = sin_ref[...]
    qn = _rms(qc_ref[...], qg_ref[...]).astype(BF16)
    q = jnp.dot(qn, wq_ref[0], preferred_element_type=F32)
    cos_h = jnp.concatenate([cos] * N_HEADS, axis=1)
    sin_h = jnp.concatenate([sin] * N_HEADS, axis=1)
    q_pe = (q[:, nope_w:nope_w + rope_w] * cos_h
            + q[:, nope_w + rope_w:nope_w + 2 * rope_w] * sin_h).astype(BF16)
    q_nope = q[:, :nope_w].astype(BF16)
    gw = HEAD_GROUP * QK_NOPE
    for g in range(N_HEADS // HEAD_GROUP):
        qa = jnp.dot(q_nope[:, g * gw:(g + 1) * gw], wuk_ref[0, g],
                     preferred_element_type=F32).astype(BF16)
        for hl in range(HEAD_GROUP):
            h = g * HEAD_GROUP + hl
            qcat_ref[:, h * QCAT:h * QCAT + KV_LORA] = qa[:, hl * KV_LORA:(hl + 1) * KV_LORA]
            qcat_ref[:, h * QCAT + KV_LORA:(h + 1) * QCAT] = q_pe[:, h * ROPE_LANES:(h + 1) * ROPE_LANES]
    c = _rms(kv_ref[...], kvg_ref[...])
    c_ref[...] = c
    kraw = kpe_ref[...]
    kr = kraw[:, :ROPE_LANES] * cos + kraw[:, ROPE_LANES:] * sin
    kpe_out_ref[...] = kr[:, :QK_ROPE]
    kcat_ref[:, :KV_LORA] = c.astype(BF16)
    kcat_ref[:, KV_LORA:] = kr.astype(BF16)


def qkv_post(z, cos_t, sin_t, q_g, kv_g, wq, wuk, *, layer, tm):
    m = z.shape[0]
    assert m % tm == 0
    nq = wq.shape[2]
    return pl.pallas_call(
        _qkv_kernel,
        out_shape=(jax.ShapeDtypeStruct((m, N_HEADS * QCAT), BF16),
                   jax.ShapeDtypeStruct((m, KV_LORA), F32),
                   jax.ShapeDtypeStruct((m, QK_ROPE), F32),
                   jax.ShapeDtypeStruct((m, QCAT), BF16)),
        grid=(m // tm,),
        in_specs=[pl.BlockSpec((tm, Q_LORA), lambda i: (i, Z_QC // Q_LORA)),
                  pl.BlockSpec((tm, KV_LORA), lambda i: (i, Z_KV // KV_LORA)),
                  pl.BlockSpec((tm, 2 * ROPE_LANES), lambda i: (i, Z_KPE // (2 * ROPE_LANES))),
                  pl.BlockSpec((tm, ROPE_LANES), lambda i: (i, 0)),
                  pl.BlockSpec((tm, ROPE_LANES), lambda i: (i, 0)),
                  pl.BlockSpec((1, Q_LORA), lambda i: (0, 0)),
                  pl.BlockSpec((1, KV_LORA), lambda i: (0, 0)),
                  pl.BlockSpec((1, Q_LORA, nq), lambda i: (layer, 0, 0)),
                  pl.BlockSpec((1,) + wuk.shape[1:], lambda i: (layer, 0, 0, 0))],
        out_specs=(pl.BlockSpec((tm, N_HEADS * QCAT), lambda i: (i, 0)),
                   pl.BlockSpec((tm, KV_LORA), lambda i: (i, 0)),
                   pl.BlockSpec((tm, QK_ROPE), lambda i: (i, 0)),
                   pl.BlockSpec((tm, QCAT), lambda i: (i, 0))),
        compiler_params=_cparams(("parallel",)),
        name="qkv_post",
    )(z, z, z, cos_t, sin_t, q_g, kv_g, wq, wuk)


def _attn_out_proj(o, wuv_ref, rows):
    ob = o.astype(BF16)
    outs = []
    for g in range(N_HEADS // HEAD_GROUP):
        wide = jnp.concatenate(
            [ob[(g * HEAD_GROUP + hl) * rows:(g * HEAD_GROUP + hl + 1) * rows, :]
             for hl in range(HEAD_GROUP)], axis=1)
        outs.append(jnp.dot(wide, wuv_ref[0, g], preferred_element_type=F32))
    return jnp.concatenate(outs, axis=1)


def _prompt_attn_kernel(q_ref, k_ref, wuv_ref, o_ref, m_ref, l_ref, acc_ref, *, tq, tk):
    i = pl.program_id(1)
    rows = N_HEADS * tq
    q = jnp.concatenate([q_ref[:, h * QCAT:(h + 1) * QCAT] for h in range(N_HEADS)], axis=0)
    m_ref[...] = jnp.full(m_ref.shape, -jnp.inf, F32)
    l_ref[...] = jnp.zeros(l_ref.shape, F32)
    acc_ref[...] = jnp.zeros(acc_ref.shape, F32)

    def chunk(j, masked):
        start = pl.multiple_of(j * tk, tk)
        k = k_ref[pl.ds(start, tk), :]
        s = lax.dot_general(q, k, (((1,), (1,)), ((), ())), preferred_element_type=F32) * SM_SCALE
        if masked:
            qpos = i * tq + (lax.broadcasted_iota(jnp.int32, (rows, tk), 0) & (tq - 1))
            kpos = start + lax.broadcasted_iota(jnp.int32, (rows, tk), 1)
            s = jnp.where(kpos <= qpos, s, -jnp.inf)
        m_old = m_ref[...]
        m_new = jnp.maximum(m_old, jnp.max(s, axis=1, keepdims=True))
        alpha = jnp.exp(m_old - m_new)
        p = jnp.exp(s - _lane_tile(m_new, tk // LANES))
        l_ref[...] = alpha * l_ref[...] + jnp.sum(p, axis=1, keepdims=True)
        acc_ref[...] = (_lane_tile(alpha, KV_LORA // LANES) * acc_ref[...]
                        + jnp.dot(p.astype(BF16), k[:, :KV_LORA], preferred_element_type=F32))
        m_ref[...] = m_new

    n_full = (i * tq) // tk

    def full_body(j, carry):
        chunk(j, False)
        return carry

    lax.fori_loop(0, n_full, full_body, 0)
    chunk(n_full, True)
    o = acc_ref[...] / _lane_tile(l_ref[...], KV_LORA // LANES)
    o_ref[...] = _attn_out_proj(o, wuv_ref, tq).astype(o_ref.dtype)


def prompt_attention(qcat, kcat, wuv, *, layer, batch, seq, tq, tk):
    assert seq % tq == 0 and seq % tk == 0 and tk % tq == 0 and (tq & (tq - 1)) == 0
    nq = seq // tq
    rows = N_HEADS * tq
    return pl.pallas_call(
        functools.partial(_prompt_attn_kernel, tq=tq, tk=tk),
        out_shape=jax.ShapeDtypeStruct((batch * seq, W_ATTN), BF16),
        grid=(batch, nq),
        in_specs=[pl.BlockSpec((tq, N_HEADS * QCAT), lambda b, i: (b * nq + i, 0)),
                  pl.BlockSpec((seq, QCAT), lambda b, i: (b, 0)),
                  pl.BlockSpec((1,) + wuv.shape[1:], lambda b, i: (layer, 0, 0, 0))],
        out_specs=pl.BlockSpec((tq, W_ATTN), lambda b, i: (b * nq + i, 0)),
        scratch_shapes=[pltpu.VMEM((rows, LANES), F32), pltpu.VMEM((rows, LANES), F32),
                        pltpu.VMEM((rows, KV_LORA), F32)],
        compiler_params=_cparams(("parallel", "arbitrary")),
        name="prompt_attn",
    )(qcat, kcat, wuv)


NEW_KEYS = LANES


def _sample_attn_kernel(pt_ref, q_ref, knew_ref, ckv_hbm, kpe_hbm, wuv_ref, o_ref,
                        ckv_buf, kpe_buf, sem, m_ref, l_ref, acc_ref,
                        *, layer, n_chunks, pc, t):
    b = pl.program_id(0)
    c = pl.program_id(1)
    total = pl.num_programs(0) * n_chunks
    step = b * n_chunks + c
    slot = step % 2
    rows = t * N_HEADS

    def copies(bb, cc, sl):
        out = []
        for p in range(pc):
            page = pt_ref[bb, cc * pc + p]
            out.append(pltpu.make_async_copy(
                ckv_hbm.at[layer, page], ckv_buf.at[sl, pl.ds(p * PAGE_SIZE, PAGE_SIZE), :],
                sem.at[sl, 0]))
            out.append(pltpu.make_async_copy(kpe_hbm.at[layer, page], kpe_buf.at[sl, p], sem.at[sl, 1]))
        return out

    @pl.when(step == 0)
    def _():
        for cp in copies(b, c, slot):
            cp.start()

    @pl.when(step + 1 < total)
    def _():
        last = c + 1 == n_chunks
        nb = jnp.where(last, b + 1, b)
        nc = jnp.where(last, 0, c + 1)
        for cp in copies(nb, nc, 1 - slot):
            cp.start()

    @pl.when(c == 0)
    def _():
        m_ref[...] = jnp.full(m_ref.shape, -jnp.inf, F32)
        l_ref[...] = jnp.zeros(l_ref.shape, F32)
        acc_ref[...] = jnp.zeros(acc_ref.shape, F32)

    for cp in copies(b, c, slot):
        cp.wait()

    q = q_ref[0]
    n_keys = pc * PAGE_SIZE

    def update(s, v):
        m_old = m_ref[...]
        m_new = jnp.maximum(m_old, jnp.max(s, axis=1, keepdims=True))
        alpha = jnp.exp(m_old - m_new)
        p = jnp.exp(s - _lane_tile(m_new, s.shape[1] // LANES))
        l_ref[...] = alpha * l_ref[...] + jnp.sum(p, axis=1, keepdims=True)
        acc_ref[...] = (_lane_tile(alpha, KV_LORA // LANES) * acc_ref[...]
                        + jnp.dot(p.astype(BF16), v, preferred_element_type=F32))
        m_ref[...] = m_new

    ck = ckv_buf[slot].astype(BF16)
    kr_t = jnp.concatenate([kpe_buf[slot, p] for p in range(pc)], axis=1)
    kr_t = jnp.concatenate([kr_t, jnp.zeros((ROPE_LANES - QK_ROPE, n_keys), F32)], axis=0).astype(BF16)
    nt = (((1,), (1,)), ((), ()))
    s = (lax.dot_general(q[:, :KV_LORA], ck, nt, preferred_element_type=F32)
         + jnp.dot(q[:, KV_LORA:], kr_t, preferred_element_type=F32)) * SM_SCALE
    update(s, ck)

    @pl.when(c == n_chunks - 1)
    def _():
        kn = knew_ref[0]
        sn = lax.dot_general(q, kn, nt, preferred_element_type=F32) * SM_SCALE
        qi = lax.broadcasted_iota(jnp.int32, (rows, NEW_KEYS), 0) // N_HEADS
        ki = lax.broadcasted_iota(jnp.int32, (rows, NEW_KEYS), 1)
        sn = jnp.where(ki <= qi, sn, -jnp.inf)
        update(sn, kn[:, :KV_LORA])
        o = (acc_ref[...] / _lane_tile(l_ref[...], KV_LORA // LANES)).astype(BF16)
        full = jnp.dot(o, wuv_ref[0], preferred_element_type=F32)
        head = lax.broadcasted_iota(jnp.int32, (rows, W_ATTN), 0) % N_HEADS
        col_head = lax.broadcasted_iota(jnp.int32, (rows, W_ATTN), 1) // V_HEAD
        own = jnp.where(head == col_head, full, 0.0)
        o_ref[0] = jnp.concatenate(
            [jnp.sum(own[tt * N_HEADS:(tt + 1) * N_HEADS, :], axis=0, keepdims=True) for tt in range(t)],
            axis=0).astype(o_ref.dtype)


def sample_attention(page_table, q3, knew3, cache_ckv, cache_kpe_t, wuv_flat, *, layer, pc):
    db, n_pages = page_table.shape
    rows = q3.shape[1]
    t = rows // N_HEADS
    assert n_pages % pc == 0 and t <= NEW_KEYS
    n_chunks = n_pages // pc
    n_keys = pc * PAGE_SIZE
    grid_spec = pltpu.PrefetchScalarGridSpec(
        num_scalar_prefetch=1,
        grid=(db, n_chunks),
        in_specs=[pl.BlockSpec((1, rows, QCAT), lambda b, c, pt: (b, 0, 0)),
                  pl.BlockSpec((1, NEW_KEYS, QCAT), lambda b, c, pt: (b, 0, 0)),
                  pl.BlockSpec(memory_space=pl.ANY),
                  pl.BlockSpec(memory_space=pl.ANY),
                  pl.BlockSpec((1, KV_LORA, W_ATTN), lambda b, c, pt: (layer, 0, 0))],
        out_specs=pl.BlockSpec((1, t, W_ATTN), lambda b, c, pt: (b, 0, 0)),
        scratch_shapes=[pltpu.VMEM((2, n_keys, KV_LORA), F32),
                        pltpu.VMEM((2, pc, QK_ROPE, PAGE_SIZE), F32),
                        pltpu.SemaphoreType.DMA((2, 2)),
                        pltpu.VMEM((rows, LANES), F32),
                        pltpu.VMEM((rows, LANES), F32),
                        pltpu.VMEM((rows, KV_LORA), F32)])
    return pl.pallas_call(
        functools.partial(_sample_attn_kernel, layer=layer, n_chunks=n_chunks, pc=pc, t=t),
        out_shape=jax.ShapeDtypeStruct((db, t, W_ATTN), BF16),
        grid_spec=grid_spec,
        compiler_params=_cparams(("arbitrary", "arbitrary")),
        name="sample_attn",
    )(page_table, q3, knew3, cache_ckv, cache_kpe_t, wuv_flat)


SSM_TC = 8
SSM_TILE_GROUPS = LANES // SSM_GROUP
SSM_TILES = W_SSM // LANES
SSM_HALF = SSM_TILE_GROUPS * SSM_STATE
SSM_COLS = 2 * SSM_HALF


def _ssm_kernel(u_ref, toep_ref, wsum_ref, wout_ref, a_ref, h0_ref, y_ref, hfin_ref,
                s_scr, hp_scr, *, nseq, nk, steps):
    r = nseq * nk
    kw = steps * LANES
    u = jnp.concatenate([u_ref[pl.ds(s, r, stride=steps), :].astype(BF16) for s in range(steps)],
                        axis=1)
    sk_all = jnp.dot(u, wsum_ref[0, 0, :kw, :], preferred_element_type=F32)
    a = a_ref[0, 0]
    a_re = a[0:1, :]
    a_im = a[1:2, :]
    h0 = h0_ref[...]
    n_slab = SSM_COLS // LANES
    if nk == 1:
        hp = h0
        h_re, h_im = h0[:, :SSM_HALF], h0[:, SSM_HALF:]
        hfin_ref[...] = jnp.concatenate(
            [a_re * h_re - a_im * h_im + sk_all[:, :SSM_HALF],
             a_re * h_im + a_im * h_re + sk_all[:, SSM_HALF:]], axis=1)
    else:
        for j in range(n_slab):
            s_scr[j] = sk_all[:, j * LANES:(j + 1) * LANES]
        a_re_b = jnp.broadcast_to(a_re, (nseq, SSM_HALF))
        a_im_b = jnp.broadcast_to(a_im, (nseq, SSM_HALF))

        def body(k, h):
            for j in range(n_slab):
                hp_scr[j, pl.ds(k, nseq, stride=nk), :] = h[:, j * LANES:(j + 1) * LANES]
            sk = jnp.concatenate([s_scr[j, pl.ds(k, nseq, stride=nk), :] for j in range(n_slab)], axis=1)
            h_re, h_im = h[:, :SSM_HALF], h[:, SSM_HALF:]
            return jnp.concatenate(
                [a_re_b * h_re - a_im_b * h_im + sk[:, :SSM_HALF],
                 a_re_b * h_im + a_im_b * h_re + sk[:, SSM_HALF:]], axis=1)

        hfin_ref[...] = lax.fori_loop(0, nk, body, h0)
        hp = jnp.concatenate([hp_scr[j] for j in range(n_slab)], axis=1)
    y = (jnp.dot(u, toep_ref[0, 0, :kw, :], preferred_element_type=F32)
         + jnp.dot(hp.astype(BF16), wout_ref[0, 0], preferred_element_type=F32))
    for tt in range(steps):
        y_ref[pl.ds(tt, r, stride=steps), :] = y[:, tt * LANES:(tt + 1) * LANES]


def ssm_scan(z, toep, wsum, wout, a, h0, *, layer, row_block, nseq, nk, steps):
    rows = nseq * nk * steps
    r = nseq * nk
    w_spec = pl.BlockSpec((1, 1, SSM_TC * LANES, SSM_COLS), lambda j: (layer, j, 0, 0))
    return pl.pallas_call(
        functools.partial(_ssm_kernel, nseq=nseq, nk=nk, steps=steps),
        out_shape=(jax.ShapeDtypeStruct((rows, W_SSM), F32),
                   jax.ShapeDtypeStruct((nseq, SSM_TILES * SSM_COLS), F32)),
        grid=(SSM_TILES,),
        in_specs=[pl.BlockSpec((rows, LANES), lambda j: (row_block, Z_SSM // LANES + j)),
                  w_spec, w_spec,
                  pl.BlockSpec((1, 1, SSM_COLS, SSM_TC * LANES), lambda j: (layer, j, 0, 0)),
                  pl.BlockSpec((1, 1, SUBLANES, SSM_HALF), lambda j: (layer, j, 0, 0)),
                  pl.BlockSpec((nseq, SSM_COLS), lambda j: (0, j))],
        out_specs=(pl.BlockSpec((rows, LANES), lambda j: (0, j)),
                   pl.BlockSpec((nseq, SSM_COLS), lambda j: (0, j))),
        scratch_shapes=[pltpu.VMEM((SSM_COLS // LANES, r, LANES), F32),
                        pltpu.VMEM((SSM_COLS // LANES, r, LANES), F32)],
        compiler_params=_cparams(("parallel",)),
        name="ssm_scan",
    )(z, toep, wsum, wout, a, h0)


def _gelu_tanh(x):
    return 0.5 * x * (1.0 + jnp.tanh(np.sqrt(2.0 / np.pi).astype(np.float32)
                                     * (x + np.float32(0.044715) * (x * x * x))))


def _ssm_post_kernel(y_ref, u_ref, d_ref, wg_ref, o_ref):
    y = y_ref[...] + d_ref[...] * u_ref[...]
    zz = _gelu_tanh(y)
    gate = jax.nn.sigmoid(jnp.dot(zz.astype(BF16), wg_ref[...], preferred_element_type=F32))
    o_ref[...] = (zz * gate).astype(o_ref.dtype)


def ssm_post(y_raw, z, d_skip, w_glu, *, tm):
    m = y_raw.shape[0]
    assert m % tm == 0
    return pl.pallas_call(
        _ssm_post_kernel,
        out_shape=jax.ShapeDtypeStruct((m, W_SSM), BF16),
        grid=(m // tm,),
        in_specs=[pl.BlockSpec((tm, W_SSM), lambda i: (i, 0)),
                  pl.BlockSpec((tm, W_SSM), lambda i: (i, Z_SSM // W_SSM)),
                  pl.BlockSpec((1, W_SSM), lambda i: (0, 0)),
                  pl.BlockSpec((W_SSM, W_SSM), lambda i: (0, 0))],
        out_specs=pl.BlockSpec((tm, W_SSM), lambda i: (i, 0)),
        compiler_params=_cparams(("parallel",)),
        name="ssm_post",
    )(y_raw, z, d_skip, w_glu)


POOL_HALO = 2 * SUBLANES


def _pool_mix(diff, pw_ref, ps_ref):
    outs = [jnp.dot(diff[:, g * POOL_GW:(g + 1) * POOL_GW].astype(BF16), pw_ref[g],
                    preferred_element_type=F32) for g in range(len(POOL_WINDOWS))]
    return jnp.concatenate(outs, axis=1) * ps_ref[...]


def _pool_prompt_kernel(u_ref, pw_ref, ps_ref, o_ref, ext):
    seq = u_ref.shape[0]
    ext[0:POOL_HALO, :] = jnp.zeros((POOL_HALO, W_POOL), F32)
    ext[POOL_HALO:, :] = u_ref[...]
    pos1 = lax.broadcasted_iota(jnp.int32, (seq, POOL_GW), 0) + 1
    means = []
    for g, w in enumerate(POOL_WINDOWS):
        tot = ext[POOL_HALO:POOL_HALO + seq, g * POOL_GW:(g + 1) * POOL_GW]
        for k in range(1, w):
            tot = tot + ext[POOL_HALO - k:POOL_HALO - k + seq, g * POOL_GW:(g + 1) * POOL_GW]
        means.append(tot / jnp.minimum(pos1, w).astype(F32))
    diff = jnp.concatenate(means, axis=1) - u_ref[...]
    o_ref[...] = _pool_mix(diff, pw_ref, ps_ref).astype(o_ref.dtype)


def pool_prompt(z, pool_w, pool_scale, *, batch, seq):
    return pl.pallas_call(
        _pool_prompt_kernel,
        out_shape=jax.ShapeDtypeStruct((batch * seq, W_POOL), BF16),
        grid=(batch,),
        in_specs=[pl.BlockSpec((seq, W_POOL), lambda b: (b, Z_POOL // W_POOL)),
                  pl.BlockSpec(pool_w.shape, lambda b: (0, 0, 0)),
                  pl.BlockSpec((1, W_POOL), lambda b: (0, 0))],
        out_specs=pl.BlockSpec((seq, W_POOL), lambda b: (b, 0)),
        scratch_shapes=[pltpu.VMEM((POOL_HALO + seq, W_POOL), F32)],
        compiler_params=_cparams(("parallel",)),
        name="pool_prompt",
    )(z, pool_w, pool_scale)


def _pool_sample_kernel(x_ref, cnt_ref, pw_ref, ps_ref, o_ref, *, t):
    for tt in range(t):
        means = []
        for g, w in enumerate(POOL_WINDOWS):
            sl = slice(g * POOL_GW, (g + 1) * POOL_GW)
            tot = x_ref[POOL_PAD + tt][:, sl]
            for k in range(1, w):
                tot = tot + x_ref[POOL_PAD + tt - k][:, sl]
            means.append(tot / cnt_ref[tt][:, sl])
        diff = jnp.concatenate(means, axis=1) - x_ref[POOL_PAD + tt]
        o_ref[tt] = _pool_mix(diff, pw_ref, ps_ref).astype(o_ref.dtype)


def pool_sample(x_tm, cnt, pool_w, pool_scale):
    rows, nseq, _ = x_tm.shape
    t = rows - POOL_PAD
    return pl.pallas_call(
        functools.partial(_pool_sample_kernel, t=t),
        out_shape=jax.ShapeDtypeStruct((t, nseq, W_POOL), BF16),
        grid=(1,),
        in_specs=[pl.BlockSpec(x_tm.shape, lambda i: (0, 0, 0)),
                  pl.BlockSpec(cnt.shape, lambda i: (0, 0, 0)),
                  pl.BlockSpec(pool_w.shape, lambda i: (0, 0, 0)),
                  pl.BlockSpec((1, W_POOL), lambda i: (0, 0))],
        out_specs=pl.BlockSpec((t, nseq, W_POOL), lambda i: (0, 0, 0)),
        compiler_params=_cparams(("arbitrary",)),
        name="pool_sample",
    )(x_tm, cnt, pool_w, pool_scale)


def _merge_kernel(oa_ref, ys_ref, yp_ref, g0_ref, g1_ref, g2_ref, wa_ref, ws_ref, wp_ref, m_ref):
    m = (g0_ref[...] * jnp.dot(oa_ref[...], wa_ref[0], preferred_element_type=F32)
         + g1_ref[...] * jnp.dot(ys_ref[...], ws_ref[0], preferred_element_type=F32)
         + g2_ref[...] * jnp.dot(yp_ref[...], wp_ref[0], preferred_element_type=F32))
    m_ref[...] = m.astype(m_ref.dtype)


def branch_merge(o_attn, y_ssm, y_pool, z, w_branch, *, layer, tm, tn):
    m = o_attn.shape[0]
    assert m % tm == 0 and D_MODEL % tn == 0
    nj = D_MODEL // tn
    return pl.pallas_call(
        _merge_kernel,
        out_shape=jax.ShapeDtypeStruct((m, D_MODEL), BF16),
        grid=(m // tm, nj),
        in_specs=[pl.BlockSpec((tm, W_ATTN), lambda i, j: (i, 0)),
                  pl.BlockSpec((tm, W_SSM), lambda i, j: (i, 0)),
                  pl.BlockSpec((tm, W_POOL), lambda i, j: (i, 0)),
                  pl.BlockSpec((tm, tn), lambda i, j: (i, j)),
                  pl.BlockSpec((tm, tn), lambda i, j: (i, nj + j)),
                  pl.BlockSpec((tm, tn), lambda i, j: (i, 2 * nj + j)),
                  pl.BlockSpec((1, W_ATTN, tn), lambda i, j: (layer, 0, j)),
                  pl.BlockSpec((1, W_SSM, tn), lambda i, j: (layer, W_ATTN // W_SSM, j)),
                  pl.BlockSpec((1, W_POOL, tn), lambda i, j: (layer, (W_ATTN + W_SSM) // W_POOL, j))],
        out_specs=pl.BlockSpec((tm, tn), lambda i, j: (i, j)),
        compiler_params=_cparams(("parallel", "arbitrary")),
        name="branch_merge",
    )(o_attn, y_ssm, y_pool, z, z, z, w_branch, w_branch, w_branch)


ROUTE_LANES = LANES


def _route(hf, wr_ref):
    wr = wr_ref[0]
    best1 = jnp.sum(hf * wr[0:1, :], axis=1, keepdims=True)
    idx1 = jnp.zeros(best1.shape, jnp.int32)
    best2 = jnp.full(best1.shape, -jnp.inf, F32)
    idx2 = jnp.zeros(best1.shape, jnp.int32)
    for e in range(1, N_EXPERTS):
        le = jnp.sum(hf * wr[e:e + 1, :], axis=1, keepdims=True)
        gt1 = le > best1
        gt2 = le > best2
        best2 = jnp.where(gt1, best1, jnp.where(gt2, le, best2))
        idx2 = jnp.where(gt1, idx1, jnp.where(gt2, e, idx2))
        best1 = jnp.where(gt1, le, best1)
        idx1 = jnp.where(gt1, e, idx1)
    ex = jnp.exp(best2 - best1)
    g1 = 1.0 / (1.0 + ex)
    g2 = ex / (1.0 + ex)
    lane = lax.broadcasted_iota(jnp.int32, (hf.shape[0], ROUTE_LANES), 1)
    return jnp.where(lane == 0, idx1.astype(F32),
                     jnp.where(lane == 1, idx2.astype(F32),
                               jnp.where(lane == 2, g1, jnp.where(lane == 3, g2, 0.0))))


def _out_proj_kernel(m_ref, w_ref, x_ref, g_ref, *rest, route):
    if route:
        wr_ref, xn_ref, hp_ref, rt_ref = rest
    else:
        xn_ref, hp_ref = rest
    xn = x_ref[...] + jnp.dot(m_ref[...], w_ref[0], preferred_element_type=F32)
    xn_ref[...] = xn
    hf = _rms(xn, g_ref[...])
    hp_ref[...] = _pack_rows(hf)
    if route:
        rt_ref[...] = _route(hf, wr_ref)


def out_proj(mm, w_out, x, gain, w_router_t, *, layer, moe_layer, tm):
    m = x.shape[0]
    assert m % tm == 0
    route = moe_layer is not None
    row = lambda i: (i, 0)
    in_specs = [pl.BlockSpec((tm, D_MODEL), row),
                pl.BlockSpec((1, D_MODEL, D_MODEL), lambda i: (layer, 0, 0)),
                pl.BlockSpec((tm, D_MODEL), row),
                pl.BlockSpec((1, D_MODEL), lambda i: (0, 0))]
    args = [mm, w_out, x, gain]
    pw = _packed_width(D_MODEL)
    out_shape = [jax.ShapeDtypeStruct((m, D_MODEL), F32), jax.ShapeDtypeStruct((m, pw), U32)]
    out_specs = [pl.BlockSpec((tm, D_MODEL), row), pl.BlockSpec((tm, pw), row)]
    if route:
        in_specs.append(pl.BlockSpec((1, N_EXPERTS, D_MODEL), lambda i: (moe_layer, 0, 0)))
        args.append(w_router_t)
        out_shape.append(jax.ShapeDtypeStruct((m, ROUTE_LANES), F32))
        out_specs.append(pl.BlockSpec((tm, ROUTE_LANES), row))
    return pl.pallas_call(
        functools.partial(_out_proj_kernel, route=route),
        out_shape=tuple(out_shape),
        grid=(m // tm,),
        in_specs=in_specs,
        out_specs=tuple(out_specs),
        compiler_params=_cparams(("parallel",)),
        name="out_proj",
    )(*args)


def _tile_row(i, nu):
    return jnp.minimum(i, nu[0] - 1)


def _ffn_up_kernel(te_ref, nu_ref, x_ref, w1_ref, w3_ref, g_ref, w1b, w3b):
    i = pl.program_id(1)
    used = i < nu_ref[0]
    cur = te_ref[_tile_row(i, nu_ref)]
    prev = te_ref[_tile_row(jnp.maximum(i - 1, 0), nu_ref)]

    @pl.when(used & ((i == 0) | (cur != prev)))
    def _():
        w1b[...] = w1_ref[0, 0].astype(BF16)
        w3b[...] = w3_ref[0, 0].astype(BF16)

    @pl.when(used)
    def _():
        x = _unpack_rows(x_ref[...])
        a = jnp.dot(x, w1b[...], preferred_element_type=F32)
        b = jnp.dot(x, w3b[...], preferred_element_type=F32)
        g_ref[...] = (a * jax.nn.sigmoid(a) * b).astype(g_ref.dtype)

    @pl.when(jnp.logical_not(used))
    def _():
        g_ref[...] = jnp.zeros(g_ref.shape, g_ref.dtype)


def ffn_up(tile_expert, n_used, xs, w1, w3, *, layer, tm, tf):
    r, xw = xs.shape
    assert r % tm == 0 and D_FF % tf == 0
    w_spec = pl.BlockSpec((1, 1, D_MODEL, tf),
                          lambda j, i, te, nu: (layer, te[_tile_row(i, nu)], 0, j))
    grid_spec = pltpu.PrefetchScalarGridSpec(
        num_scalar_prefetch=2,
        grid=(D_FF // tf, r // tm),
        in_specs=[pl.BlockSpec((tm, xw), lambda j, i, te, nu: (_tile_row(i, nu), 0)), w_spec, w_spec],
        out_specs=pl.BlockSpec((tm, tf), lambda j, i, te, nu: (i, j)),
        scratch_shapes=[pltpu.VMEM((D_MODEL, tf), BF16), pltpu.VMEM((D_MODEL, tf), BF16)])
    return pl.pallas_call(
        _ffn_up_kernel,
        out_shape=jax.ShapeDtypeStruct((r, D_FF), BF16),
        grid_spec=grid_spec,
        compiler_params=_cparams(("arbitrary", "arbitrary")),
        name="ffn_up",
    )(tile_expert, n_used, xs, w1, w3)


def _ffn_down_kernel(te_ref, nu_ref, g_ref, w2_ref, *rest, residual):
    i = pl.program_id(1)
    if residual:
        x_ref, y_ref = rest
    else:
        (y_ref,) = rest

    @pl.when(i < nu_ref[0])
    def _():
        y = jnp.dot(g_ref[...], w2_ref[0, 0], preferred_element_type=F32)
        y_ref[...] = x_ref[...] + y if residual else y

    @pl.when(i >= nu_ref[0])
    def _():
        y_ref[...] = jnp.zeros(y_ref.shape, y_ref.dtype)


def ffn_down(tile_expert, n_used, g, w2, x_res, *, layer, tm, tn):
    r = g.shape[0]
    assert r % tm == 0 and D_MODEL % tn == 0
    residual = x_res is not None
    in_specs = [pl.BlockSpec((tm, D_FF), lambda j, i, te, nu: (_tile_row(i, nu), 0)),
                pl.BlockSpec((1, 1, D_FF, tn), lambda j, i, te, nu: (layer, te[_tile_row(i, nu)], 0, j))]
    args = [tile_expert, n_used, g, w2]
    if residual:
        in_specs.append(pl.BlockSpec((tm, tn), lambda j, i, te, nu: (_tile_row(i, nu), j)))
        args.append(x_res)
    grid_spec = pltpu.PrefetchScalarGridSpec(
        num_scalar_prefetch=2,
        grid=(D_MODEL // tn, r // tm),
        in_specs=in_specs,
        out_specs=pl.BlockSpec((tm, tn), lambda j, i, te, nu: (i, j)))
    return pl.pallas_call(
        functools.partial(_ffn_down_kernel, residual=residual),
        out_shape=jax.ShapeDtypeStruct((r, D_MODEL), F32),
        grid_spec=grid_spec,
        compiler_params=_cparams(("arbitrary", "arbitrary")),
        name="ffn_down",
    )(*args)


DMA_UNROLL = 8
SCATTER_BATCH = 16


def _dispatch_kernel(dest_ref, nu_ref, hf_hbm, o_ref, row_tok, buf, sem, *, n_assign, tm):
    i = pl.program_id(0)
    slot = i % 2
    nu = nu_ref[0]

    @pl.when(i == 0)
    def _():
        def clear(r, carry):
            row_tok[r] = 0
            return carry

        def scatter(blk, carry):
            base = blk * SCATTER_BATCH
            ds = [dest_ref[base + k] for k in range(SCATTER_BATCH)]
            for k in range(SCATTER_BATCH):
                row_tok[ds[k]] = (base + k) // TOP_K
            return carry

        lax.fori_loop(0, row_tok.shape[0], clear, 0, unroll=DMA_UNROLL)
        lax.fori_loop(0, n_assign // SCATTER_BATCH, scatter, 0)

    def row_copy(tok, sl, r):
        return pltpu.make_async_copy(hf_hbm.at[pl.ds(tok, 1), :], buf.at[sl, pl.ds(r, 1), :], sem.at[sl])

    def start_tile(tile, sl):
        def issue(r, carry):
            row_copy(row_tok[tile * tm + r], sl, r).start()
            return carry

        lax.fori_loop(0, tm, issue, 0, unroll=DMA_UNROLL)

    @pl.when((i == 0) & (nu > 0))
    def _():
        start_tile(i, slot)

    @pl.when(i + 1 < nu)
    def _():
        start_tile(i + 1, 1 - slot)

    @pl.when(i < nu)
    def _():
        def drain(r, carry):
            row_copy(0, slot, 0).wait()
            return carry

        lax.fori_loop(0, tm, drain, 0, unroll=DMA_UNROLL)
        o_ref[...] = buf[slot]

    @pl.when(i >= nu)
    def _():
        o_ref[...] = jnp.zeros(o_ref.shape, o_ref.dtype)


def dispatch(dest, n_used, hf, *, n_tiles, tm):
    n_assign = dest.shape[0]
    assert n_assign % SCATTER_BATCH == 0
    width = hf.shape[1]
    grid_spec = pltpu.PrefetchScalarGridSpec(
        num_scalar_prefetch=2,
        grid=(n_tiles,),
        in_specs=[pl.BlockSpec(memory_space=pl.ANY)],
        out_specs=pl.BlockSpec((tm, width), lambda i, d, nu: (i, 0)),
        scratch_shapes=[pltpu.SMEM((n_tiles * tm,), jnp.int32),
                        pltpu.VMEM((2, tm, width), hf.dtype),
                        pltpu.SemaphoreType.DMA((2,))])
    return pl.pallas_call(
        functools.partial(_dispatch_kernel, n_assign=n_assign, tm=tm),
        out_shape=jax.ShapeDtypeStruct((n_tiles * tm, width), hf.dtype),
        grid_spec=grid_spec,
        compiler_params=_cparams(("arbitrary",)),
        name="moe_dispatch",
    )(dest, n_used, hf)


def _combine_kernel(dest_ref, x_ref, rt_ref, y_hbm, o_ref, buf, sem, *, tn):
    i = pl.program_id(0)
    n = pl.num_programs(0)
    slot = i % 2

    def row_copy(d, sl, k, rr):
        return pltpu.make_async_copy(y_hbm.at[pl.ds(d, 1), :], buf.at[sl, k, pl.ds(rr, 1), :],
                                     sem.at[sl])

    def start_tile(tile, sl):
        def issue(rr, carry):
            for k in range(TOP_K):
                row_copy(dest_ref[(tile * tn + rr) * TOP_K + k], sl, k, rr).start()
            return carry

        lax.fori_loop(0, tn, issue, 0, unroll=DMA_UNROLL)

    @pl.when(i == 0)
    def _():
        start_tile(i, slot)

    @pl.when(i + 1 < n)
    def _():
        start_tile(i + 1, 1 - slot)

    def drain(rr, carry):
        for k in range(TOP_K):
            row_copy(0, slot, k, 0).wait()
        return carry

    lax.fori_loop(0, tn, drain, 0, unroll=DMA_UNROLL)

    rt = rt_ref[...]
    o_ref[...] = (x_ref[...] + (rt[:, 2:3] * buf[slot, 0] + rt[:, 3:4] * buf[slot, 1]))


def combine(dest, x, route, y, *, tn):
    m = x.shape[0]
    assert m % tn == 0
    grid_spec = pltpu.PrefetchScalarGridSpec(
        num_scalar_prefetch=1,
        grid=(m // tn,),
        in_specs=[pl.BlockSpec((tn, D_MODEL), lambda i, d: (i, 0)),
                  pl.BlockSpec((tn, ROUTE_LANES), lambda i, d: (i, 0)),
                  pl.BlockSpec(memory_space=pl.ANY)],
        out_specs=pl.BlockSpec((tn, D_MODEL), lambda i, d: (i, 0)),
        scratch_shapes=[pltpu.VMEM((2, TOP_K, tn, D_MODEL), F32),
                        pltpu.SemaphoreType.DMA((2,))])
    return pl.pallas_call(
        functools.partial(_combine_kernel, tn=tn),
        out_shape=jax.ShapeDtypeStruct((m, D_MODEL), F32),
        grid_spec=grid_spec,
        compiler_params=_cparams(("arbitrary",)),
        name="moe_combine",
    )(dest, x, route, y)


def _norm_kernel(x_ref, g_ref, o_ref):
    o_ref[...] = _rms(x_ref[...], g_ref[...])


def final_norm(x, gain, *, tm):
    m = x.shape[0]
    assert m % tm == 0
    return pl.pallas_call(
        _norm_kernel,
        out_shape=jax.ShapeDtypeStruct((m, D_MODEL), F32),
        grid=(m // tm,),
        in_specs=[pl.BlockSpec((tm, D_MODEL), lambda i: (i, 0)),
                  pl.BlockSpec((1, D_MODEL), lambda i: (0, 0))],
        out_specs=pl.BlockSpec((tm, D_MODEL), lambda i: (i, 0)),
        compiler_params=_cparams(("parallel",)),
        name="final_norm",
    )(x, gain)


def _rot_half(w):
    half = QK_ROPE // 2
    return jnp.concatenate([-w[..., half:], w[..., :half]], axis=-1)


def _lane_pad(w):
    return jnp.pad(w, [(0, 0)] * (w.ndim - 1) + [(0, ROPE_LANES - w.shape[-1])])


def _prep_w_in(w_in):
    o1 = Q_LORA
    o2 = o1 + KV_LORA
    o3 = o2 + QK_ROPE
    o4 = o3 + W_SSM
    o5 = o4 + W_POOL
    kpe = w_in[..., o2:o3]
    parts = [w_in[..., o5:], w_in[..., :o1], w_in[..., o3:o4], w_in[..., o4:o5], w_in[..., o1:o2],
             _lane_pad(kpe), _lane_pad(_rot_half(kpe))]
    return jnp.concatenate(parts, axis=-1).astype(BF16)


def _prep_w_q(w_q_up):
    depth = w_q_up.shape[0]
    w = w_q_up.reshape(depth, Q_LORA, N_HEADS, QK_NOPE + QK_ROPE)
    nope = w[..., :QK_NOPE].reshape(depth, Q_LORA, N_HEADS * QK_NOPE)
    rope = w[..., QK_NOPE:]
    rope_p = _lane_pad(rope).reshape(depth, Q_LORA, N_HEADS * ROPE_LANES)
    rot_p = _lane_pad(_rot_half(rope)).reshape(depth, Q_LORA, N_HEADS * ROPE_LANES)
    return jnp.concatenate([nope, rope_p, rot_p], axis=-1).astype(BF16)


def _block_diag_heads(w):
    depth, _, r, c = w.shape
    ng = N_HEADS // HEAD_GROUP
    w5 = w.reshape(depth, ng, HEAD_GROUP, r, c)
    eye = jnp.eye(HEAD_GROUP, dtype=w.dtype)
    out = w5[:, :, :, :, None, :] * eye[None, None, :, None, :, None]
    return out.reshape(depth, ng, HEAD_GROUP * r, HEAD_GROUP * c)


def _rope_tables(pos):
    half = QK_ROPE // 2
    freqs = ROPE_BASE ** (-jnp.arange(half, dtype=F32) / half)
    ang = pos.astype(F32)[:, None] * freqs[None, :]
    cos, sin = jnp.cos(ang), jnp.sin(ang)
    return (_lane_pad(jnp.concatenate([cos, cos], axis=-1)),
            _lane_pad(jnp.concatenate([sin, sin], axis=-1)))


def _tile_diag(w):
    depth, _, r, c = w.shape
    w5 = w.reshape(depth, SSM_TILES, SSM_TILE_GROUPS, r, c)
    eye = jnp.eye(SSM_TILE_GROUPS, dtype=w.dtype)
    return w5[:, :, :, :, None, :] * eye[None, None, :, None, :, None]


def _prep_ssm(lam_re, lam_im, log_dt, b_re, b_im, c_re, c_im, steps_list):
    hi = lax.Precision.HIGHEST
    depth = lam_re.shape[0]
    g, p, n, tc = SSM_GROUPS, SSM_STATE, SSM_GROUP, SSM_TC
    lam = lax.complex(lam_re.astype(F32), lam_im.astype(F32))
    dt = jnp.exp(log_dt.astype(F32))[..., None]
    ldt = lam * dt
    pw = jnp.exp(ldt[..., None] * jnp.arange(tc + 1, dtype=F32))
    lam_bar = pw[..., 1]
    b_bar = ((lam_bar - 1.0) / lam)[..., None] * lax.complex(b_re.astype(F32), b_im.astype(F32))
    c_c = lax.complex(c_re.astype(F32), c_im.astype(F32))
    kt = jnp.real(jnp.einsum("dgmp,dgpt,dgpn->dgtmn", c_c, pw[..., :tc], b_bar, precision=hi))
    s_idx = jnp.arange(tc)[:, None]
    t_idx = jnp.arange(tc)[None, :]
    lag = jnp.clip(t_idx - s_idx, 0, tc - 1)
    toep = jnp.where((t_idx >= s_idx)[None, None, :, :, None, None], kt[:, :, lag], 0.0)
    toep = toep.transpose(0, 1, 2, 5, 3, 4).reshape(depth, g, tc * n, tc * n)
    toep = _tile_diag(toep).reshape(depth, SSM_TILES, SSM_TILE_GROUPS, tc, n, SSM_TILE_GROUPS, tc, n)
    toep = toep.transpose(0, 1, 3, 2, 4, 6, 5, 7).reshape(depth, SSM_TILES, tc * LANES, tc * LANES)

    def wsum_for(steps):
        ex = jnp.clip(steps - 1 - jnp.arange(tc), 0, tc)
        v = pw[..., ex][..., None] * b_bar[:, :, :, None, :]
        v = jnp.where((jnp.arange(tc) < steps)[None, None, None, :, None], v, 0.0)
        v = v.transpose(0, 1, 3, 4, 2)
        w = jnp.stack([jnp.real(v), jnp.imag(v)], axis=-2)
        w = _tile_diag(w.reshape(depth, g, tc * n, 2 * p))
        w = w.reshape(depth, SSM_TILES, SSM_TILE_GROUPS, tc, n, SSM_TILE_GROUPS, 2, p)
        w = w.transpose(0, 1, 3, 2, 4, 6, 5, 7)
        return w.reshape(depth, SSM_TILES, tc * LANES, SSM_COLS).astype(BF16)

    def a_for(steps):
        a = pw[..., steps].reshape(depth, SSM_TILES, SSM_HALF)
        a = jnp.stack([jnp.real(a), jnp.imag(a)], axis=2)
        return jnp.pad(a, ((0, 0), (0, 0), (0, SUBLANES - 2), (0, 0)))

    ac = c_c.transpose(0, 1, 3, 2)[:, :, :, None, :] * pw[..., 1:tc + 1][..., None]
    wout = jnp.stack([jnp.real(ac), -jnp.imag(ac)], axis=2)
    wout = _tile_diag(wout.reshape(depth, g, 2 * p, tc * n))
    wout = wout.reshape(depth, SSM_TILES, SSM_TILE_GROUPS, 2, p, SSM_TILE_GROUPS, tc, n)
    wout = wout.transpose(0, 1, 3, 2, 4, 6, 5, 7)
    wout = wout.reshape(depth, SSM_TILES, SSM_COLS, tc * LANES).astype(BF16)
    out = dict(toep=toep.astype(BF16), wout=wout)
    for steps in steps_list:
        out[("wsum", steps)] = wsum_for(steps)
        out[("a", steps)] = a_for(steps)
    return out


def _state_pack(h):
    nb = h.shape[0]
    h5 = h.astype(F32).reshape(nb, SSM_TILES, SSM_TILE_GROUPS, SSM_STATE, 2)
    return h5.transpose(0, 1, 4, 2, 3).reshape(nb, SSM_TILES * SSM_COLS)


def _state_unpack(hp, dtype):
    nb = hp.shape[0]
    h5 = hp.reshape(nb, SSM_TILES, 2, SSM_TILE_GROUPS, SSM_STATE).transpose(0, 1, 3, 4, 2)
    return h5.reshape(nb, SSM_GROUPS, SSM_STATE, 2).astype(dtype)


def _routing_plan(route, tm, n_tiles):
    flat_e = route[:, :TOP_K].astype(jnp.int32).reshape(-1)
    onehot = (flat_e[:, None] == jnp.arange(N_EXPERTS, dtype=jnp.int32)[None, :]).astype(jnp.int32)
    csum = jnp.cumsum(onehot, axis=0)
    rank = jnp.sum((csum - onehot) * onehot, axis=1)
    counts = csum[-1]
    padded = (counts + tm - 1) // tm * tm
    pad_end = jnp.cumsum(padded)
    pad_start = pad_end - padded
    dest = (jnp.sum(onehot * pad_start[None, :], axis=1) + rank).astype(jnp.int32)
    n_used = (pad_end[-1:] // tm).astype(jnp.int32)
    tile_start = jnp.arange(n_tiles, dtype=jnp.int32) * tm
    tile_expert = jnp.minimum(jnp.sum((tile_start[:, None] >= pad_end[None, :]).astype(jnp.int32), axis=1),
                              N_EXPERTS - 1).astype(jnp.int32)
    return dest, tile_expert, n_used


TM_BIG_OPTIONS = (1088, 544, 272, 136)
TM_MID_OPTIONS = (544, 272, 136)
TN_COMBINE_OPTIONS = (128, 64, 32, 16, 8)
TM_MOE = 512
TN_IN = 1024
TN_MERGE = 512
TF_UP = 512
TN_DOWN = 1024
TQ_ATTN = 128
TK_ATTN = 512
PAGES_PER_CHUNK = 32


def _largest_divisor(n, options):
    for o in options:
        if n % o == 0:
            return o
    raise ValueError(f"no tile in {options} divides {n}")


def kernel(x_prompt, x_sample, cache_ckv, cache_kpe, state_ssm, state_pool, page_table, attn_norm_g, w_in, q_norm_g, w_q_up, kv_norm_g, w_uk, w_uv, ssm_lam_re, ssm_lam_im, ssm_log_dt, ssm_b_re, ssm_b_im, ssm_c_re, ssm_c_im, ssm_d, ssm_w_glu, pool_w, pool_scale, w_branch, w_out, ffn_norm_g, ffn_w1, ffn_w3, ffn_w2, moe_router, moe_w1, moe_w3, moe_w2, final_norm_g):
    bsz, seq, _ = x_prompt.shape
    db, t, _ = x_sample.shape
    depth = w_in.shape[0]
    past = page_table.shape[1] * PAGE_SIZE
    m_p = bsz * seq
    m_s = db * t
    m = m_p + m_s
    assert seq % SSM_TC == 0 and t <= SSM_TC and m_p % m_s == 0

    w_in2 = _prep_w_in(w_in)
    wq2 = _prep_w_q(w_q_up)
    wukbd = _block_diag_heads(w_uk.transpose(0, 2, 3, 1)).astype(BF16)
    wuvbd = _block_diag_heads(w_uv.transpose(0, 2, 1, 3)).astype(BF16)
    wuv_flat = w_uv.reshape(depth, KV_LORA, W_ATTN).astype(BF16)
    ssm_w = _prep_ssm(ssm_lam_re, ssm_lam_im, ssm_log_dt, ssm_b_re, ssm_b_im, ssm_c_re, ssm_c_im,
                      (SSM_TC, t))
    w_glu = ssm_w_glu.astype(BF16)
    pool_wb = pool_w.astype(BF16)
    wb = w_branch.astype(BF16)
    wo = w_out.astype(BF16)
    fw1, fw3 = ffn_w1[:, None], ffn_w3[:, None]
    fw2 = ffn_w2.astype(BF16)[:, None]
    mw2 = moe_w2.astype(BF16)
    w_router_t = moe_router.astype(F32).transpose(0, 2, 1)
    cache_kpe_t = cache_kpe.transpose(0, 1, 3, 2)

    pos = jnp.concatenate([jnp.tile(jnp.arange(seq), bsz), past + jnp.tile(jnp.arange(t), db)])
    cos_t, sin_t = _rope_tables(pos)
    pos_s = past + jnp.arange(t)
    cnt_s = jnp.concatenate(
        [jnp.broadcast_to(jnp.minimum(pos_s + 1, w).astype(F32)[:, None, None], (t, 1, POOL_GW))
         for w in POOL_WINDOWS], axis=-1)

    tm_big = _largest_divisor(m, TM_BIG_OPTIONS)
    tm_mid = _largest_divisor(m, TM_MID_OPTIONS)
    tn_combine = _largest_divisor(m, TN_COMBINE_OPTIONS)
    tk_attn = min(TK_ATTN, seq)
    pages_per_chunk = min(PAGES_PER_CHUNK, page_table.shape[1])
    n_dense_tiles = m // tm_mid
    te_dense = jnp.zeros((n_dense_tiles,), jnp.int32)
    nu_dense = jnp.full((1,), n_dense_tiles, jnp.int32)
    n_moe_tiles = (m * TOP_K) // TM_MOE + N_EXPERTS

    x = jnp.concatenate([x_prompt.reshape(m_p, D_MODEL), x_sample.reshape(m_s, D_MODEL)], axis=0)
    outs = [[] for _ in range(8)]
    for l in range(depth):
        z = in_proj(x, attn_norm_g[l][None], w_in2, layer=l, tm=tm_big, tn=TN_IN)
        qcat, c_lat, k_pe, kcat = qkv_post(z, cos_t, sin_t, q_norm_g[l][None], kv_norm_g[l][None],
                                           wq2, wukbd, layer=l, tm=tm_mid)
        oa_p = prompt_attention(qcat, kcat, wuvbd, layer=l, batch=bsz, seq=seq, tq=TQ_ATTN, tk=tk_attn)
        q3 = qcat[m_p:].reshape(db, t * N_HEADS, QCAT)
        knew3 = jnp.pad(kcat[m_p:].reshape(db, t, QCAT), ((0, 0), (0, NEW_KEYS - t), (0, 0)))
        oa_s = sample_attention(page_table, q3, knew3, cache_ckv, cache_kpe_t, wuv_flat,
                                layer=l, pc=pages_per_chunk)
        o_attn = jnp.concatenate([oa_p, oa_s.reshape(m_s, W_ATTN)], axis=0)
        y_p, hfin_p = ssm_scan(z, ssm_w["toep"], ssm_w[("wsum", SSM_TC)], ssm_w["wout"], ssm_w[("a", SSM_TC)],
                               jnp.zeros((bsz, SSM_TILES * SSM_COLS), F32),
                               layer=l, row_block=0, nseq=bsz, nk=seq // SSM_TC, steps=SSM_TC)
        y_s, hfin_s = ssm_scan(z, ssm_w["toep"], ssm_w[("wsum", t)], ssm_w["wout"], ssm_w[("a", t)],
                               _state_pack(state_ssm[l]),
                               layer=l, row_block=m_p // m_s, nseq=db, nk=1, steps=t)
        y_ssm = ssm_post(jnp.concatenate([y_p, y_s], axis=0), z, ssm_d[l][None], w_glu[l], tm=tm_mid)
        yp_p = pool_prompt(z, pool_wb[l], pool_scale[l][None], batch=bsz, seq=seq)
        u_pool = z[:, Z_POOL:Z_POOL + W_POOL]
        u_ext = jnp.concatenate([state_pool[l].astype(F32), u_pool[m_p:].reshape(db, t, W_POOL)], axis=1)
        yp_s = pool_sample(u_ext.transpose(1, 0, 2), cnt_s, pool_wb[l], pool_scale[l][None])
        y_pool = jnp.concatenate([yp_p, yp_s.transpose(1, 0, 2).reshape(m_s, W_POOL)], axis=0)
        mm = branch_merge(o_attn, y_ssm, y_pool, z, wb, layer=l, tm=tm_big, tn=TN_MERGE)
        j = l // 2
        if l % 2 == 0:
            xn, hfp = out_proj(mm, wo, x, ffn_norm_g[l][None], None, layer=l, moe_layer=None, tm=tm_mid)
            g_act = ffn_up(te_dense, nu_dense, hfp, fw1, fw3, layer=j, tm=tm_mid, tf=TF_UP)
            x = ffn_down(te_dense, nu_dense, g_act, fw2, xn, layer=j, tm=tm_mid, tn=TN_DOWN)
        else:
            xn, hfp, route = out_proj(mm, wo, x, ffn_norm_g[l][None], w_router_t, layer=l, moe_layer=j,
                                      tm=tm_mid)
            dest, tile_expert, n_used = _routing_plan(route, TM_MOE, n_moe_tiles)
            xs = dispatch(dest, n_used, hfp, n_tiles=n_moe_tiles, tm=TM_MOE)
            g_act = ffn_up(tile_expert, n_used, xs, moe_w1, moe_w3, layer=j, tm=TM_MOE, tf=TF_UP)
            y_sorted = ffn_down(tile_expert, n_used, g_act, mw2, None, layer=j, tm=TM_MOE, tn=TN_DOWN)
            x = combine(dest, xn, route, y_sorted, tn=tn_combine)
        outs[0].append(c_lat[:m_p].reshape(bsz, seq, KV_LORA))
        outs[1].append(k_pe[:m_p].reshape(bsz, seq, QK_ROPE))
        outs[2].append(_state_unpack(hfin_p, state_ssm.dtype))
        outs[3].append(u_pool[:m_p].reshape(bsz, seq, W_POOL)[:, seq - POOL_PAD:])
        outs[4].append(c_lat[m_p:].reshape(db, t, KV_LORA))
        outs[5].append(k_pe[m_p:].reshape(db, t, QK_ROPE))
        outs[6].append(_state_unpack(hfin_s, state_ssm.dtype))
        outs[7].append(u_ext[:, t:])
    y = final_norm(x, final_norm_g[None], tm=tm_mid)
    return (y[:m_p].reshape(bsz, seq, D_MODEL), y[m_p:].reshape(db, t, D_MODEL),
            *[jnp.stack(o) for o in outs])
```

```python
import functools

import jax
import jax.numpy as jnp
import numpy as np
from jax import lax
from jax.experimental import pallas as pl
from jax.experimental.pallas import tpu as pltpu

F32 = jnp.float32
BF16 = jnp.bfloat16
U32 = jnp.uint32

D_MODEL = 2048
N_HEADS = 16
QK_NOPE = 64
QK_ROPE = 32
V_HEAD = 64
W_ATTN = N_HEADS * V_HEAD
Q_LORA = 512
KV_LORA = 256
ROPE_BASE = 10000.0
SM_SCALE = (QK_NOPE + QK_ROPE) ** -0.5
PAGE_SIZE = 128
W_SSM = 512
SSM_GROUP = 16
SSM_GROUPS = W_SSM // SSM_GROUP
SSM_STATE = 64
W_POOL = 512
POOL_WINDOWS = (2, 4, 8, 16)
POOL_GW = W_POOL // len(POOL_WINDOWS)
POOL_PAD = max(POOL_WINDOWS) - 1
N_BRANCH = 3
D_FF = 5632
N_EXPERTS = 8
TOP_K = 2
EPS = 1e-6

LANES = 128
SUBLANES = 8
VMEM_LIMIT_BYTES = 56 * 1024 * 1024

ROPE_LANES = LANES
Z_GATE = 0
Z_QC = Z_GATE + N_BRANCH * D_MODEL
Z_SSM = Z_QC + Q_LORA
Z_POOL = Z_SSM + W_SSM
Z_KV = Z_POOL + W_POOL
Z_KPE = Z_KV + KV_LORA
Z_COLS = Z_KPE + 2 * ROPE_LANES
QCAT = KV_LORA + ROPE_LANES
HEAD_GROUP = 4


def _cparams(sem, vmem=VMEM_LIMIT_BYTES):
    return pltpu.CompilerParams(dimension_semantics=sem, vmem_limit_bytes=vmem)


def _lane_tile(x, n):
    return x if n == 1 else jnp.concatenate([x] * n, axis=1)


def _rms(x, g):
    ms = jnp.mean(x * x, axis=-1, keepdims=True)
    return x * lax.rsqrt(ms + EPS) * g


def _packed_width(d):
    return d // 2


def _pack_rows(h):
    c = h.shape[1] // 2
    bits = pltpu.bitcast(h.astype(BF16).astype(F32), U32)
    return (bits[:, :c] >> 16) | (bits[:, c:] & jnp.uint32(0xFFFF0000))


def _unpack_rows(p):
    lo = pltpu.bitcast(p << 16, F32).astype(BF16)
    hi = pltpu.bitcast(p & jnp.uint32(0xFFFF0000), F32).astype(BF16)
    return jnp.concatenate([lo, hi], axis=1)


IN_SUB = 256


def _in_proj_kernel(x_ref, g_ref, w_ref, z_ref, h_ref, *, n_gate_tiles):
    j = pl.program_id(1)

    @pl.when(j == 0)
    def _():
        h_ref[...] = _rms(x_ref[...], g_ref[...]).astype(BF16)

    h = h_ref[...]
    is_gate = j < n_gate_tiles
    for c in range(z_ref.shape[1] // IN_SUB):
        sl = slice(c * IN_SUB, (c + 1) * IN_SUB)
        acc = jnp.dot(h, w_ref[0, :, sl], preferred_element_type=F32)
        z_ref[:, sl] = jnp.where(is_gate, jax.nn.sigmoid(acc), acc)


def in_proj(x, gain, w, *, layer, tm, tn):
    m, d = x.shape
    n = w.shape[2]
    assert m % tm == 0 and n % tn == 0 and (N_BRANCH * D_MODEL) % tn == 0 and tn % IN_SUB == 0
    return pl.pallas_call(
        functools.partial(_in_proj_kernel, n_gate_tiles=(N_BRANCH * D_MODEL) // tn),
        out_shape=jax.ShapeDtypeStruct((m, n), F32),
        grid=(m // tm, n // tn),
        in_specs=[pl.BlockSpec((tm, d), lambda i, j: (i, 0)),
                  pl.BlockSpec((1, d), lambda i, j: (0, 0)),
                  pl.BlockSpec((1, d, tn), lambda i, j: (layer, 0, j))],
        out_specs=pl.BlockSpec((tm, tn), lambda i, j: (i, j)),
        scratch_shapes=[pltpu.VMEM((tm, d), BF16)],
        compiler_params=_cparams(("parallel", "arbitrary")),
        name="in_proj",
    )(x, gain, w)


def _qkv_kernel(qc_ref, kv_ref, kpe_ref, cos_ref, sin_ref, qg_ref, kvg_ref, wq_ref, wuk_ref,
                qcat_ref, c_ref, kpe_out_ref, kcat_ref):
    nope_w = N_HEADS * QK_NOPE
    rope_w = N_HEADS * ROPE_LANES
    cos = cos_ref[...]
    sin = sin_ref[...]
    qn = _rms(qc_ref[...], qg_ref[...]).astype(BF16)
    q = jnp.dot(qn, wq_ref[0], preferred_element_type=F32)
    cos_h = jnp.concatenate([cos] * N_HEADS, axis=1)
    sin_h = jnp.concatenate([sin] * N_HEADS, axis=1)
    q_pe = (q[:, nope_w:nope_w + rope_w] * cos_h
            + q[:, nope_w + rope_w:nope_w + 2 * rope_w] * sin_h).astype(BF16)
    q_nope = q[:, :nope_w].astype(BF16)
    gw = HEAD_GROUP * QK_NOPE
    for g in range(N_HEADS // HEAD_GROUP):
        qa = jnp.dot(q_nope[:, g * gw:(g + 1) * gw], wuk_ref[0, g],
                     preferred_element_type=F32).astype(BF16)
        for hl in range(HEAD_GROUP):
            h = g * HEAD_GROUP + hl
            qcat_ref[:, h * QCAT:h * QCAT + KV_LORA] = qa[:, hl * KV_LORA:(hl + 1) * KV_LORA]
            qcat_ref[:, h * QCAT + KV_LORA:(h + 1) * QCAT] = q_pe[:, h * ROPE_LANES:(h + 1) * ROPE_LANES]
    c = _rms(kv_ref[...], kvg_ref[...])
    c_ref[...] = c
    kraw = kpe_ref[...]
    kr = kraw[:, :ROPE_LANES] * cos + kraw[:, ROPE_LANES:] * sin
    kpe_out_ref[...] = kr[:, :QK_ROPE]
    kcat_ref[:, :KV_LORA] = c.astype(BF16)
    kcat_ref[:, KV_LORA:] = kr.astype(BF16)


def qkv_post(z, cos_t, sin_t, q_g, kv_g, wq, wuk, *, layer, tm):
    m = z.shape[0]
    assert m % tm == 0
    nq = wq.shape[2]
    return pl.pallas_call(
        _qkv_kernel,
        out_shape=(jax.ShapeDtypeStruct((m, N_HEADS * QCAT), BF16),
                   jax.ShapeDtypeStruct((m, KV_LORA), F32),
                   jax.ShapeDtypeStruct((m, QK_ROPE), F32),
                   jax.ShapeDtypeStruct((m, QCAT), BF16)),
        grid=(m // tm,),
        in_specs=[pl.BlockSpec((tm, Q_LORA), lambda i: (i, Z_QC // Q_LORA)),
                  pl.BlockSpec((tm, KV_LORA), lambda i: (i, Z_KV // KV_LORA)),
                  pl.BlockSpec((tm, 2 * ROPE_LANES), lambda i: (i, Z_KPE // (2 * ROPE_LANES))),
                  pl.BlockSpec((tm, ROPE_LANES), lambda i: (i, 0)),
                  pl.BlockSpec((tm, ROPE_LANES), lambda i: (i, 0)),
                  pl.BlockSpec((1, Q_LORA), lambda i: (0, 0)),
                  pl.BlockSpec((1, KV_LORA), lambda i: (0, 0)),
                  pl.BlockSpec((1, Q_LORA, nq), lambda i: (layer, 0, 0)),
                  pl.BlockSpec((1,) + wuk.shape[1:], lambda i: (layer, 0, 0, 0))],
        out_specs=(pl.BlockSpec((tm, N_HEADS * QCAT), lambda i: (i, 0)),
                   pl.BlockSpec((tm, KV_LORA), lambda i: (i, 0)),
                   pl.BlockSpec((tm, QK_ROPE), lambda i: (i, 0)),
                   pl.BlockSpec((tm, QCAT), lambda i: (i, 0))),
        compiler_params=_cparams(("parallel",)),
        name="qkv_post",
    )(z, z, z, cos_t, sin_t, q_g, kv_g, wq, wuk)


def _attn_out_proj(o, wuv_ref, rows):
    ob = o.astype(BF16)
    outs = []
    for g in range(N_HEADS // HEAD_GROUP):
        wide = jnp.concatenate(
            [ob[(g * HEAD_GROUP + hl) * rows:(g * HEAD_GROUP + hl + 1) * rows, :]
             for hl in range(HEAD_GROUP)], axis=1)
        outs.append(jnp.dot(wide, wuv_ref[0, g], preferred_element_type=F32))
    return jnp.concatenate(outs, axis=1)


ATTN_ROW_GROUPS = 4


def _prompt_attn_kernel(q_ref, k_ref, wuv_ref, o_ref, m_ref, l_ref, acc_ref, *, tq, tk):
    i = pl.program_id(1)
    rows = N_HEADS * tq
    q = jnp.concatenate([q_ref[:, h * QCAT:(h + 1) * QCAT] for h in range(N_HEADS)], axis=0)
    m_ref[...] = jnp.full(m_ref.shape, -jnp.inf, F32)
    l_ref[...] = jnp.zeros(l_ref.shape, F32)
    acc_ref[...] = jnp.zeros(acc_ref.shape, F32)

    gr = rows // ATTN_ROW_GROUPS

    def chunk(j, masked):
        start = pl.multiple_of(j * tk, tk)
        k = k_ref[pl.ds(start, tk), :]
        if masked:
            qpos = i * tq + (lax.broadcasted_iota(jnp.int32, (gr, tk), 0) & (tq - 1))
            kpos = start + lax.broadcasted_iota(jnp.int32, (gr, tk), 1)
            visible = kpos <= qpos
        for g in range(ATTN_ROW_GROUPS):
            rs = slice(g * gr, (g + 1) * gr)
            s = lax.dot_general(q[rs], k, (((1,), (1,)), ((), ())), preferred_element_type=F32) * SM_SCALE
            if masked:
                s = jnp.where(visible, s, -jnp.inf)
            m_old = m_ref[rs, :]
            m_new = jnp.maximum(m_old, jnp.max(s, axis=1, keepdims=True))
            alpha = jnp.exp(m_old - m_new)
            p = jnp.exp(s - _lane_tile(m_new, tk // LANES))
            l_ref[rs, :] = alpha * l_ref[rs, :] + jnp.sum(p, axis=1, keepdims=True)
            acc_ref[rs, :] = (_lane_tile(alpha, KV_LORA // LANES) * acc_ref[rs, :]
                              + jnp.dot(p.astype(BF16), k[:, :KV_LORA], preferred_element_type=F32))
            m_ref[rs, :] = m_new

    n_full = (i * tq) // tk

    def full_body(j, carry):
        chunk(j, False)
        return carry

    lax.fori_loop(0, n_full, full_body, 0)
    chunk(n_full, True)
    o = acc_ref[...] / _lane_tile(l_ref[...], KV_LORA // LANES)
    o_ref[...] = _attn_out_proj(o, wuv_ref, tq).astype(o_ref.dtype)


def prompt_attention(qcat, kcat, wuv, *, layer, batch, seq, tq, tk):
    assert seq % tq == 0 and seq % tk == 0 and tk % tq == 0 and (tq & (tq - 1)) == 0
    nq = seq // tq
    rows = N_HEADS * tq
    return pl.pallas_call(
        functools.partial(_prompt_attn_kernel, tq=tq, tk=tk),
        out_shape=jax.ShapeDtypeStruct((batch * seq, W_ATTN), BF16),
        grid=(batch, nq),
        in_specs=[pl.BlockSpec((tq, N_HEADS * QCAT), lambda b, i: (b * nq + i, 0)),
                  pl.BlockSpec((seq, QCAT), lambda b, i: (b, 0)),
                  pl.BlockSpec((1,) + wuv.shape[1:], lambda b, i: (layer, 0, 0, 0))],
        out_specs=pl.BlockSpec((tq, W_ATTN), lambda b, i: (b * nq + i, 0)),
        scratch_shapes=[pltpu.VMEM((rows, LANES), F32), pltpu.VMEM((rows, LANES), F32),
                        pltpu.VMEM((rows, KV_LORA), F32)],
        compiler_params=_cparams(("parallel", "arbitrary")),
        name="prompt_attn",
    )(qcat, kcat, wuv)


NEW_KEYS = LANES
SAMPLE_KEY_BLOCKS = 1


def _sample_attn_kernel(pt_ref, q_ref, knew_ref, ckv_hbm, kpe_hbm, wuv_ref, o_ref,
                        ckv_buf, kpe_buf, sem, m_ref, l_ref, acc_ref,
                        *, layer, n_chunks, pc, t):
    b = pl.program_id(0)
    c = pl.program_id(1)
    total = pl.num_programs(0) * n_chunks
    step = b * n_chunks + c
    slot = step % 2
    rows = t * N_HEADS

    def copies(bb, cc, sl):
        out = []
        for p in range(pc):
            page = pt_ref[bb, cc * pc + p]
            out.append(pltpu.make_async_copy(
                ckv_hbm.at[layer, page], ckv_buf.at[sl, pl.ds(p * PAGE_SIZE, PAGE_SIZE), :],
                sem.at[sl, 0]))
            out.append(pltpu.make_async_copy(kpe_hbm.at[layer, page], kpe_buf.at[sl, p], sem.at[sl, 1]))
        return out

    @pl.when(step == 0)
    def _():
        for cp in copies(b, c, slot):
            cp.start()

    @pl.when(step + 1 < total)
    def _():
        last = c + 1 == n_chunks
        nb = jnp.where(last, b + 1, b)
        nc = jnp.where(last, 0, c + 1)
        for cp in copies(nb, nc, 1 - slot):
            cp.start()

    @pl.when(c == 0)
    def _():
        m_ref[...] = jnp.full(m_ref.shape, -jnp.inf, F32)
        l_ref[...] = jnp.zeros(l_ref.shape, F32)
        acc_ref[...] = jnp.zeros(acc_ref.shape, F32)

    pltpu.make_async_copy(ckv_buf.at[slot], ckv_buf.at[slot], sem.at[slot, 0]).wait()
    pltpu.make_async_copy(kpe_buf.at[slot], kpe_buf.at[slot], sem.at[slot, 1]).wait()

    q = q_ref[0]
    nt = (((1,), (1,)), ((), ()))

    def partial_softmax(s, v):
        m_blk = jnp.max(s, axis=1, keepdims=True)
        p = jnp.exp(s - m_blk)
        return m_blk, jnp.sum(p, axis=1, keepdims=True), jnp.dot(p.astype(BF16), v, preferred_element_type=F32)

    def merge(parts):
        m_old = m_ref[...]
        m_new = m_old
        for m_blk, _, _ in parts:
            m_new = jnp.maximum(m_new, m_blk)
        alpha = jnp.exp(m_old - m_new)
        l_new = alpha * l_ref[...]
        acc = _lane_tile(alpha, KV_LORA // LANES) * acc_ref[...]
        for m_blk, l_blk, o_blk in parts:
            w = jnp.exp(m_blk - m_new)
            l_new = l_new + w * l_blk
            acc = acc + _lane_tile(w, KV_LORA // LANES) * o_blk
        m_ref[...] = m_new
        l_ref[...] = l_new
        acc_ref[...] = acc

    sub_pages = pc // SAMPLE_KEY_BLOCKS
    sub_keys = sub_pages * PAGE_SIZE
    parts = []
    for sb in range(SAMPLE_KEY_BLOCKS):
        ck = ckv_buf[slot, sb * sub_keys:(sb + 1) * sub_keys, :].astype(BF16)
        kr_t = jnp.concatenate([kpe_buf[slot, sb * sub_pages + p] for p in range(sub_pages)], axis=1)
        kr_t = jnp.concatenate([kr_t, jnp.zeros((ROPE_LANES - QK_ROPE, sub_keys), F32)],
                               axis=0).astype(BF16)
        s = (lax.dot_general(q[:, :KV_LORA], ck, nt, preferred_element_type=F32)
             + jnp.dot(q[:, KV_LORA:], kr_t, preferred_element_type=F32)) * SM_SCALE
        parts.append(partial_softmax(s, ck))
    merge(parts)

    @pl.when(c == n_chunks - 1)
    def _():
        kn = knew_ref[0]
        sn = lax.dot_general(q, kn, nt, preferred_element_type=F32) * SM_SCALE
        qi = lax.broadcasted_iota(jnp.int32, (rows, NEW_KEYS), 0) // N_HEADS
        ki = lax.broadcasted_iota(jnp.int32, (rows, NEW_KEYS), 1)
        sn = jnp.where(ki <= qi, sn, -jnp.inf)
        merge([partial_softmax(sn, kn[:, :KV_LORA])])
        o = (acc_ref[...] / _lane_tile(l_ref[...], KV_LORA // LANES)).astype(BF16)
        full = jnp.dot(o, wuv_ref[0], preferred_element_type=F32)
        head = lax.broadcasted_iota(jnp.int32, (rows, W_ATTN), 0) % N_HEADS
        col_head = lax.broadcasted_iota(jnp.int32, (rows, W_ATTN), 1) // V_HEAD
        own = jnp.where(head == col_head, full, 0.0)
        o_ref[0] = jnp.concatenate(
            [jnp.sum(own[tt * N_HEADS:(tt + 1) * N_HEADS, :], axis=0, keepdims=True) for tt in range(t)],
            axis=0).astype(o_ref.dtype)


def sample_attention(page_table, q3, knew3, cache_ckv, cache_kpe_t, wuv_flat, *, layer, pc):
    db, n_pages = page_table.shape
    rows = q3.shape[1]
    t = rows // N_HEADS
    assert n_pages % pc == 0 and pc % SAMPLE_KEY_BLOCKS == 0 and t <= NEW_KEYS
    n_chunks = n_pages // pc
    n_keys = pc * PAGE_SIZE
    grid_spec = pltpu.PrefetchScalarGridSpec(
        num_scalar_prefetch=1,
        grid=(db, n_chunks),
        in_specs=[pl.BlockSpec((1, rows, QCAT), lambda b, c, pt: (b, 0, 0)),
                  pl.BlockSpec((1, NEW_KEYS, QCAT), lambda b, c, pt: (b, 0, 0)),
                  pl.BlockSpec(memory_space=pl.ANY),
                  pl.BlockSpec(memory_space=pl.ANY),
                  pl.BlockSpec((1, KV_LORA, W_ATTN), lambda b, c, pt: (layer, 0, 0))],
        out_specs=pl.BlockSpec((1, t, W_ATTN), lambda b, c, pt: (b, 0, 0)),
        scratch_shapes=[pltpu.VMEM((2, n_keys, KV_LORA), F32),
                        pltpu.VMEM((2, pc, QK_ROPE, PAGE_SIZE), F32),
                        pltpu.SemaphoreType.DMA((2, 2)),
                        pltpu.VMEM((rows, LANES), F32),
                        pltpu.VMEM((rows, LANES), F32),
                        pltpu.VMEM((rows, KV_LORA), F32)])
    return pl.pallas_call(
        functools.partial(_sample_attn_kernel, layer=layer, n_chunks=n_chunks, pc=pc, t=t),
        out_shape=jax.ShapeDtypeStruct((db, t, W_ATTN), BF16),
        grid_spec=grid_spec,
        compiler_params=_cparams(("arbitrary", "arbitrary")),
        name="sample_attn",
    )(page_table, q3, knew3, cache_ckv, cache_kpe_t, wuv_flat)


SSM_TC = 8
SSM_TILE_GROUPS = LANES // SSM_GROUP
SSM_TILES = W_SSM // LANES
SSM_HALF = SSM_TILE_GROUPS * SSM_STATE
SSM_COLS = 2 * SSM_HALF


def _ssm_kernel(u_ref, toep_ref, wsum_ref, wout_ref, a_ref, h0_ref, y_ref, hfin_ref,
                s_scr, hp_scr, *, nseq, nk, steps):
    r = nseq * nk
    kw = steps * LANES
    u = jnp.concatenate([u_ref[pl.ds(s, r, stride=steps), :].astype(BF16) for s in range(steps)],
                        axis=1)
    sk_all = jnp.dot(u, wsum_ref[0, 0, :kw, :], preferred_element_type=F32)
    a = a_ref[0, 0]
    a_re = a[0:1, :]
    a_im = a[1:2, :]
    h0 = h0_ref[...]
    n_slab = SSM_COLS // LANES
    if nk == 1:
        hp = h0
        h_re, h_im = h0[:, :SSM_HALF], h0[:, SSM_HALF:]
        hfin_ref[...] = jnp.concatenate(
            [a_re * h_re - a_im * h_im + sk_all[:, :SSM_HALF],
             a_re * h_im + a_im * h_re + sk_all[:, SSM_HALF:]], axis=1)
    else:
        for j in range(n_slab):
            s_scr[j] = sk_all[:, j * LANES:(j + 1) * LANES]
        a_re_b = jnp.broadcast_to(a_re, (nseq, SSM_HALF))
        a_im_b = jnp.broadcast_to(a_im, (nseq, SSM_HALF))

        def body(k, h):
            for j in range(n_slab):
                hp_scr[j, pl.ds(k, nseq, stride=nk), :] = h[:, j * LANES:(j + 1) * LANES]
            sk = jnp.concatenate([s_scr[j, pl.ds(k, nseq, stride=nk), :] for j in range(n_slab)], axis=1)
            h_re, h_im = h[:, :SSM_HALF], h[:, SSM_HALF:]
            return jnp.concatenate(
                [a_re_b * h_re - a_im_b * h_im + sk[:, :SSM_HALF],
                 a_re_b * h_im + a_im_b * h_re + sk[:, SSM_HALF:]], axis=1)

        hfin_ref[...] = lax.fori_loop(0, nk, body, h0)
        hp = jnp.concatenate([hp_scr[j] for j in range(n_slab)], axis=1)
    y = (jnp.dot(u, toep_ref[0, 0, :kw, :], preferred_element_type=F32)
         + jnp.dot(hp.astype(BF16), wout_ref[0, 0], preferred_element_type=F32))
    for tt in range(steps):
        y_ref[pl.ds(tt, r, stride=steps), :] = y[:, tt * LANES:(tt + 1) * LANES]


def ssm_scan(z, toep, wsum, wout, a, h0, *, layer, row_block, nseq, nk, steps):
    rows = nseq * nk * steps
    r = nseq * nk
    w_spec = pl.BlockSpec((1, 1, SSM_TC * LANES, SSM_COLS), lambda j: (layer, j, 0, 0))
    return pl.pallas_call(
        functools.partial(_ssm_kernel, nseq=nseq, nk=nk, steps=steps),
        out_shape=(jax.ShapeDtypeStruct((rows, W_SSM), F32),
                   jax.ShapeDtypeStruct((nseq, SSM_TILES * SSM_COLS), F32)),
        grid=(SSM_TILES,),
        in_specs=[pl.BlockSpec((rows, LANES), lambda j: (row_block, Z_SSM // LANES + j)),
                  w_spec, w_spec,
                  pl.BlockSpec((1, 1, SSM_COLS, SSM_TC * LANES), lambda j: (layer, j, 0, 0)),
                  pl.BlockSpec((1, 1, SUBLANES, SSM_HALF), lambda j: (layer, j, 0, 0)),
                  pl.BlockSpec((nseq, SSM_COLS), lambda j: (0, j))],
        out_specs=(pl.BlockSpec((rows, LANES), lambda j: (0, j)),
                   pl.BlockSpec((nseq, SSM_COLS), lambda j: (0, j))),
        scratch_shapes=[pltpu.VMEM((SSM_COLS // LANES, r, LANES), F32),
                        pltpu.VMEM((SSM_COLS // LANES, r, LANES), F32)],
        compiler_params=_cparams(("parallel",)),
        name="ssm_scan",
    )(z, toep, wsum, wout, a, h0)


def _gelu_tanh(x):
    return 0.5 * x * (1.0 + jnp.tanh(np.sqrt(2.0 / np.pi).astype(np.float32)
                                     * (x + np.float32(0.044715) * (x * x * x))))


def _ssm_post_kernel(y_ref, u_ref, d_ref, wg_ref, o_ref):
    y = y_ref[...] + d_ref[...] * u_ref[...]
    zz = _gelu_tanh(y)
    gate = jax.nn.sigmoid(jnp.dot(zz.astype(BF16), wg_ref[...], preferred_element_type=F32))
    o_ref[...] = (zz * gate).astype(o_ref.dtype)


def ssm_post(y_raw, z, d_skip, w_glu, *, tm):
    m = y_raw.shape[0]
    assert m % tm == 0
    return pl.pallas_call(
        _ssm_post_kernel,
        out_shape=jax.ShapeDtypeStruct((m, W_SSM), BF16),
        grid=(m // tm,),
        in_specs=[pl.BlockSpec((tm, W_SSM), lambda i: (i, 0)),
                  pl.BlockSpec((tm, W_SSM), lambda i: (i, Z_SSM // W_SSM)),
                  pl.BlockSpec((1, W_SSM), lambda i: (0, 0)),
                  pl.BlockSpec((W_SSM, W_SSM), lambda i: (0, 0))],
        out_specs=pl.BlockSpec((tm, W_SSM), lambda i: (i, 0)),
        compiler_params=_cparams(("parallel",)),
        name="ssm_post",
    )(y_raw, z, d_skip, w_glu)


POOL_HALO = 2 * SUBLANES


def _pool_mix(diff, pw_ref, ps_ref):
    outs = [jnp.dot(diff[:, g * POOL_GW:(g + 1) * POOL_GW].astype(BF16), pw_ref[g],
                    preferred_element_type=F32) for g in range(len(POOL_WINDOWS))]
    return jnp.concatenate(outs, axis=1) * ps_ref[...]


def _pool_prompt_kernel(u_ref, pw_ref, ps_ref, o_ref, ext):
    seq = u_ref.shape[0]
    ext[0:POOL_HALO, :] = jnp.zeros((POOL_HALO, W_POOL), F32)
    ext[POOL_HALO:, :] = u_ref[...]
    pos1 = lax.broadcasted_iota(jnp.int32, (seq, POOL_GW), 0) + 1
    means = []
    for g, w in enumerate(POOL_WINDOWS):
        tot = ext[POOL_HALO:POOL_HALO + seq, g * POOL_GW:(g + 1) * POOL_GW]
        for k in range(1, w):
            tot = tot + ext[POOL_HALO - k:POOL_HALO - k + seq, g * POOL_GW:(g + 1) * POOL_GW]
        means.append(tot / jnp.minimum(pos1, w).astype(F32))
    diff = jnp.concatenate(means, axis=1) - u_ref[...]
    o_ref[...] = _pool_mix(diff, pw_ref, ps_ref).astype(o_ref.dtype)


def pool_prompt(z, pool_w, pool_scale, *, batch, seq):
    return pl.pallas_call(
        _pool_prompt_kernel,
        out_shape=jax.ShapeDtypeStruct((batch * seq, W_POOL), BF16),
        grid=(batch,),
        in_specs=[pl.BlockSpec((seq, W_POOL), lambda b: (b, Z_POOL // W_POOL)),
                  pl.BlockSpec(pool_w.shape, lambda b: (0, 0, 0)),
                  pl.BlockSpec((1, W_POOL), lambda b: (0, 0))],
        out_specs=pl.BlockSpec((seq, W_POOL), lambda b: (b, 0)),
        scratch_shapes=[pltpu.VMEM((POOL_HALO + seq, W_POOL), F32)],
        compiler_params=_cparams(("parallel",)),
        name="pool_prompt",
    )(z, pool_w, pool_scale)


def _pool_sample_kernel(x_ref, cnt_ref, pw_ref, ps_ref, o_ref, *, t):
    for tt in range(t):
        means = []
        for g, w in enumerate(POOL_WINDOWS):
            sl = slice(g * POOL_GW, (g + 1) * POOL_GW)
            tot = x_ref[POOL_PAD + tt][:, sl]
            for k in range(1, w):
                tot = tot + x_ref[POOL_PAD + tt - k][:, sl]
            means.append(tot / cnt_ref[tt][:, sl])
        diff = jnp.concatenate(means, axis=1) - x_ref[POOL_PAD + tt]
        o_ref[tt] = _pool_mix(diff, pw_ref, ps_ref).astype(o_ref.dtype)


def pool_sample(x_tm, cnt, pool_w, pool_scale):
    rows, nseq, _ = x_tm.shape
    t = rows - POOL_PAD
    return pl.pallas_call(
        functools.partial(_pool_sample_kernel, t=t),
        out_shape=jax.ShapeDtypeStruct((t, nseq, W_POOL), BF16),
        grid=(1,),
        in_specs=[pl.BlockSpec(x_tm.shape, lambda i: (0, 0, 0)),
                  pl.BlockSpec(cnt.shape, lambda i: (0, 0, 0)),
                  pl.BlockSpec(pool_w.shape, lambda i: (0, 0, 0)),
                  pl.BlockSpec((1, W_POOL), lambda i: (0, 0))],
        out_specs=pl.BlockSpec((t, nseq, W_POOL), lambda i: (0, 0, 0)),
        compiler_params=_cparams(("arbitrary",)),
        name="pool_sample",
    )(x_tm, cnt, pool_w, pool_scale)


def _merge_kernel(oa_ref, ys_ref, yp_ref, g0_ref, g1_ref, g2_ref, wa_ref, ws_ref, wp_ref, m_ref):
    m = (g0_ref[...] * jnp.dot(oa_ref[...], wa_ref[0], preferred_element_type=F32)
         + g1_ref[...] * jnp.dot(ys_ref[...], ws_ref[0], preferred_element_type=F32)
         + g2_ref[...] * jnp.dot(yp_ref[...], wp_ref[0], preferred_element_type=F32))
    m_ref[...] = m.astype(m_ref.dtype)


def branch_merge(o_attn, y_ssm, y_pool, z, w_branch, *, layer, tm, tn):
    m = o_attn.shape[0]
    assert m % tm == 0 and D_MODEL % tn == 0
    nj = D_MODEL // tn
    return pl.pallas_call(
        _merge_kernel,
        out_shape=jax.ShapeDtypeStruct((m, D_MODEL), BF16),
        grid=(m // tm, nj),
        in_specs=[pl.BlockSpec((tm, W_ATTN), lambda i, j: (i, 0)),
                  pl.BlockSpec((tm, W_SSM), lambda i, j: (i, 0)),
                  pl.BlockSpec((tm, W_POOL), lambda i, j: (i, 0)),
                  pl.BlockSpec((tm, tn), lambda i, j: (i, j)),
                  pl.BlockSpec((tm, tn), lambda i, j: (i, nj + j)),
                  pl.BlockSpec((tm, tn), lambda i, j: (i, 2 * nj + j)),
                  pl.BlockSpec((1, W_ATTN, tn), lambda i, j: (layer, 0, j)),
                  pl.BlockSpec((1, W_SSM, tn), lambda i, j: (layer, W_ATTN // W_SSM, j)),
                  pl.BlockSpec((1, W_POOL, tn), lambda i, j: (layer, (W_ATTN + W_SSM) // W_POOL, j))],
        out_specs=pl.BlockSpec((tm, tn), lambda i, j: (i, j)),
        compiler_params=_cparams(("parallel", "arbitrary")),
        name="branch_merge",
    )(o_attn, y_ssm, y_pool, z, z, z, w_branch, w_branch, w_branch)


ROUTE_LANES = LANES


def _route(hf, wr_ref):
    wr = wr_ref[0]
    best1 = jnp.sum(hf * wr[0:1, :], axis=1, keepdims=True)
    idx1 = jnp.zeros(best1.shape, jnp.int32)
    best2 = jnp.full(best1.shape, -jnp.inf, F32)
    idx2 = jnp.zeros(best1.shape, jnp.int32)
    for e in range(1, N_EXPERTS):
        le = jnp.sum(hf * wr[e:e + 1, :], axis=1, keepdims=True)
        gt1 = le > best1
        gt2 = le > best2
        best2 = jnp.where(gt1, best1, jnp.where(gt2, le, best2))
        idx2 = jnp.where(gt1, idx1, jnp.where(gt2, e, idx2))
        best1 = jnp.where(gt1, le, best1)
        idx1 = jnp.where(gt1, e, idx1)
    ex = jnp.exp(best2 - best1)
    g1 = 1.0 / (1.0 + ex)
    g2 = ex / (1.0 + ex)
    lane = lax.broadcasted_iota(jnp.int32, (hf.shape[0], ROUTE_LANES), 1)
    return jnp.where(lane == 0, idx1.astype(F32),
                     jnp.where(lane == 1, idx2.astype(F32),
                               jnp.where(lane == 2, g1, jnp.where(lane == 3, g2, 0.0))))


def _out_proj_kernel(m_ref, w_ref, x_ref, g_ref, *rest, route):
    if route:
        wr_ref, xn_ref, hp_ref, rt_ref = rest
    else:
        xn_ref, hp_ref = rest
    xn = x_ref[...] + jnp.dot(m_ref[...], w_ref[0], preferred_element_type=F32)
    xn_ref[...] = xn
    hf = _rms(xn, g_ref[...])
    hp_ref[...] = _pack_rows(hf)
    if route:
        rt_ref[...] = _route(hf, wr_ref)


def out_proj(mm, w_out, x, gain, w_router_t, *, layer, moe_layer, tm):
    m = x.shape[0]
    assert m % tm == 0
    route = moe_layer is not None
    row = lambda i: (i, 0)
    in_specs = [pl.BlockSpec((tm, D_MODEL), row),
                pl.BlockSpec((1, D_MODEL, D_MODEL), lambda i: (layer, 0, 0)),
                pl.BlockSpec((tm, D_MODEL), row),
                pl.BlockSpec((1, D_MODEL), lambda i: (0, 0))]
    args = [mm, w_out, x, gain]
    pw = _packed_width(D_MODEL)
    out_shape = [jax.ShapeDtypeStruct((m, D_MODEL), F32), jax.ShapeDtypeStruct((m, pw), U32)]
    out_specs = [pl.BlockSpec((tm, D_MODEL), row), pl.BlockSpec((tm, pw), row)]
    if route:
        in_specs.append(pl.BlockSpec((1, N_EXPERTS, D_MODEL), lambda i: (moe_layer, 0, 0)))
        args.append(w_router_t)
        out_shape.append(jax.ShapeDtypeStruct((m, ROUTE_LANES), F32))
        out_specs.append(pl.BlockSpec((tm, ROUTE_LANES), row))
    return pl.pallas_call(
        functools.partial(_out_proj_kernel, route=route),
        out_shape=tuple(out_shape),
        grid=(m // tm,),
        in_specs=in_specs,
        out_specs=tuple(out_specs),
        compiler_params=_cparams(("parallel",)),
        name="out_proj",
    )(*args)


def _tile_row(i, nu):
    return jnp.minimum(i, nu[0] - 1)


def _ffn_up_kernel(te_ref, nu_ref, x_ref, w1_ref, w3_ref, g_ref, w1b, w3b):
    i = pl.program_id(1)
    used = i < nu_ref[0]
    cur = te_ref[_tile_row(i, nu_ref)]
    prev = te_ref[_tile_row(jnp.maximum(i - 1, 0), nu_ref)]

    @pl.when(used & ((i == 0) | (cur != prev)))
    def _():
        w1b[...] = w1_ref[0, 0].astype(BF16)
        w3b[...] = w3_ref[0, 0].astype(BF16)

    @pl.when(used)
    def _():
        x = _unpack_rows(x_ref[...])
        for c in range(g_ref.shape[1] // IN_SUB):
            sl = slice(c * IN_SUB, (c + 1) * IN_SUB)
            a = jnp.dot(x, w1b[:, sl], preferred_element_type=F32)
            b = jnp.dot(x, w3b[:, sl], preferred_element_type=F32)
            g_ref[:, sl] = (a * jax.nn.sigmoid(a) * b).astype(g_ref.dtype)

    @pl.when(jnp.logical_not(used))
    def _():
        g_ref[...] = jnp.zeros(g_ref.shape, g_ref.dtype)


def ffn_up(tile_expert, n_used, xs, w1, w3, *, layer, tm, tf):
    r, xw = xs.shape
    assert r % tm == 0 and D_FF % tf == 0
    w_spec = pl.BlockSpec((1, 1, D_MODEL, tf),
                          lambda j, i, te, nu: (layer, te[_tile_row(i, nu)], 0, j))
    grid_spec = pltpu.PrefetchScalarGridSpec(
        num_scalar_prefetch=2,
        grid=(D_FF // tf, r // tm),
        in_specs=[pl.BlockSpec((tm, xw), lambda j, i, te, nu: (_tile_row(i, nu), 0)), w_spec, w_spec],
        out_specs=pl.BlockSpec((tm, tf), lambda j, i, te, nu: (i, j)),
        scratch_shapes=[pltpu.VMEM((D_MODEL, tf), BF16), pltpu.VMEM((D_MODEL, tf), BF16)])
    return pl.pallas_call(
        _ffn_up_kernel,
        out_shape=jax.ShapeDtypeStruct((r, D_FF), BF16),
        grid_spec=grid_spec,
        compiler_params=_cparams(("arbitrary", "arbitrary")),
        name="ffn_up",
    )(tile_expert, n_used, xs, w1, w3)


def _ffn_down_kernel(te_ref, nu_ref, g_ref, w2_ref, *rest, residual):
    i = pl.program_id(1)
    if residual:
        x_ref, y_ref = rest
    else:
        (y_ref,) = rest

    @pl.when(i < nu_ref[0])
    def _():
        y = jnp.dot(g_ref[...], w2_ref[0, 0], preferred_element_type=F32)
        y_ref[...] = x_ref[...] + y if residual else y

    @pl.when(i >= nu_ref[0])
    def _():
        y_ref[...] = jnp.zeros(y_ref.shape, y_ref.dtype)


def ffn_down(tile_expert, n_used, g, w2, x_res, *, layer, tm, tn):
    r = g.shape[0]
    assert r % tm == 0 and D_MODEL % tn == 0
    residual = x_res is not None
    in_specs = [pl.BlockSpec((tm, D_FF), lambda j, i, te, nu: (_tile_row(i, nu), 0)),
                pl.BlockSpec((1, 1, D_FF, tn), lambda j, i, te, nu: (layer, te[_tile_row(i, nu)], 0, j))]
    args = [tile_expert, n_used, g, w2]
    if residual:
        in_specs.append(pl.BlockSpec((tm, tn), lambda j, i, te, nu: (_tile_row(i, nu), j)))
        args.append(x_res)
    grid_spec = pltpu.PrefetchScalarGridSpec(
        num_scalar_prefetch=2,
        grid=(D_MODEL // tn, r // tm),
        in_specs=in_specs,
        out_specs=pl.BlockSpec((tm, tn), lambda j, i, te, nu: (i, j)))
    return pl.pallas_call(
        functools.partial(_ffn_down_kernel, residual=residual),
        out_shape=jax.ShapeDtypeStruct((r, D_MODEL), F32),
        grid_spec=grid_spec,
        compiler_params=_cparams(("arbitrary", "arbitrary")),
        name="ffn_down",
    )(*args)


DMA_UNROLL = 8
SCATTER_BATCH = 16


def _dispatch_kernel(dest_ref, nu_ref, hf_hbm, o_ref, row_tok, buf, sem, *, n_assign, tm):
    i = pl.program_id(0)
    slot = i % 2
    nu = nu_ref[0]

    @pl.when(i == 0)
    def _():
        def clear(r, carry):
            row_tok[r] = 0
            return carry

        def scatter(blk, carry):
            base = blk * SCATTER_BATCH
            ds = [dest_ref[base + k] for k in range(SCATTER_BATCH)]
            for k in range(SCATTER_BATCH):
                row_tok[ds[k]] = (base + k) // TOP_K
            return carry

        lax.fori_loop(0, row_tok.shape[0], clear, 0, unroll=DMA_UNROLL)
        lax.fori_loop(0, n_assign // SCATTER_BATCH, scatter, 0)

    def row_copy(tok, sl, r):
        return pltpu.make_async_copy(hf_hbm.at[pl.ds(tok, 1), :], buf.at[sl, pl.ds(r, 1), :], sem.at[sl])

    def start_tile(tile, sl):
        def issue(r, carry):
            row_copy(row_tok[tile * tm + r], sl, r).start()
            return carry

        lax.fori_loop(0, tm, issue, 0, unroll=DMA_UNROLL)

    @pl.when((i == 0) & (nu > 0))
    def _():
        start_tile(i, slot)

    @pl.when(i + 1 < nu)
    def _():
        start_tile(i + 1, 1 - slot)

    @pl.when(i < nu)
    def _():
        pltpu.make_async_copy(buf.at[slot], buf.at[slot], sem.at[slot]).wait()
        o_ref[...] = buf[slot]

    @pl.when(i >= nu)
    def _():
        o_ref[...] = jnp.zeros(o_ref.shape, o_ref.dtype)


def dispatch(dest, n_used, hf, *, n_tiles, tm):
    n_assign = dest.shape[0]
    assert n_assign % SCATTER_BATCH == 0
    width = hf.shape[1]
    grid_spec = pltpu.PrefetchScalarGridSpec(
        num_scalar_prefetch=2,
        grid=(n_tiles,),
        in_specs=[pl.BlockSpec(memory_space=pl.ANY)],
        out_specs=pl.BlockSpec((tm, width), lambda i, d, nu: (i, 0)),
        scratch_shapes=[pltpu.SMEM((n_tiles * tm,), jnp.int32),
                        pltpu.VMEM((2, tm, width), hf.dtype),
                        pltpu.SemaphoreType.DMA((2,))])
    return pl.pallas_call(
        functools.partial(_dispatch_kernel, n_assign=n_assign, tm=tm),
        out_shape=jax.ShapeDtypeStruct((n_tiles * tm, width), hf.dtype),
        grid_spec=grid_spec,
        compiler_params=_cparams(("arbitrary",)),
        name="moe_dispatch",
    )(dest, n_used, hf)


def _combine_kernel(dest_ref, x_ref, rt_ref, y_hbm, o_ref, buf, sem, *, tn):
    i = pl.program_id(0)
    n = pl.num_programs(0)
    slot = i % 2

    def row_copy(d, sl, k, rr):
        return pltpu.make_async_copy(y_hbm.at[pl.ds(d, 1), :], buf.at[sl, k, pl.ds(rr, 1), :],
                                     sem.at[sl])

    def start_tile(tile, sl):
        def issue(rr, carry):
            for k in range(TOP_K):
                row_copy(dest_ref[(tile * tn + rr) * TOP_K + k], sl, k, rr).start()
            return carry

        lax.fori_loop(0, tn, issue, 0, unroll=DMA_UNROLL)

    @pl.when(i == 0)
    def _():
        start_tile(i, slot)

    @pl.when(i + 1 < n)
    def _():
        start_tile(i + 1, 1 - slot)

    pltpu.make_async_copy(buf.at[slot], buf.at[slot], sem.at[slot]).wait()

    rt = rt_ref[...]
    o_ref[...] = (x_ref[...] + (rt[:, 2:3] * buf[slot, 0] + rt[:, 3:4] * buf[slot, 1]))


def combine(dest, x, route, y, *, tn):
    m = x.shape[0]
    assert m % tn == 0
    grid_spec = pltpu.PrefetchScalarGridSpec(
        num_scalar_prefetch=1,
        grid=(m // tn,),
        in_specs=[pl.BlockSpec((tn, D_MODEL), lambda i, d: (i, 0)),
                  pl.BlockSpec((tn, ROUTE_LANES), lambda i, d: (i, 0)),
                  pl.BlockSpec(memory_space=pl.ANY)],
        out_specs=pl.BlockSpec((tn, D_MODEL), lambda i, d: (i, 0)),
        scratch_shapes=[pltpu.VMEM((2, TOP_K, tn, D_MODEL), F32),
                        pltpu.SemaphoreType.DMA((2,))])
    return pl.pallas_call(
        functools.partial(_combine_kernel, tn=tn),
        out_shape=jax.ShapeDtypeStruct((m, D_MODEL), F32),
        grid_spec=grid_spec,
        compiler_params=_cparams(("arbitrary",)),
        name="moe_combine",
    )(dest, x, route, y)


def _norm_kernel(x_ref, g_ref, o_ref):
    o_ref[...] = _rms(x_ref[...], g_ref[...])


def final_norm(x, gain, *, tm):
    m = x.shape[0]
    assert m % tm == 0
    return pl.pallas_call(
        _norm_kernel,
        out_shape=jax.ShapeDtypeStruct((m, D_MODEL), F32),
        grid=(m // tm,),
        in_specs=[pl.BlockSpec((tm, D_MODEL), lambda i: (i, 0)),
                  pl.BlockSpec((1, D_MODEL), lambda i: (0, 0))],
        out_specs=pl.BlockSpec((tm, D_MODEL), lambda i: (i, 0)),
        compiler_params=_cparams(("parallel",)),
        name="final_norm",
    )(x, gain)


def _rot_half(w):
    half = QK_ROPE // 2
    return jnp.concatenate([-w[..., half:], w[..., :half]], axis=-1)


def _lane_pad(w):
    return jnp.pad(w, [(0, 0)] * (w.ndim - 1) + [(0, ROPE_LANES - w.shape[-1])])


def _prep_w_in(w_in):
    o1 = Q_LORA
    o2 = o1 + KV_LORA
    o3 = o2 + QK_ROPE
    o4 = o3 + W_SSM
    o5 = o4 + W_POOL
    kpe = w_in[..., o2:o3]
    parts = [w_in[..., o5:], w_in[..., :o1], w_in[..., o3:o4], w_in[..., o4:o5], w_in[..., o1:o2],
             _lane_pad(kpe), _lane_pad(_rot_half(kpe))]
    return jnp.concatenate(parts, axis=-1).astype(BF16)


def _prep_w_q(w_q_up):
    depth = w_q_up.shape[0]
    w = w_q_up.reshape(depth, Q_LORA, N_HEADS, QK_NOPE + QK_ROPE)
    nope = w[..., :QK_NOPE].reshape(depth, Q_LORA, N_HEADS * QK_NOPE)
    rope = w[..., QK_NOPE:]
    rope_p = _lane_pad(rope).reshape(depth, Q_LORA, N_HEADS * ROPE_LANES)
    rot_p = _lane_pad(_rot_half(rope)).reshape(depth, Q_LORA, N_HEADS * ROPE_LANES)
    return jnp.concatenate([nope, rope_p, rot_p], axis=-1).astype(BF16)


def _block_diag_heads(w):
    depth, _, r, c = w.shape
    ng = N_HEADS // HEAD_GROUP
    w5 = w.reshape(depth, ng, HEAD_GROUP, r, c)
    eye = jnp.eye(HEAD_GROUP, dtype=w.dtype)
    out = w5[:, :, :, :, None, :] * eye[None, None, :, None, :, None]
    return out.reshape(depth, ng, HEAD_GROUP * r, HEAD_GROUP * c)


def _rope_tables(pos):
    half = QK_ROPE // 2
    freqs = ROPE_BASE ** (-jnp.arange(half, dtype=F32) / half)
    ang = pos.astype(F32)[:, None] * freqs[None, :]
    cos, sin = jnp.cos(ang), jnp.sin(ang)
    return (_lane_pad(jnp.concatenate([cos, cos], axis=-1)),
            _lane_pad(jnp.concatenate([sin, sin], axis=-1)))


def _expand_tile_groups(small):
    eye = jnp.eye(SSM_TILE_GROUPS, dtype=small.dtype)
    return small[:, :, :, :, :, :, None, :] * eye[None, None, None, :, None, None, :, None]


def _prep_ssm(lam_re, lam_im, log_dt, b_re, b_im, c_re, c_im, steps_list):
    hi = lax.Precision.HIGHEST
    depth = lam_re.shape[0]
    g, p, n, tc = SSM_GROUPS, SSM_STATE, SSM_GROUP, SSM_TC
    lam = lax.complex(lam_re.astype(F32), lam_im.astype(F32))
    dt = jnp.exp(log_dt.astype(F32))[..., None]
    ldt = lam * dt
    pw = jnp.exp(ldt[..., None] * jnp.arange(tc + 1, dtype=F32))
    lam_bar = pw[..., 1]
    b_bar = ((lam_bar - 1.0) / lam)[..., None] * lax.complex(b_re.astype(F32), b_im.astype(F32))
    c_c = lax.complex(c_re.astype(F32), c_im.astype(F32))
    kt = jnp.real(jnp.einsum("dgmp,dgpt,dgpn->dgtmn", c_c, pw[..., :tc], b_bar, precision=hi))
    s_idx = jnp.arange(tc)[:, None]
    t_idx = jnp.arange(tc)[None, :]
    lag = jnp.clip(t_idx - s_idx, 0, tc - 1)
    toep = jnp.where((t_idx >= s_idx)[None, None, :, :, None, None], kt[:, :, lag], 0.0)
    toep = toep.reshape(depth, SSM_TILES, SSM_TILE_GROUPS, tc, tc, n, n)
    toep = _expand_tile_groups(toep.transpose(0, 1, 3, 2, 6, 4, 5))
    toep = toep.astype(BF16).reshape(depth, SSM_TILES, tc * LANES, tc * LANES)

    def wsum_for(steps):
        ex = jnp.clip(steps - 1 - jnp.arange(tc), 0, tc)
        v = pw[..., ex][..., None] * b_bar[:, :, :, None, :]
        v = jnp.where((jnp.arange(tc) < steps)[None, None, None, :, None], v, 0.0)
        v = v.transpose(0, 1, 3, 4, 2)
        w = jnp.stack([jnp.real(v), jnp.imag(v)], axis=-2)
        w = w.reshape(depth, SSM_TILES, SSM_TILE_GROUPS, tc, n, 2, p)
        w = _expand_tile_groups(w.transpose(0, 1, 3, 2, 4, 5, 6))
        return w.astype(BF16).reshape(depth, SSM_TILES, tc * LANES, SSM_COLS)

    def a_for(steps):
        a = pw[..., steps].reshape(depth, SSM_TILES, SSM_HALF)
        a = jnp.stack([jnp.real(a), jnp.imag(a)], axis=2)
        return jnp.pad(a, ((0, 0), (0, 0), (0, SUBLANES - 2), (0, 0)))

    ac = c_c.transpose(0, 1, 3, 2)[:, :, :, None, :] * pw[..., 1:tc + 1][..., None]
    wout = jnp.stack([jnp.real(ac), -jnp.imag(ac)], axis=2)
    wout = wout.reshape(depth, SSM_TILES, SSM_TILE_GROUPS, 2, p, tc, n)
    wout = _expand_tile_groups(wout.transpose(0, 1, 3, 2, 4, 5, 6))
    wout = wout.astype(BF16).reshape(depth, SSM_TILES, SSM_COLS, tc * LANES)
    out = dict(toep=toep, wout=wout)
    for steps in steps_list:
        out[("wsum", steps)] = wsum_for(steps)
        out[("a", steps)] = a_for(steps)
    return out


def _state_pack(h):
    nb = h.shape[0]
    h5 = h.astype(F32).reshape(nb, SSM_TILES, SSM_TILE_GROUPS, SSM_STATE, 2)
    return h5.transpose(0, 1, 4, 2, 3).reshape(nb, SSM_TILES * SSM_COLS)


def _state_unpack(hp, dtype):
    nb = hp.shape[0]
    h5 = hp.reshape(nb, SSM_TILES, 2, SSM_TILE_GROUPS, SSM_STATE).transpose(0, 1, 3, 4, 2)
    return h5.reshape(nb, SSM_GROUPS, SSM_STATE, 2).astype(dtype)


def _routing_plan(route, tm, n_tiles):
    flat_e = route[:, :TOP_K].astype(jnp.int32).reshape(-1)
    onehot = (flat_e[:, None] == jnp.arange(N_EXPERTS, dtype=jnp.int32)[None, :]).astype(jnp.int32)
    csum = jnp.cumsum(onehot, axis=0)
    rank = jnp.sum((csum - onehot) * onehot, axis=1)
    counts = csum[-1]
    padded = (counts + tm - 1) // tm * tm
    pad_end = jnp.cumsum(padded)
    pad_start = pad_end - padded
    dest = (jnp.sum(onehot * pad_start[None, :], axis=1) + rank).astype(jnp.int32)
    n_used = (pad_end[-1:] // tm).astype(jnp.int32)
    tile_start = jnp.arange(n_tiles, dtype=jnp.int32) * tm
    tile_expert = jnp.minimum(jnp.sum((tile_start[:, None] >= pad_end[None, :]).astype(jnp.int32), axis=1),
                              N_EXPERTS - 1).astype(jnp.int32)
    return dest, tile_expert, n_used


TM_BIG_OPTIONS = (1088, 544, 272, 136)
TM_MID_OPTIONS = (544, 272, 136)
TN_COMBINE_OPTIONS = (128, 64, 32, 16, 8)
TM_MOE = 512
TN_IN = 1024
TN_MERGE = 512
TF_UP = 512
TN_DOWN = 1024
TQ_ATTN = 128
TK_ATTN = 512
PAGES_PER_CHUNK = 32


def _largest_divisor(n, options):
    for o in options:
        if n % o == 0:
            return o
    raise ValueError(f"no tile in {options} divides {n}")


def kernel(x_prompt, x_sample, cache_ckv, cache_kpe, state_ssm, state_pool, page_table, attn_norm_g, w_in, q_norm_g, w_q_up, kv_norm_g, w_uk, w_uv, ssm_lam_re, ssm_lam_im, ssm_log_dt, ssm_b_re, ssm_b_im, ssm_c_re, ssm_c_im, ssm_d, ssm_w_glu, pool_w, pool_scale, w_branch, w_out, ffn_norm_g, ffn_w1, ffn_w3, ffn_w2, moe_router, moe_w1, moe_w3, moe_w2, final_norm_g):
    bsz, seq, _ = x_prompt.shape
    db, t, _ = x_sample.shape
    depth = w_in.shape[0]
    past = page_table.shape[1] * PAGE_SIZE
    m_p = bsz * seq
    m_s = db * t
    m = m_p + m_s
    assert seq % SSM_TC == 0 and t <= SSM_TC and m_p % m_s == 0

    w_in2 = _prep_w_in(w_in)
    wq2 = _prep_w_q(w_q_up)
    wukbd = _block_diag_heads(w_uk.transpose(0, 2, 3, 1)).astype(BF16)
    wuvbd = _block_diag_heads(w_uv.transpose(0, 2, 1, 3)).astype(BF16)
    wuv_flat = w_uv.reshape(depth, KV_LORA, W_ATTN).astype(BF16)
    ssm_w = _prep_ssm(ssm_lam_re, ssm_lam_im, ssm_log_dt, ssm_b_re, ssm_b_im, ssm_c_re, ssm_c_im,
                      (SSM_TC, t))
    w_glu = ssm_w_glu.astype(BF16)
    pool_wb = pool_w.astype(BF16)
    wb = w_branch.astype(BF16)
    wo = w_out.astype(BF16)
    fw1, fw3 = ffn_w1[:, None], ffn_w3[:, None]
    fw2 = ffn_w2.astype(BF16)[:, None]
    mw2 = moe_w2.astype(BF16)
    w_router_t = moe_router.astype(F32).transpose(0, 2, 1)
    cache_kpe_t = cache_kpe.transpose(0, 1, 3, 2)

    pos = jnp.concatenate([jnp.tile(jnp.arange(seq), bsz), past + jnp.tile(jnp.arange(t), db)])
    cos_t, sin_t = _rope_tables(pos)
    pos_s = past + jnp.arange(t)
    cnt_s = jnp.concatenate(
        [jnp.broadcast_to(jnp.minimum(pos_s + 1, w).astype(F32)[:, None, None], (t, 1, POOL_GW))
         for w in POOL_WINDOWS], axis=-1)

    tm_big = _largest_divisor(m, TM_BIG_OPTIONS)
    tm_mid = _largest_divisor(m, TM_MID_OPTIONS)
    tn_combine = _largest_divisor(m, TN_COMBINE_OPTIONS)
    tk_attn = min(TK_ATTN, seq)
    pages_per_chunk = min(PAGES_PER_CHUNK, page_table.shape[1])
    def dense_tiles(tm):
        return jnp.zeros((m // tm,), jnp.int32), jnp.full((1,), m // tm, jnp.int32)

    te_up, nu_up = dense_tiles(tm_big)
    te_down, nu_down = dense_tiles(tm_mid)
    n_moe_tiles = (m * TOP_K) // TM_MOE + N_EXPERTS

    x = jnp.concatenate([x_prompt.reshape(m_p, D_MODEL), x_sample.reshape(m_s, D_MODEL)], axis=0)
    outs = [[] for _ in range(8)]
    for l in range(depth):
        z = in_proj(x, attn_norm_g[l][None], w_in2, layer=l, tm=tm_big, tn=TN_IN)
        qcat, c_lat, k_pe, kcat = qkv_post(z, cos_t, sin_t, q_norm_g[l][None], kv_norm_g[l][None],
                                           wq2, wukbd, layer=l, tm=tm_mid)
        oa_p = prompt_attention(qcat, kcat, wuvbd, layer=l, batch=bsz, seq=seq, tq=TQ_ATTN, tk=tk_attn)
        q3 = qcat[m_p:].reshape(db, t * N_HEADS, QCAT)
        knew3 = jnp.pad(kcat[m_p:].reshape(db, t, QCAT), ((0, 0), (0, NEW_KEYS - t), (0, 0)))
        oa_s = sample_attention(page_table, q3, knew3, cache_ckv, cache_kpe_t, wuv_flat,
                                layer=l, pc=pages_per_chunk)
        o_attn = jnp.concatenate([oa_p, oa_s.reshape(m_s, W_ATTN)], axis=0)
        y_p, hfin_p = ssm_scan(z, ssm_w["toep"], ssm_w[("wsum", SSM_TC)], ssm_w["wout"], ssm_w[("a", SSM_TC)],
                               jnp.zeros((bsz, SSM_TILES * SSM_COLS), F32),
                               layer=l, row_block=0, nseq=bsz, nk=seq // SSM_TC, steps=SSM_TC)
        y_s, hfin_s = ssm_scan(z, ssm_w["toep"], ssm_w[("wsum", t)], ssm_w["wout"], ssm_w[("a", t)],
                               _state_pack(state_ssm[l]),
                               layer=l, row_block=m_p // m_s, nseq=db, nk=1, steps=t)
        y_ssm = ssm_post(jnp.concatenate([y_p, y_s], axis=0), z, ssm_d[l][None], w_glu[l], tm=tm_mid)
        yp_p = pool_prompt(z, pool_wb[l], pool_scale[l][None], batch=bsz, seq=seq)
        u_pool = z[:, Z_POOL:Z_POOL + W_POOL]
        u_ext = jnp.concatenate([state_pool[l].astype(F32), u_pool[m_p:].reshape(db, t, W_POOL)], axis=1)
        yp_s = pool_sample(u_ext.transpose(1, 0, 2), cnt_s, pool_wb[l], pool_scale[l][None])
        y_pool = jnp.concatenate([yp_p, yp_s.transpose(1, 0, 2).reshape(m_s, W_POOL)], axis=0)
        mm = branch_merge(o_attn, y_ssm, y_pool, z, wb, layer=l, tm=tm_big, tn=TN_MERGE)
        j = l // 2
        if l % 2 == 0:
            xn, hfp = out_proj(mm, wo, x, ffn_norm_g[l][None], None, layer=l, moe_layer=None, tm=tm_mid)
            g_act = ffn_up(te_up, nu_up, hfp, fw1, fw3, layer=j, tm=tm_big, tf=TF_UP)
            x = ffn_down(te_down, nu_down, g_act, fw2, xn, layer=j, tm=tm_mid, tn=TN_DOWN)
        else:
            xn, hfp, route = out_proj(mm, wo, x, ffn_norm_g[l][None], w_router_t, layer=l, moe_layer=j,
                                      tm=tm_mid)
            dest, tile_expert, n_used = _routing_plan(route, TM_MOE, n_moe_tiles)
            xs = dispatch(dest, n_used, hfp, n_tiles=n_moe_tiles, tm=TM_MOE)
            g_act = ffn_up(tile_expert, n_used, xs, moe_w1, moe_w3, layer=j, tm=TM_MOE, tf=TF_UP)
            y_sorted = ffn_down(tile_expert, n_used, g_act, mw2, None, layer=j, tm=TM_MOE, tn=TN_DOWN)
            x = combine(dest, xn, route, y_sorted, tn=tn_combine)
        outs[0].append(c_lat[:m_p].reshape(bsz, seq, KV_LORA))
        outs[1].append(k_pe[:m_p].reshape(bsz, seq, QK_ROPE))
        outs[2].append(_state_unpack(hfin_p, state_ssm.dtype))
        outs[3].append(u_pool[:m_p].reshape(bsz, seq, W_POOL)[:, seq - POOL_PAD:])
        outs[4].append(c_lat[m_p:].reshape(db, t, KV_LORA))
        outs[5].append(k_pe[m_p:].reshape(db, t, QK_ROPE))
        outs[6].append(_state_unpack(hfin_s, state_ssm.dtype))
        outs[7].append(u_ext[:, t:])
    y = final_norm(x, final_norm_g[None], tm=tm_mid)
    return (y[:m_p].reshape(bsz, seq, D_MODEL), y[m_p:].reshape(db, t, D_MODEL),
            *[jnp.stack(o) for o in outs])
```

```python
import functools

import jax
import jax.numpy as jnp
import numpy as np
from jax import lax
from jax.experimental import pallas as pl
from jax.experimental.pallas import tpu as pltpu

F32 = jnp.float32
BF16 = jnp.bfloat16
U32 = jnp.uint32

D_MODEL = 2048
N_HEADS = 16
QK_NOPE = 64
QK_ROPE = 32
V_HEAD = 64
W_ATTN = N_HEADS * V_HEAD
Q_LORA = 512
KV_LORA = 256
ROPE_BASE = 10000.0
SM_SCALE = (QK_NOPE + QK_ROPE) ** -0.5
PAGE_SIZE = 128
W_SSM = 512
SSM_GROUP = 16
SSM_GROUPS = W_SSM // SSM_GROUP
SSM_STATE = 64
W_POOL = 512
POOL_WINDOWS = (2, 4, 8, 16)
POOL_GW = W_POOL // len(POOL_WINDOWS)
POOL_PAD = max(POOL_WINDOWS) - 1
N_BRANCH = 3
D_FF = 5632
N_EXPERTS = 8
TOP_K = 2
EPS = 1e-6

LANES = 128
SUBLANES = 8
VMEM_LIMIT_BYTES = 56 * 1024 * 1024

ROPE_LANES = LANES
Z_GATE = 0
Z_QC = Z_GATE + N_BRANCH * D_MODEL
Z_SSM = Z_QC + Q_LORA
Z_POOL = Z_SSM + W_SSM
Z_KV = Z_POOL + W_POOL
Z_KPE = Z_KV + KV_LORA
Z_COLS = Z_KPE + 2 * ROPE_LANES
QCAT = KV_LORA + ROPE_LANES
HEAD_GROUP = 4


def _cparams(sem, vmem=VMEM_LIMIT_BYTES):
    return pltpu.CompilerParams(dimension_semantics=sem, vmem_limit_bytes=vmem)


def _lane_tile(x, n):
    return x if n == 1 else jnp.concatenate([x] * n, axis=1)


def _rms(x, g):
    ms = jnp.mean(x * x, axis=-1, keepdims=True)
    return x * lax.rsqrt(ms + EPS) * g


def _packed_width(d):
    return d // 2


def _pack_rows(h):
    c = h.shape[1] // 2
    bits = pltpu.bitcast(h.astype(BF16).astype(F32), U32)
    return (bits[:, :c] >> 16) | (bits[:, c:] & jnp.uint32(0xFFFF0000))


def _unpack_rows(p):
    lo = pltpu.bitcast(p << 16, F32).astype(BF16)
    hi = pltpu.bitcast(p & jnp.uint32(0xFFFF0000), F32).astype(BF16)
    return jnp.concatenate([lo, hi], axis=1)


IN_SUB = 256


def _in_proj_kernel(x_ref, g_ref, w_ref, z_ref, h_ref, *, n_gate_tiles):
    j = pl.program_id(1)

    @pl.when(j == 0)
    def _():
        h_ref[...] = _rms(x_ref[...], g_ref[...]).astype(BF16)

    h = h_ref[...]
    is_gate = j < n_gate_tiles
    for c in range(z_ref.shape[1] // IN_SUB):
        sl = slice(c * IN_SUB, (c + 1) * IN_SUB)
        acc = jnp.dot(h, w_ref[0, :, sl], preferred_element_type=F32)
        z_ref[:, sl] = jnp.where(is_gate, jax.nn.sigmoid(acc), acc)


def in_proj(x, gain, w, *, layer, tm, tn):
    m, d = x.shape
    n = w.shape[2]
    assert m % tm == 0 and n % tn == 0 and (N_BRANCH * D_MODEL) % tn == 0 and tn % IN_SUB == 0
    return pl.pallas_call(
        functools.partial(_in_proj_kernel, n_gate_tiles=(N_BRANCH * D_MODEL) // tn),
        out_shape=jax.ShapeDtypeStruct((m, n), F32),
        grid=(m // tm, n // tn),
        in_specs=[pl.BlockSpec((tm, d), lambda i, j: (i, 0)),
                  pl.BlockSpec((1, d), lambda i, j: (0, 0)),
                  pl.BlockSpec((1, d, tn), lambda i, j: (layer, 0, j))],
        out_specs=pl.BlockSpec((tm, tn), lambda i, j: (i, j)),
        scratch_shapes=[pltpu.VMEM((tm, d), BF16)],
        compiler_params=_cparams(("parallel", "arbitrary")),
        name="in_proj",
    )(x, gain, w)


def _qkv_kernel(qc_ref, kv_ref, kpe_ref, cos_ref, sin_ref, qg_ref, kvg_ref, wq_ref, wuk_ref,
                qcat_ref, c_ref, kpe_out_ref, kcat_ref):
    nope_w = N_HEADS * QK_NOPE
    rope_w = N_HEADS * ROPE_LANES
    cos = cos_ref[...]
    sin = sin_ref[...]
    qn = _rms(qc_ref[...], qg_ref[...]).astype(BF16)
    q = jnp.dot(qn, wq_ref[0], preferred_element_type=F32)
    cos_h = jnp.concatenate([cos] * N_HEADS, axis=1)
    sin_h = jnp.concatenate([sin] * N_HEADS, axis=1)
    q_pe = (q[:, nope_w:nope_w + rope_w] * cos_h
            + q[:, nope_w + rope_w:nope_w + 2 * rope_w] * sin_h).astype(BF16)
    q_nope = q[:, :nope_w].astype(BF16)
    gw = HEAD_GROUP * QK_NOPE
    for g in range(N_HEADS // HEAD_GROUP):
        qa = jnp.dot(q_nope[:, g * gw:(g + 1) * gw], wuk_ref[0, g],
                     preferred_element_type=F32).astype(BF16)
        for hl in range(HEAD_GROUP):
            h = g * HEAD_GROUP + hl
            qcat_ref[:, h * QCAT:h * QCAT + KV_LORA] = qa[:, hl * KV_LORA:(hl + 1) * KV_LORA]
            qcat_ref[:, h * QCAT + KV_LORA:(h + 1) * QCAT] = q_pe[:, h * ROPE_LANES:(h + 1) * ROPE_LANES]
    c = _rms(kv_ref[...], kvg_ref[...])
    c_ref[...] = c
    kraw = kpe_ref[...]
    kr = kraw[:, :ROPE_LANES] * cos + kraw[:, ROPE_LANES:] * sin
    kpe_out_ref[...] = kr[:, :QK_ROPE]
    kcat_ref[:, :KV_LORA] = c.astype(BF16)
    kcat_ref[:, KV_LORA:] = kr.astype(BF16)


def qkv_post(z, cos_t, sin_t, q_g, kv_g, wq, wuk, *, layer, tm):
    m = z.shape[0]
    assert m % tm == 0
    nq = wq.shape[2]
    return pl.pallas_call(
        _qkv_kernel,
        out_shape=(jax.ShapeDtypeStruct((m, N_HEADS * QCAT), BF16),
                   jax.ShapeDtypeStruct((m, KV_LORA), F32),
                   jax.ShapeDtypeStruct((m, QK_ROPE), F32),
                   jax.ShapeDtypeStruct((m, QCAT), BF16)),
        grid=(m // tm,),
        in_specs=[pl.BlockSpec((tm, Q_LORA), lambda i: (i, Z_QC // Q_LORA)),
                  pl.BlockSpec((tm, KV_LORA), lambda i: (i, Z_KV // KV_LORA)),
                  pl.BlockSpec((tm, 2 * ROPE_LANES), lambda i: (i, Z_KPE // (2 * ROPE_LANES))),
                  pl.BlockSpec((tm, ROPE_LANES), lambda i: (i, 0)),
                  pl.BlockSpec((tm, ROPE_LANES), lambda i: (i, 0)),
                  pl.BlockSpec((1, Q_LORA), lambda i: (0, 0)),
                  pl.BlockSpec((1, KV_LORA), lambda i: (0, 0)),
                  pl.BlockSpec((1, Q_LORA, nq), lambda i: (layer, 0, 0)),
                  pl.BlockSpec((1,) + wuk.shape[1:], lambda i: (layer, 0, 0, 0))],
        out_specs=(pl.BlockSpec((tm, N_HEADS * QCAT), lambda i: (i, 0)),
                   pl.BlockSpec((tm, KV_LORA), lambda i: (i, 0)),
                   pl.BlockSpec((tm, QK_ROPE), lambda i: (i, 0)),
                   pl.BlockSpec((tm, QCAT), lambda i: (i, 0))),
        compiler_params=_cparams(("parallel",)),
        name="qkv_post",
    )(z, z, z, cos_t, sin_t, q_g, kv_g, wq, wuk)


def _attn_out_proj(o, wuv_ref, rows):
    ob = o.astype(BF16)
    outs = []
    for g in range(N_HEADS // HEAD_GROUP):
        wide = jnp.concatenate(
            [ob[(g * HEAD_GROUP + hl) * rows:(g * HEAD_GROUP + hl + 1) * rows, :]
             for hl in range(HEAD_GROUP)], axis=1)
        outs.append(jnp.dot(wide, wuv_ref[0, g], preferred_element_type=F32))
    return jnp.concatenate(outs, axis=1)


ATTN_ROW_GROUPS = 1


def _prompt_attn_kernel(q_ref, k_ref, wuv_ref, o_ref, m_ref, l_ref, acc_ref, *, tq, tk):
    i = pl.program_id(1)
    rows = N_HEADS * tq
    q = jnp.concatenate([q_ref[:, h * QCAT:(h + 1) * QCAT] for h in range(N_HEADS)], axis=0)
    m_ref[...] = jnp.full(m_ref.shape, -jnp.inf, F32)
    l_ref[...] = jnp.zeros(l_ref.shape, F32)
    acc_ref[...] = jnp.zeros(acc_ref.shape, F32)

    gr = rows // ATTN_ROW_GROUPS

    def chunk(j, masked):
        start = pl.multiple_of(j * tk, tk)
        k = k_ref[pl.ds(start, tk), :]
        if masked:
            qpos = i * tq + (lax.broadcasted_iota(jnp.int32, (gr, tk), 0) & (tq - 1))
            kpos = start + lax.broadcasted_iota(jnp.int32, (gr, tk), 1)
            visible = kpos <= qpos
        for g in range(ATTN_ROW_GROUPS):
            rs = slice(g * gr, (g + 1) * gr)
            s = lax.dot_general(q[rs], k, (((1,), (1,)), ((), ())), preferred_element_type=F32) * SM_SCALE
            if masked:
                s = jnp.where(visible, s, -jnp.inf)
            m_old = m_ref[rs, :]
            m_new = jnp.maximum(m_old, jnp.max(s, axis=1, keepdims=True))
            alpha = jnp.exp(m_old - m_new)
            p = jnp.exp(s - _lane_tile(m_new, tk // LANES))
            l_ref[rs, :] = alpha * l_ref[rs, :] + jnp.sum(p, axis=1, keepdims=True)
            acc_ref[rs, :] = (_lane_tile(alpha, KV_LORA // LANES) * acc_ref[rs, :]
                              + jnp.dot(p.astype(BF16), k[:, :KV_LORA], preferred_element_type=F32))
            m_ref[rs, :] = m_new

    n_full = (i * tq) // tk

    def full_body(j, carry):
        chunk(j, False)
        return carry

    lax.fori_loop(0, n_full, full_body, 0)
    chunk(n_full, True)
    o = acc_ref[...] / _lane_tile(l_ref[...], KV_LORA // LANES)
    o_ref[...] = _attn_out_proj(o, wuv_ref, tq).astype(o_ref.dtype)


def prompt_attention(qcat, kcat, wuv, *, layer, batch, seq, tq, tk):
    assert seq % tq == 0 and seq % tk == 0 and tk % tq == 0 and (tq & (tq - 1)) == 0
    nq = seq // tq
    rows = N_HEADS * tq
    return pl.pallas_call(
        functools.partial(_prompt_attn_kernel, tq=tq, tk=tk),
        out_shape=jax.ShapeDtypeStruct((batch * seq, W_ATTN), BF16),
        grid=(batch, nq),
        in_specs=[pl.BlockSpec((tq, N_HEADS * QCAT), lambda b, i: (b * nq + i, 0)),
                  pl.BlockSpec((seq, QCAT), lambda b, i: (b, 0)),
                  pl.BlockSpec((1,) + wuv.shape[1:], lambda b, i: (layer, 0, 0, 0))],
        out_specs=pl.BlockSpec((tq, W_ATTN), lambda b, i: (b * nq + i, 0)),
        scratch_shapes=[pltpu.VMEM((rows, LANES), F32), pltpu.VMEM((rows, LANES), F32),
                        pltpu.VMEM((rows, KV_LORA), F32)],
        compiler_params=_cparams(("parallel", "arbitrary")),
        name="prompt_attn",
    )(qcat, kcat, wuv)


NEW_KEYS = LANES
SAMPLE_KEY_BLOCKS = 1


def _sample_attn_kernel(pt_ref, q_ref, knew_ref, ckv_hbm, kpe_hbm, wuv_ref, o_ref,
                        ckv_buf, kpe_buf, sem, m_ref, l_ref, acc_ref,
                        *, layer, n_chunks, pc, t):
    b = pl.program_id(0)
    c = pl.program_id(1)
    total = pl.num_programs(0) * n_chunks
    step = b * n_chunks + c
    slot = step % 2
    rows = t * N_HEADS

    def copies(bb, cc, sl):
        out = []
        for p in range(pc):
            page = pt_ref[bb, cc * pc + p]
            out.append(pltpu.make_async_copy(
                ckv_hbm.at[layer, page], ckv_buf.at[sl, pl.ds(p * PAGE_SIZE, PAGE_SIZE), :],
                sem.at[sl, 0]))
            out.append(pltpu.make_async_copy(kpe_hbm.at[layer, page], kpe_buf.at[sl, p], sem.at[sl, 1]))
        return out

    @pl.when(step == 0)
    def _():
        for cp in copies(b, c, slot):
            cp.start()

    @pl.when(step + 1 < total)
    def _():
        last = c + 1 == n_chunks
        nb = jnp.where(last, b + 1, b)
        nc = jnp.where(last, 0, c + 1)
        for cp in copies(nb, nc, 1 - slot):
            cp.start()

    @pl.when(c == 0)
    def _():
        m_ref[...] = jnp.full(m_ref.shape, -jnp.inf, F32)
        l_ref[...] = jnp.zeros(l_ref.shape, F32)
        acc_ref[...] = jnp.zeros(acc_ref.shape, F32)

    pltpu.make_async_copy(ckv_buf.at[slot], ckv_buf.at[slot], sem.at[slot, 0]).wait()
    pltpu.make_async_copy(kpe_buf.at[slot], kpe_buf.at[slot], sem.at[slot, 1]).wait()

    q = q_ref[0]
    nt = (((1,), (1,)), ((), ()))

    def partial_softmax(s, v):
        m_blk = jnp.max(s, axis=1, keepdims=True)
        p = jnp.exp(s - m_blk)
        return m_blk, jnp.sum(p, axis=1, keepdims=True), jnp.dot(p.astype(BF16), v, preferred_element_type=F32)

    def merge(parts):
        m_old = m_ref[...]
        m_new = m_old
        for m_blk, _, _ in parts:
            m_new = jnp.maximum(m_new, m_blk)
        alpha = jnp.exp(m_old - m_new)
        l_new = alpha * l_ref[...]
        acc = _lane_tile(alpha, KV_LORA // LANES) * acc_ref[...]
        for m_blk, l_blk, o_blk in parts:
            w = jnp.exp(m_blk - m_new)
            l_new = l_new + w * l_blk
            acc = acc + _lane_tile(w, KV_LORA // LANES) * o_blk
        m_ref[...] = m_new
        l_ref[...] = l_new
        acc_ref[...] = acc

    sub_pages = pc // SAMPLE_KEY_BLOCKS
    sub_keys = sub_pages * PAGE_SIZE
    parts = []
    for sb in range(SAMPLE_KEY_BLOCKS):
        ck = ckv_buf[slot, sb * sub_keys:(sb + 1) * sub_keys, :].astype(BF16)
        kr_t = jnp.concatenate([kpe_buf[slot, sb * sub_pages + p] for p in range(sub_pages)], axis=1)
        kr_t = kr_t.astype(BF16)
        s = (lax.dot_general(q[:, :KV_LORA], ck, nt, preferred_element_type=F32)
             + jnp.dot(q[:, KV_LORA:KV_LORA + QK_ROPE], kr_t, preferred_element_type=F32)) * SM_SCALE
        parts.append(partial_softmax(s, ck))
    merge(parts)

    @pl.when(c == n_chunks - 1)
    def _():
        kn = knew_ref[0]
        sn = lax.dot_general(q, kn, nt, preferred_element_type=F32) * SM_SCALE
        qi = lax.broadcasted_iota(jnp.int32, (rows, NEW_KEYS), 0) // N_HEADS
        ki = lax.broadcasted_iota(jnp.int32, (rows, NEW_KEYS), 1)
        sn = jnp.where(ki <= qi, sn, -jnp.inf)
        merge([partial_softmax(sn, kn[:, :KV_LORA])])
        o = (acc_ref[...] / _lane_tile(l_ref[...], KV_LORA // LANES)).astype(BF16)
        full = jnp.dot(o, wuv_ref[0], preferred_element_type=F32)
        head = lax.broadcasted_iota(jnp.int32, (rows, W_ATTN), 0) % N_HEADS
        col_head = lax.broadcasted_iota(jnp.int32, (rows, W_ATTN), 1) // V_HEAD
        own = jnp.where(head == col_head, full, 0.0)
        o_ref[0] = jnp.concatenate(
            [jnp.sum(own[tt * N_HEADS:(tt + 1) * N_HEADS, :], axis=0, keepdims=True) for tt in range(t)],
            axis=0).astype(o_ref.dtype)


def sample_attention(page_table, q3, knew3, cache_ckv, cache_kpe_t, wuv_flat, *, layer, pc):
    db, n_pages = page_table.shape
    rows = q3.shape[1]
    t = rows // N_HEADS
    assert n_pages % pc == 0 and pc % SAMPLE_KEY_BLOCKS == 0 and t <= NEW_KEYS
    n_chunks = n_pages // pc
    n_keys = pc * PAGE_SIZE
    grid_spec = pltpu.PrefetchScalarGridSpec(
        num_scalar_prefetch=1,
        grid=(db, n_chunks),
        in_specs=[pl.BlockSpec((1, rows, QCAT), lambda b, c, pt: (b, 0, 0)),
                  pl.BlockSpec((1, NEW_KEYS, QCAT), lambda b, c, pt: (b, 0, 0)),
                  pl.BlockSpec(memory_space=pl.ANY),
                  pl.BlockSpec(memory_space=pl.ANY),
                  pl.BlockSpec((1, KV_LORA, W_ATTN), lambda b, c, pt: (layer, 0, 0))],
        out_specs=pl.BlockSpec((1, t, W_ATTN), lambda b, c, pt: (b, 0, 0)),
        scratch_shapes=[pltpu.VMEM((2, n_keys, KV_LORA), F32),
                        pltpu.VMEM((2, pc, QK_ROPE, PAGE_SIZE), F32),
                        pltpu.SemaphoreType.DMA((2, 2)),
                        pltpu.VMEM((rows, LANES), F32),
                        pltpu.VMEM((rows, LANES), F32),
                        pltpu.VMEM((rows, KV_LORA), F32)])
    return pl.pallas_call(
        functools.partial(_sample_attn_kernel, layer=layer, n_chunks=n_chunks, pc=pc, t=t),
        out_shape=jax.ShapeDtypeStruct((db, t, W_ATTN), BF16),
        grid_spec=grid_spec,
        compiler_params=_cparams(("arbitrary", "arbitrary")),
        name="sample_attn",
    )(page_table, q3, knew3, cache_ckv, cache_kpe_t, wuv_flat)


SSM_TC = 8
SSM_TILE_GROUPS = LANES // SSM_GROUP
SSM_TILES = W_SSM // LANES
SSM_HALF = SSM_TILE_GROUPS * SSM_STATE
SSM_COLS = 2 * SSM_HALF


def _ssm_expand(small, e_ref, row_div, col_div):
    full = jnp.dot(small, e_ref[...], preferred_element_type=F32)
    rg = (lax.broadcasted_iota(jnp.int32, full.shape, 0) // row_div) % SSM_TILE_GROUPS
    cg = (lax.broadcasted_iota(jnp.int32, full.shape, 1) // col_div) % SSM_TILE_GROUPS
    return jnp.where(rg == cg, full, 0.0).astype(BF16)


def _ssm_kernel(u_ref, toep_ref, wsum_ref, wout_ref, etm_ref, erp_ref, a_ref, h0_ref, y_ref, hfin_ref,
                s_scr, hp_scr, *, nseq, nk, steps):
    r = nseq * nk
    kw = steps * LANES
    u = jnp.concatenate([u_ref[pl.ds(s, r, stride=steps), :].astype(BF16) for s in range(steps)],
                        axis=1)
    toep = _ssm_expand(toep_ref[0, 0, :kw, :], etm_ref, SSM_GROUP, SSM_GROUP)
    wsum = _ssm_expand(wsum_ref[0, 0, :kw, :], erp_ref, SSM_GROUP, SSM_STATE)
    wout = _ssm_expand(wout_ref[0, 0], etm_ref, SSM_STATE, SSM_GROUP)
    sk_all = jnp.dot(u, wsum, preferred_element_type=F32)
    a = a_ref[0, 0]
    a_re = a[0:1, :]
    a_im = a[1:2, :]
    h0 = h0_ref[...]
    n_slab = SSM_COLS // LANES
    if nk == 1:
        hp = h0
        h_re, h_im = h0[:, :SSM_HALF], h0[:, SSM_HALF:]
        hfin_ref[...] = jnp.concatenate(
            [a_re * h_re - a_im * h_im + sk_all[:, :SSM_HALF],
             a_re * h_im + a_im * h_re + sk_all[:, SSM_HALF:]], axis=1)
    else:
        for j in range(n_slab):
            s_scr[j] = sk_all[:, j * LANES:(j + 1) * LANES]
        a_re_b = jnp.broadcast_to(a_re, (nseq, SSM_HALF))
        a_im_b = jnp.broadcast_to(a_im, (nseq, SSM_HALF))

        def body(k, h):
            for j in range(n_slab):
                hp_scr[j, pl.ds(k, nseq, stride=nk), :] = h[:, j * LANES:(j + 1) * LANES]
            sk = jnp.concatenate([s_scr[j, pl.ds(k, nseq, stride=nk), :] for j in range(n_slab)], axis=1)
            h_re, h_im = h[:, :SSM_HALF], h[:, SSM_HALF:]
            return jnp.concatenate(
                [a_re_b * h_re - a_im_b * h_im + sk[:, :SSM_HALF],
                 a_re_b * h_im + a_im_b * h_re + sk[:, SSM_HALF:]], axis=1)

        hfin_ref[...] = lax.fori_loop(0, nk, body, h0)
        hp = jnp.concatenate([hp_scr[j] for j in range(n_slab)], axis=1)
    y = (jnp.dot(u, toep, preferred_element_type=F32)
         + jnp.dot(hp.astype(BF16), wout, preferred_element_type=F32))
    for tt in range(steps):
        y_ref[pl.ds(tt, r, stride=steps), :] = y[:, tt * LANES:(tt + 1) * LANES]


def ssm_scan(z, toep, wsum, wout, e_tm, e_rp, a, h0, *, layer, row_block, nseq, nk, steps):
    rows = nseq * nk * steps
    r = nseq * nk
    w_spec = pl.BlockSpec((1, 1, SSM_COLS, LANES), lambda j: (layer, j, 0, 0))
    e_spec = pl.BlockSpec((LANES, SSM_COLS), lambda j: (0, 0))
    return pl.pallas_call(
        functools.partial(_ssm_kernel, nseq=nseq, nk=nk, steps=steps),
        out_shape=(jax.ShapeDtypeStruct((rows, W_SSM), F32),
                   jax.ShapeDtypeStruct((nseq, SSM_TILES * SSM_COLS), F32)),
        grid=(SSM_TILES,),
        in_specs=[pl.BlockSpec((rows, LANES), lambda j: (row_block, Z_SSM // LANES + j)),
                  w_spec, w_spec, w_spec, e_spec, e_spec,
                  pl.BlockSpec((1, 1, SUBLANES, SSM_HALF), lambda j: (layer, j, 0, 0)),
                  pl.BlockSpec((nseq, SSM_COLS), lambda j: (0, j))],
        out_specs=(pl.BlockSpec((rows, LANES), lambda j: (0, j)),
                   pl.BlockSpec((nseq, SSM_COLS), lambda j: (0, j))),
        scratch_shapes=[pltpu.VMEM((SSM_COLS // LANES, r, LANES), F32),
                        pltpu.VMEM((SSM_COLS // LANES, r, LANES), F32)],
        compiler_params=_cparams(("parallel",)),
        name="ssm_scan",
    )(z, toep, wsum, wout, e_tm, e_rp, a, h0)


def _gelu_tanh(x):
    return 0.5 * x * (1.0 + jnp.tanh(np.sqrt(2.0 / np.pi).astype(np.float32)
                                     * (x + np.float32(0.044715) * (x * x * x))))


def _ssm_post_kernel(y_ref, u_ref, d_ref, wg_ref, o_ref):
    y = y_ref[...] + d_ref[...] * u_ref[...]
    zz = _gelu_tanh(y)
    gate = jax.nn.sigmoid(jnp.dot(zz.astype(BF16), wg_ref[...], preferred_element_type=F32))
    o_ref[...] = (zz * gate).astype(o_ref.dtype)


def ssm_post(y_raw, z, d_skip, w_glu, *, tm):
    m = y_raw.shape[0]
    assert m % tm == 0
    return pl.pallas_call(
        _ssm_post_kernel,
        out_shape=jax.ShapeDtypeStruct((m, W_SSM), BF16),
        grid=(m // tm,),
        in_specs=[pl.BlockSpec((tm, W_SSM), lambda i: (i, 0)),
                  pl.BlockSpec((tm, W_SSM), lambda i: (i, Z_SSM // W_SSM)),
                  pl.BlockSpec((1, W_SSM), lambda i: (0, 0)),
                  pl.BlockSpec((W_SSM, W_SSM), lambda i: (0, 0))],
        out_specs=pl.BlockSpec((tm, W_SSM), lambda i: (i, 0)),
        compiler_params=_cparams(("parallel",)),
        name="ssm_post",
    )(y_raw, z, d_skip, w_glu)


POOL_HALO = 2 * SUBLANES


def _pool_mix(diff, pw_ref, ps_ref):
    outs = [jnp.dot(diff[:, g * POOL_GW:(g + 1) * POOL_GW].astype(BF16), pw_ref[g],
                    preferred_element_type=F32) for g in range(len(POOL_WINDOWS))]
    return jnp.concatenate(outs, axis=1) * ps_ref[...]


def _pool_prompt_kernel(u_ref, pw_ref, ps_ref, o_ref, ext):
    seq = u_ref.shape[0]
    ext[0:POOL_HALO, :] = jnp.zeros((POOL_HALO, W_POOL), F32)
    ext[POOL_HALO:, :] = u_ref[...]
    pos1 = lax.broadcasted_iota(jnp.int32, (seq, POOL_GW), 0) + 1
    means = []
    for g, w in enumerate(POOL_WINDOWS):
        tot = ext[POOL_HALO:POOL_HALO + seq, g * POOL_GW:(g + 1) * POOL_GW]
        for k in range(1, w):
            tot = tot + ext[POOL_HALO - k:POOL_HALO - k + seq, g * POOL_GW:(g + 1) * POOL_GW]
        means.append(tot / jnp.minimum(pos1, w).astype(F32))
    diff = jnp.concatenate(means, axis=1) - u_ref[...]
    o_ref[...] = _pool_mix(diff, pw_ref, ps_ref).astype(o_ref.dtype)


def pool_prompt(z, pool_w, pool_scale, *, batch, seq):
    return pl.pallas_call(
        _pool_prompt_kernel,
        out_shape=jax.ShapeDtypeStruct((batch * seq, W_POOL), BF16),
        grid=(batch,),
        in_specs=[pl.BlockSpec((seq, W_POOL), lambda b: (b, Z_POOL // W_POOL)),
                  pl.BlockSpec(pool_w.shape, lambda b: (0, 0, 0)),
                  pl.BlockSpec((1, W_POOL), lambda b: (0, 0))],
        out_specs=pl.BlockSpec((seq, W_POOL), lambda b: (b, 0)),
        scratch_shapes=[pltpu.VMEM((POOL_HALO + seq, W_POOL), F32)],
        compiler_params=_cparams(("parallel",)),
        name="pool_prompt",
    )(z, pool_w, pool_scale)


def _pool_sample_kernel(x_ref, cnt_ref, pw_ref, ps_ref, o_ref, *, t):
    for tt in range(t):
        means = []
        for g, w in enumerate(POOL_WINDOWS):
            sl = slice(g * POOL_GW, (g + 1) * POOL_GW)
            tot = x_ref[POOL_PAD + tt][:, sl]
            for k in range(1, w):
                tot = tot + x_ref[POOL_PAD + tt - k][:, sl]
            means.append(tot / cnt_ref[tt][:, sl])
        diff = jnp.concatenate(means, axis=1) - x_ref[POOL_PAD + tt]
        o_ref[tt] = _pool_mix(diff, pw_ref, ps_ref).astype(o_ref.dtype)


def pool_sample(x_tm, cnt, pool_w, pool_scale):
    rows, nseq, _ = x_tm.shape
    t = rows - POOL_PAD
    return pl.pallas_call(
        functools.partial(_pool_sample_kernel, t=t),
        out_shape=jax.ShapeDtypeStruct((t, nseq, W_POOL), BF16),
        grid=(1,),
        in_specs=[pl.BlockSpec(x_tm.shape, lambda i: (0, 0, 0)),
                  pl.BlockSpec(cnt.shape, lambda i: (0, 0, 0)),
                  pl.BlockSpec(pool_w.shape, lambda i: (0, 0, 0)),
                  pl.BlockSpec((1, W_POOL), lambda i: (0, 0))],
        out_specs=pl.BlockSpec((t, nseq, W_POOL), lambda i: (0, 0, 0)),
        compiler_params=_cparams(("arbitrary",)),
        name="pool_sample",
    )(x_tm, cnt, pool_w, pool_scale)


def _merge_kernel(oa_ref, ys_ref, yp_ref, g0_ref, g1_ref, g2_ref, wa_ref, ws_ref, wp_ref, m_ref):
    m = (g0_ref[...] * jnp.dot(oa_ref[...], wa_ref[0], preferred_element_type=F32)
         + g1_ref[...] * jnp.dot(ys_ref[...], ws_ref[0], preferred_element_type=F32)
         + g2_ref[...] * jnp.dot(yp_ref[...], wp_ref[0], preferred_element_type=F32))
    m_ref[...] = m.astype(m_ref.dtype)


def branch_merge(o_attn, y_ssm, y_pool, z, w_branch, *, layer, tm, tn):
    m = o_attn.shape[0]
    assert m % tm == 0 and D_MODEL % tn == 0
    nj = D_MODEL // tn
    return pl.pallas_call(
        _merge_kernel,
        out_shape=jax.ShapeDtypeStruct((m, D_MODEL), BF16),
        grid=(m // tm, nj),
        in_specs=[pl.BlockSpec((tm, W_ATTN), lambda i, j: (i, 0)),
                  pl.BlockSpec((tm, W_SSM), lambda i, j: (i, 0)),
                  pl.BlockSpec((tm, W_POOL), lambda i, j: (i, 0)),
                  pl.BlockSpec((tm, tn), lambda i, j: (i, j)),
                  pl.BlockSpec((tm, tn), lambda i, j: (i, nj + j)),
                  pl.BlockSpec((tm, tn), lambda i, j: (i, 2 * nj + j)),
                  pl.BlockSpec((1, W_ATTN, tn), lambda i, j: (layer, 0, j)),
                  pl.BlockSpec((1, W_SSM, tn), lambda i, j: (layer, W_ATTN // W_SSM, j)),
                  pl.BlockSpec((1, W_POOL, tn), lambda i, j: (layer, (W_ATTN + W_SSM) // W_POOL, j))],
        out_specs=pl.BlockSpec((tm, tn), lambda i, j: (i, j)),
        compiler_params=_cparams(("parallel", "arbitrary")),
        name="branch_merge",
    )(o_attn, y_ssm, y_pool, z, z, z, w_branch, w_branch, w_branch)


ROUTE_LANES = LANES


def _route(hf, wr_ref):
    wr = wr_ref[0]
    best1 = jnp.sum(hf * wr[0:1, :], axis=1, keepdims=True)
    idx1 = jnp.zeros(best1.shape, jnp.int32)
    best2 = jnp.full(best1.shape, -jnp.inf, F32)
    idx2 = jnp.zeros(best1.shape, jnp.int32)
    for e in range(1, N_EXPERTS):
        le = jnp.sum(hf * wr[e:e + 1, :], axis=1, keepdims=True)
        gt1 = le > best1
        gt2 = le > best2
        best2 = jnp.where(gt1, best1, jnp.where(gt2, le, best2))
        idx2 = jnp.where(gt1, idx1, jnp.where(gt2, e, idx2))
        best1 = jnp.where(gt1, le, best1)
        idx1 = jnp.where(gt1, e, idx1)
    ex = jnp.exp(best2 - best1)
    g1 = 1.0 / (1.0 + ex)
    g2 = ex / (1.0 + ex)
    lane = lax.broadcasted_iota(jnp.int32, (hf.shape[0], ROUTE_LANES), 1)
    return jnp.where(lane == 0, idx1.astype(F32),
                     jnp.where(lane == 1, idx2.astype(F32),
                               jnp.where(lane == 2, g1, jnp.where(lane == 3, g2, 0.0))))


def _out_proj_kernel(m_ref, w_ref, x_ref, g_ref, *rest, route):
    if route:
        wr_ref, xn_ref, hp_ref, rt_ref = rest
    else:
        xn_ref, hp_ref = rest
    xn = x_ref[...] + jnp.dot(m_ref[...], w_ref[0], preferred_element_type=F32)
    xn_ref[...] = xn
    hf = _rms(xn, g_ref[...])
    if route:
        hp_ref[...] = _pack_rows(hf)
        rt_ref[...] = _route(hf, wr_ref)
    else:
        hp_ref[...] = hf.astype(hp_ref.dtype)


def out_proj(mm, w_out, x, gain, w_router_t, *, layer, moe_layer, tm):
    m = x.shape[0]
    assert m % tm == 0
    route = moe_layer is not None
    row = lambda i: (i, 0)
    in_specs = [pl.BlockSpec((tm, D_MODEL), row),
                pl.BlockSpec((1, D_MODEL, D_MODEL), lambda i: (layer, 0, 0)),
                pl.BlockSpec((tm, D_MODEL), row),
                pl.BlockSpec((1, D_MODEL), lambda i: (0, 0))]
    args = [mm, w_out, x, gain]
    pw, pdt = (_packed_width(D_MODEL), U32) if route else (D_MODEL, BF16)
    out_shape = [jax.ShapeDtypeStruct((m, D_MODEL), F32), jax.ShapeDtypeStruct((m, pw), pdt)]
    out_specs = [pl.BlockSpec((tm, D_MODEL), row), pl.BlockSpec((tm, pw), row)]
    if route:
        in_specs.append(pl.BlockSpec((1, N_EXPERTS, D_MODEL), lambda i: (moe_layer, 0, 0)))
        args.append(w_router_t)
        out_shape.append(jax.ShapeDtypeStruct((m, ROUTE_LANES), F32))
        out_specs.append(pl.BlockSpec((tm, ROUTE_LANES), row))
    return pl.pallas_call(
        functools.partial(_out_proj_kernel, route=route),
        out_shape=tuple(out_shape),
        grid=(m // tm,),
        in_specs=in_specs,
        out_specs=tuple(out_specs),
        compiler_params=_cparams(("parallel",)),
        name="out_proj",
    )(*args)


def _tile_row(i, nu):
    return jnp.minimum(i, nu[0] - 1)


def _ffn_up_kernel(te_ref, nu_ref, x_ref, w1_ref, w3_ref, g_ref, w1b, w3b):
    i = pl.program_id(1)
    used = i < nu_ref[0]
    cur = te_ref[_tile_row(i, nu_ref)]
    prev = te_ref[_tile_row(jnp.maximum(i - 1, 0), nu_ref)]

    @pl.when(used & ((i == 0) | (cur != prev)))
    def _():
        w1b[...] = w1_ref[0, 0].astype(BF16)
        w3b[...] = w3_ref[0, 0].astype(BF16)

    @pl.when(used)
    def _():
        x = x_ref[...]
        for c in range(g_ref.shape[1] // IN_SUB):
            sl = slice(c * IN_SUB, (c + 1) * IN_SUB)
            a = jnp.dot(x, w1b[:, sl], preferred_element_type=F32)
            b = jnp.dot(x, w3b[:, sl], preferred_element_type=F32)
            g_ref[:, sl] = (a * jax.nn.sigmoid(a) * b).astype(g_ref.dtype)

    @pl.when(jnp.logical_not(used))
    def _():
        g_ref[...] = jnp.zeros(g_ref.shape, g_ref.dtype)


def ffn_up(tile_expert, n_used, xs, w1, w3, *, layer, tm, tf):
    r, xw = xs.shape
    assert r % tm == 0 and D_FF % tf == 0
    w_spec = pl.BlockSpec((1, 1, D_MODEL, tf),
                          lambda j, i, te, nu: (layer, te[_tile_row(i, nu)], 0, j))
    grid_spec = pltpu.PrefetchScalarGridSpec(
        num_scalar_prefetch=2,
        grid=(D_FF // tf, r // tm),
        in_specs=[pl.BlockSpec((tm, xw), lambda j, i, te, nu: (_tile_row(i, nu), 0)), w_spec, w_spec],
        out_specs=pl.BlockSpec((tm, tf), lambda j, i, te, nu: (i, j)),
        scratch_shapes=[pltpu.VMEM((D_MODEL, tf), BF16), pltpu.VMEM((D_MODEL, tf), BF16)])
    return pl.pallas_call(
        _ffn_up_kernel,
        out_shape=jax.ShapeDtypeStruct((r, D_FF), BF16),
        grid_spec=grid_spec,
        compiler_params=_cparams(("arbitrary", "arbitrary")),
        name="ffn_up",
    )(tile_expert, n_used, xs, w1, w3)


def _ffn_down_kernel(te_ref, nu_ref, g_ref, w2_ref, *rest, residual):
    i = pl.program_id(1)
    if residual:
        x_ref, y_ref = rest
    else:
        (y_ref,) = rest

    @pl.when(i < nu_ref[0])
    def _():
        y = jnp.dot(g_ref[...], w2_ref[0, 0], preferred_element_type=F32)
        y_ref[...] = x_ref[...] + y if residual else y

    @pl.when(i >= nu_ref[0])
    def _():
        y_ref[...] = jnp.zeros(y_ref.shape, y_ref.dtype)


def ffn_down(tile_expert, n_used, g, w2, x_res, *, layer, tm, tn):
    r = g.shape[0]
    assert r % tm == 0 and D_MODEL % tn == 0
    residual = x_res is not None
    in_specs = [pl.BlockSpec((tm, D_FF), lambda j, i, te, nu: (_tile_row(i, nu), 0)),
                pl.BlockSpec((1, 1, D_FF, tn), lambda j, i, te, nu: (layer, te[_tile_row(i, nu)], 0, j))]
    args = [tile_expert, n_used, g, w2]
    if residual:
        in_specs.append(pl.BlockSpec((tm, tn), lambda j, i, te, nu: (_tile_row(i, nu), j)))
        args.append(x_res)
    grid_spec = pltpu.PrefetchScalarGridSpec(
        num_scalar_prefetch=2,
        grid=(D_MODEL // tn, r // tm),
        in_specs=in_specs,
        out_specs=pl.BlockSpec((tm, tn), lambda j, i, te, nu: (i, j)))
    return pl.pallas_call(
        functools.partial(_ffn_down_kernel, residual=residual),
        out_shape=jax.ShapeDtypeStruct((r, D_MODEL), F32),
        grid_spec=grid_spec,
        compiler_params=_cparams(("arbitrary", "arbitrary")),
        name="ffn_down",
    )(*args)


DMA_UNROLL = 8
SCATTER_BATCH = 16


def _dispatch_kernel(dest_ref, nu_ref, hf_hbm, o_ref, row_tok, buf, sem, *, n_assign, tm):
    i = pl.program_id(0)
    slot = i % 2
    nu = nu_ref[0]

    @pl.when(i == 0)
    def _():
        def clear(r, carry):
            row_tok[r] = 0
            return carry

        def scatter(blk, carry):
            base = blk * SCATTER_BATCH
            ds = [dest_ref[base + k] for k in range(SCATTER_BATCH)]
            for k in range(SCATTER_BATCH):
                row_tok[ds[k]] = (base + k) // TOP_K
            return carry

        lax.fori_loop(0, row_tok.shape[0], clear, 0, unroll=DMA_UNROLL)
        lax.fori_loop(0, n_assign // SCATTER_BATCH, scatter, 0)

    def row_copy(tok, sl, r):
        return pltpu.make_async_copy(hf_hbm.at[pl.ds(tok, 1), :], buf.at[sl, pl.ds(r, 1), :], sem.at[sl])

    def start_tile(tile, sl):
        def issue(r, carry):
            row_copy(row_tok[tile * tm + r], sl, r).start()
            return carry

        lax.fori_loop(0, tm, issue, 0, unroll=DMA_UNROLL)

    @pl.when((i == 0) & (nu > 0))
    def _():
        start_tile(i, slot)

    @pl.when(i + 1 < nu)
    def _():
        start_tile(i + 1, 1 - slot)

    @pl.when(i < nu)
    def _():
        pltpu.make_async_copy(buf.at[slot], buf.at[slot], sem.at[slot]).wait()
        o_ref[...] = _unpack_rows(buf[slot])

    @pl.when(i >= nu)
    def _():
        o_ref[...] = jnp.zeros(o_ref.shape, o_ref.dtype)


def dispatch(dest, n_used, hf, *, n_tiles, tm):
    n_assign = dest.shape[0]
    assert n_assign % SCATTER_BATCH == 0
    width = hf.shape[1]
    grid_spec = pltpu.PrefetchScalarGridSpec(
        num_scalar_prefetch=2,
        grid=(n_tiles,),
        in_specs=[pl.BlockSpec(memory_space=pl.ANY)],
        out_specs=pl.BlockSpec((tm, D_MODEL), lambda i, d, nu: (i, 0)),
        scratch_shapes=[pltpu.SMEM((n_tiles * tm,), jnp.int32),
                        pltpu.VMEM((2, tm, width), hf.dtype),
                        pltpu.SemaphoreType.DMA((2,))])
    return pl.pallas_call(
        functools.partial(_dispatch_kernel, n_assign=n_assign, tm=tm),
        out_shape=jax.ShapeDtypeStruct((n_tiles * tm, D_MODEL), BF16),
        grid_spec=grid_spec,
        compiler_params=_cparams(("arbitrary",)),
        name="moe_dispatch",
    )(dest, n_used, hf)


def _combine_kernel(dest_ref, x_ref, rt_ref, y_hbm, o_ref, buf, sem, *, tn):
    i = pl.program_id(0)
    n = pl.num_programs(0)
    slot = i % 2

    def row_copy(d, sl, k, rr):
        return pltpu.make_async_copy(y_hbm.at[pl.ds(d, 1), :], buf.at[sl, k, pl.ds(rr, 1), :],
                                     sem.at[sl])

    def start_tile(tile, sl):
        def issue(rr, carry):
            for k in range(TOP_K):
                row_copy(dest_ref[(tile * tn + rr) * TOP_K + k], sl, k, rr).start()
            return carry

        lax.fori_loop(0, tn, issue, 0, unroll=DMA_UNROLL)

    @pl.when(i == 0)
    def _():
        start_tile(i, slot)

    @pl.when(i + 1 < n)
    def _():
        start_tile(i + 1, 1 - slot)

    pltpu.make_async_copy(buf.at[slot], buf.at[slot], sem.at[slot]).wait()

    rt = rt_ref[...]
    o_ref[...] = (x_ref[...] + (rt[:, 2:3] * buf[slot, 0] + rt[:, 3:4] * buf[slot, 1]))


def combine(dest, x, route, y, *, tn):
    m = x.shape[0]
    assert m % tn == 0
    grid_spec = pltpu.PrefetchScalarGridSpec(
        num_scalar_prefetch=1,
        grid=(m // tn,),
        in_specs=[pl.BlockSpec((tn, D_MODEL), lambda i, d: (i, 0)),
                  pl.BlockSpec((tn, ROUTE_LANES), lambda i, d: (i, 0)),
                  pl.BlockSpec(memory_space=pl.ANY)],
        out_specs=pl.BlockSpec((tn, D_MODEL), lambda i, d: (i, 0)),
        scratch_shapes=[pltpu.VMEM((2, TOP_K, tn, D_MODEL), F32),
                        pltpu.SemaphoreType.DMA((2,))])
    return pl.pallas_call(
        functools.partial(_combine_kernel, tn=tn),
        out_shape=jax.ShapeDtypeStruct((m, D_MODEL), F32),
        grid_spec=grid_spec,
        compiler_params=_cparams(("arbitrary",)),
        name="moe_combine",
    )(dest, x, route, y)


def _norm_kernel(x_ref, g_ref, o_ref):
    o_ref[...] = _rms(x_ref[...], g_ref[...])


def final_norm(x, gain, *, tm):
    m = x.shape[0]
    assert m % tm == 0
    return pl.pallas_call(
        _norm_kernel,
        out_shape=jax.ShapeDtypeStruct((m, D_MODEL), F32),
        grid=(m // tm,),
        in_specs=[pl.BlockSpec((tm, D_MODEL), lambda i: (i, 0)),
                  pl.BlockSpec((1, D_MODEL), lambda i: (0, 0))],
        out_specs=pl.BlockSpec((tm, D_MODEL), lambda i: (i, 0)),
        compiler_params=_cparams(("parallel",)),
        name="final_norm",
    )(x, gain)


def _rot_half(w):
    half = QK_ROPE // 2
    return jnp.concatenate([-w[..., half:], w[..., :half]], axis=-1)


def _lane_pad(w):
    return jnp.pad(w, [(0, 0)] * (w.ndim - 1) + [(0, ROPE_LANES - w.shape[-1])])


def _prep_w_in(w_in):
    o1 = Q_LORA
    o2 = o1 + KV_LORA
    o3 = o2 + QK_ROPE
    o4 = o3 + W_SSM
    o5 = o4 + W_POOL
    kpe = w_in[..., o2:o3]
    parts = [w_in[..., o5:], w_in[..., :o1], w_in[..., o3:o4], w_in[..., o4:o5], w_in[..., o1:o2],
             _lane_pad(kpe), _lane_pad(_rot_half(kpe))]
    return jnp.concatenate(parts, axis=-1).astype(BF16)


def _prep_w_q(w_q_up):
    depth = w_q_up.shape[0]
    w = w_q_up.reshape(depth, Q_LORA, N_HEADS, QK_NOPE + QK_ROPE)
    nope = w[..., :QK_NOPE].reshape(depth, Q_LORA, N_HEADS * QK_NOPE)
    rope = w[..., QK_NOPE:]
    rope_p = _lane_pad(rope).reshape(depth, Q_LORA, N_HEADS * ROPE_LANES)
    rot_p = _lane_pad(_rot_half(rope)).reshape(depth, Q_LORA, N_HEADS * ROPE_LANES)
    return jnp.concatenate([nope, rope_p, rot_p], axis=-1).astype(BF16)


def _block_diag_heads(w):
    depth, _, r, c = w.shape
    ng = N_HEADS // HEAD_GROUP
    w5 = w.reshape(depth, ng, HEAD_GROUP, r, c)
    eye = jnp.eye(HEAD_GROUP, dtype=w.dtype)
    out = w5[:, :, :, :, None, :] * eye[None, None, :, None, :, None]
    return out.reshape(depth, ng, HEAD_GROUP * r, HEAD_GROUP * c)


def _rope_tables(pos):
    half = QK_ROPE // 2
    freqs = ROPE_BASE ** (-jnp.arange(half, dtype=F32) / half)
    ang = pos.astype(F32)[:, None] * freqs[None, :]
    cos, sin = jnp.cos(ang), jnp.sin(ang)
    return (_lane_pad(jnp.concatenate([cos, cos], axis=-1)),
            _lane_pad(jnp.concatenate([sin, sin], axis=-1)))


def _prep_ssm(lam_re, lam_im, log_dt, b_re, b_im, c_re, c_im, steps_list):
    hi = lax.Precision.HIGHEST
    depth = lam_re.shape[0]
    g, p, n, tc = SSM_GROUPS, SSM_STATE, SSM_GROUP, SSM_TC
    lam = lax.complex(lam_re.astype(F32), lam_im.astype(F32))
    dt = jnp.exp(log_dt.astype(F32))[..., None]
    ldt = lam * dt
    pw = jnp.exp(ldt[..., None] * jnp.arange(tc + 1, dtype=F32))
    lam_bar = pw[..., 1]
    b_bar = ((lam_bar - 1.0) / lam)[..., None] * lax.complex(b_re.astype(F32), b_im.astype(F32))
    c_c = lax.complex(c_re.astype(F32), c_im.astype(F32))
    kt = jnp.real(jnp.einsum("dgmp,dgpt,dgpn->dgtmn", c_c, pw[..., :tc], b_bar, precision=hi))
    s_idx = jnp.arange(tc)[:, None]
    t_idx = jnp.arange(tc)[None, :]
    lag = jnp.clip(t_idx - s_idx, 0, tc - 1)
    toep = jnp.where((t_idx >= s_idx)[None, None, :, :, None, None], kt[:, :, lag], 0.0)
    toep = toep.reshape(depth, SSM_TILES, SSM_TILE_GROUPS, tc, tc, n, n)
    toep = toep.transpose(0, 1, 3, 2, 6, 4, 5)
    toep = toep.astype(BF16).reshape(depth, SSM_TILES, SSM_COLS, LANES)

    def wsum_for(steps):
        ex = jnp.clip(steps - 1 - jnp.arange(tc), 0, tc)
        v = pw[..., ex][..., None] * b_bar[:, :, :, None, :]
        v = jnp.where((jnp.arange(tc) < steps)[None, None, None, :, None], v, 0.0)
        v = v.transpose(0, 1, 3, 4, 2)
        w = jnp.stack([jnp.real(v), jnp.imag(v)], axis=-2)
        w = w.reshape(depth, SSM_TILES, SSM_TILE_GROUPS, tc, n, 2, p)
        w = w.transpose(0, 1, 3, 2, 4, 5, 6)
        return w.astype(BF16).reshape(depth, SSM_TILES, SSM_COLS, LANES)

    def a_for(steps):
        a = pw[..., steps].reshape(depth, SSM_TILES, SSM_HALF)
        a = jnp.stack([jnp.real(a), jnp.imag(a)], axis=2)
        return jnp.pad(a, ((0, 0), (0, 0), (0, SUBLANES - 2), (0, 0)))

    ac = c_c.transpose(0, 1, 3, 2)[:, :, :, None, :] * pw[..., 1:tc + 1][..., None]
    wout = jnp.stack([jnp.real(ac), -jnp.imag(ac)], axis=2)
    wout = wout.reshape(depth, SSM_TILES, SSM_TILE_GROUPS, 2, p, tc, n)
    wout = wout.transpose(0, 1, 3, 2, 4, 5, 6)
    wout = wout.astype(BF16).reshape(depth, SSM_TILES, SSM_COLS, LANES)
    e_tm = np.zeros((tc, n, tc, SSM_TILE_GROUPS, n), np.float32)
    e_rp = np.zeros((2, p, 2, SSM_TILE_GROUPS, p), np.float32)
    for gl in range(SSM_TILE_GROUPS):
        e_tm[np.arange(tc)[:, None], np.arange(n)[None, :], np.arange(tc)[:, None], gl, np.arange(n)[None, :]] = 1.0
        e_rp[np.arange(2)[:, None], np.arange(p)[None, :], np.arange(2)[:, None], gl, np.arange(p)[None, :]] = 1.0
    out = dict(toep=toep, wout=wout,
               e_tm=jnp.asarray(e_tm.reshape(LANES, SSM_COLS), BF16),
               e_rp=jnp.asarray(e_rp.reshape(LANES, SSM_COLS), BF16))
    for steps in steps_list:
        out[("wsum", steps)] = wsum_for(steps)
        out[("a", steps)] = a_for(steps)
    return out


def _state_pack(h):
    nb = h.shape[0]
    h5 = h.astype(F32).reshape(nb, SSM_TILES, SSM_TILE_GROUPS, SSM_STATE, 2)
    return h5.transpose(0, 1, 4, 2, 3).reshape(nb, SSM_TILES * SSM_COLS)


def _state_unpack(hp, dtype):
    nb = hp.shape[0]
    h5 = hp.reshape(nb, SSM_TILES, 2, SSM_TILE_GROUPS, SSM_STATE).transpose(0, 1, 3, 4, 2)
    return h5.reshape(nb, SSM_GROUPS, SSM_STATE, 2).astype(dtype)


def _routing_plan(route, tm, n_tiles):
    flat_e = route[:, :TOP_K].astype(jnp.int32).reshape(-1)
    onehot = (flat_e[:, None] == jnp.arange(N_EXPERTS, dtype=jnp.int32)[None, :]).astype(jnp.int32)
    csum = jnp.cumsum(onehot, axis=0)
    rank = jnp.sum((csum - onehot) * onehot, axis=1)
    counts = csum[-1]
    padded = (counts + tm - 1) // tm * tm
    pad_end = jnp.cumsum(padded)
    pad_start = pad_end - padded
    dest = (jnp.sum(onehot * pad_start[None, :], axis=1) + rank).astype(jnp.int32)
    n_used = (pad_end[-1:] // tm).astype(jnp.int32)
    tile_start = jnp.arange(n_tiles, dtype=jnp.int32) * tm
    tile_expert = jnp.minimum(jnp.sum((tile_start[:, None] >= pad_end[None, :]).astype(jnp.int32), axis=1),
                              N_EXPERTS - 1).astype(jnp.int32)
    return dest, tile_expert, n_used


TM_BIG_OPTIONS = (1088, 544, 272, 136)
TM_MID_OPTIONS = (544, 272, 136)
TN_COMBINE_OPTIONS = (128, 64, 32, 16, 8)
TM_MOE = 512
TN_IN = 1024
TN_MERGE = 512
TF_UP = 512
TN_DOWN = 1024
TQ_ATTN = 128
TK_ATTN = 512
PAGES_PER_CHUNK = 32


def _largest_divisor(n, options):
    for o in options:
        if n % o == 0:
            return o
    raise ValueError(f"no tile in {options} divides {n}")


def kernel(x_prompt, x_sample, cache_ckv, cache_kpe, state_ssm, state_pool, page_table, attn_norm_g, w_in, q_norm_g, w_q_up, kv_norm_g, w_uk, w_uv, ssm_lam_re, ssm_lam_im, ssm_log_dt, ssm_b_re, ssm_b_im, ssm_c_re, ssm_c_im, ssm_d, ssm_w_glu, pool_w, pool_scale, w_branch, w_out, ffn_norm_g, ffn_w1, ffn_w3, ffn_w2, moe_router, moe_w1, moe_w3, moe_w2, final_norm_g):
    bsz, seq, _ = x_prompt.shape
    db, t, _ = x_sample.shape
    depth = w_in.shape[0]
    past = page_table.shape[1] * PAGE_SIZE
    m_p = bsz * seq
    m_s = db * t
    m = m_p + m_s
    assert seq % SSM_TC == 0 and t <= SSM_TC and m_p % m_s == 0

    w_in2 = _prep_w_in(w_in)
    wq2 = _prep_w_q(w_q_up)
    wukbd = _block_diag_heads(w_uk.transpose(0, 2, 3, 1)).astype(BF16)
    wuvbd = _block_diag_heads(w_uv.transpose(0, 2, 1, 3)).astype(BF16)
    wuv_flat = w_uv.reshape(depth, KV_LORA, W_ATTN).astype(BF16)
    ssm_w = _prep_ssm(ssm_lam_re, ssm_lam_im, ssm_log_dt, ssm_b_re, ssm_b_im, ssm_c_re, ssm_c_im,
                      (SSM_TC, t))
    w_glu = ssm_w_glu.astype(BF16)
    pool_wb = pool_w.astype(BF16)
    wb = w_branch.astype(BF16)
    wo = w_out.astype(BF16)
    fw1, fw3 = ffn_w1[:, None], ffn_w3[:, None]
    fw2 = ffn_w2.astype(BF16)[:, None]
    mw2 = moe_w2.astype(BF16)
    w_router_t = moe_router.astype(F32).transpose(0, 2, 1)
    cache_kpe_t = cache_kpe.transpose(0, 1, 3, 2)

    pos = jnp.concatenate([jnp.tile(jnp.arange(seq), bsz), past + jnp.tile(jnp.arange(t), db)])
    cos_t, sin_t = _rope_tables(pos)
    pos_s = past + jnp.arange(t)
    cnt_s = jnp.concatenate(
        [jnp.broadcast_to(jnp.minimum(pos_s + 1, w).astype(F32)[:, None, None], (t, 1, POOL_GW))
         for w in POOL_WINDOWS], axis=-1)

    tm_big = _largest_divisor(m, TM_BIG_OPTIONS)
    tm_mid = _largest_divisor(m, TM_MID_OPTIONS)
    tn_combine = _largest_divisor(m, TN_COMBINE_OPTIONS)
    tk_attn = min(TK_ATTN, seq)
    pages_per_chunk = min(PAGES_PER_CHUNK, page_table.shape[1])
    def dense_tiles(tm):
        return jnp.zeros((m // tm,), jnp.int32), jnp.full((1,), m // tm, jnp.int32)

    te_up, nu_up = dense_tiles(tm_big)
    te_down, nu_down = dense_tiles(tm_mid)
    n_moe_tiles = (m * TOP_K) // TM_MOE + N_EXPERTS

    x = jnp.concatenate([x_prompt.reshape(m_p, D_MODEL), x_sample.reshape(m_s, D_MODEL)], axis=0)
    outs = [[] for _ in range(8)]
    for l in range(depth):
        z = in_proj(x, attn_norm_g[l][None], w_in2, layer=l, tm=tm_big, tn=TN_IN)
        qcat, c_lat, k_pe, kcat = qkv_post(z, cos_t, sin_t, q_norm_g[l][None], kv_norm_g[l][None],
                                           wq2, wukbd, layer=l, tm=tm_mid)
        oa_p = prompt_attention(qcat, kcat, wuvbd, layer=l, batch=bsz, seq=seq, tq=TQ_ATTN, tk=tk_attn)
        q3 = qcat[m_p:].reshape(db, t * N_HEADS, QCAT)
        knew3 = jnp.pad(kcat[m_p:].reshape(db, t, QCAT), ((0, 0), (0, NEW_KEYS - t), (0, 0)))
        oa_s = sample_attention(page_table, q3, knew3, cache_ckv, cache_kpe_t, wuv_flat,
                                layer=l, pc=pages_per_chunk)
        o_attn = jnp.concatenate([oa_p, oa_s.reshape(m_s, W_ATTN)], axis=0)
        ssm_ops = (ssm_w["toep"], ssm_w["wout"], ssm_w["e_tm"], ssm_w["e_rp"])
        y_p, hfin_p = ssm_scan(z, ssm_ops[0], ssm_w[("wsum", SSM_TC)], *ssm_ops[1:], ssm_w[("a", SSM_TC)],
                               jnp.zeros((bsz, SSM_TILES * SSM_COLS), F32),
                               layer=l, row_block=0, nseq=bsz, nk=seq // SSM_TC, steps=SSM_TC)
        y_s, hfin_s = ssm_scan(z, ssm_ops[0], ssm_w[("wsum", t)], *ssm_ops[1:], ssm_w[("a", t)],
                               _state_pack(state_ssm[l]),
                               layer=l, row_block=m_p // m_s, nseq=db, nk=1, steps=t)
        y_ssm = ssm_post(jnp.concatenate([y_p, y_s], axis=0), z, ssm_d[l][None], w_glu[l], tm=tm_mid)
        yp_p = pool_prompt(z, pool_wb[l], pool_scale[l][None], batch=bsz, seq=seq)
        u_pool = z[:, Z_POOL:Z_POOL + W_POOL]
        u_ext = jnp.concatenate([state_pool[l].astype(F32), u_pool[m_p:].reshape(db, t, W_POOL)], axis=1)
        yp_s = pool_sample(u_ext.transpose(1, 0, 2), cnt_s, pool_wb[l], pool_scale[l][None])
        y_pool = jnp.concatenate([yp_p, yp_s.transpose(1, 0, 2).reshape(m_s, W_POOL)], axis=0)
        mm = branch_merge(o_attn, y_ssm, y_pool, z, wb, layer=l, tm=tm_big, tn=TN_MERGE)
        j = l // 2
        if l % 2 == 0:
            xn, hfp = out_proj(mm, wo, x, ffn_norm_g[l][None], None, layer=l, moe_layer=None, tm=tm_mid)
            g_act = ffn_up(te_up, nu_up, hfp, fw1, fw3, layer=j, tm=tm_big, tf=TF_UP)
            x = ffn_down(te_down, nu_down, g_act, fw2, xn, layer=j, tm=tm_mid, tn=TN_DOWN)
        else:
            xn, hfp, route = out_proj(mm, wo, x, ffn_norm_g[l][None], w_router_t, layer=l, moe_layer=j,
                                      tm=tm_mid)
            dest, tile_expert, n_used = _routing_plan(route, TM_MOE, n_moe_tiles)
            xs = dispatch(dest, n_used, hfp, n_tiles=n_moe_tiles, tm=TM_MOE)
            g_act = ffn_up(tile_expert, n_used, xs, moe_w1, moe_w3, layer=j, tm=TM_MOE, tf=TF_UP)
            y_sorted = ffn_down(tile_expert, n_used, g_act, mw2, None, layer=j, tm=TM_MOE, tn=TN_DOWN)
            x = combine(dest, xn, route, y_sorted, tn=tn_combine)
        outs[0].append(c_lat[:m_p].reshape(bsz, seq, KV_LORA))
        outs[1].append(k_pe[:m_p].reshape(bsz, seq, QK_ROPE))
        outs[2].append(_state_unpack(hfin_p, state_ssm.dtype))
        outs[3].append(u_pool[:m_p].reshape(bsz, seq, W_POOL)[:, seq - POOL_PAD:])
        outs[4].append(c_lat[m_p:].reshape(db, t, KV_LORA))
        outs[5].append(k_pe[m_p:].reshape(db, t, QK_ROPE))
        outs[6].append(_state_unpack(hfin_s, state_ssm.dtype))
        outs[7].append(u_ext[:, t:])
    y = final_norm(x, final_norm_g[None], tm=tm_mid)
    return (y[:m_p].reshape(bsz, seq, D_MODEL), y[m_p:].reshape(db, t, D_MODEL),
            *[jnp.stack(o) for o in outs])
```

```python
import functools

import jax
import jax.numpy as jnp
import numpy as np
from jax import lax
from jax.experimental import pallas as pl
from jax.experimental.pallas import tpu as pltpu

F32 = jnp.float32
BF16 = jnp.bfloat16
U32 = jnp.uint32

D_MODEL = 2048
N_HEADS = 16
QK_NOPE = 64
QK_ROPE = 32
V_HEAD = 64
W_ATTN = N_HEADS * V_HEAD
Q_LORA = 512
KV_LORA = 256
ROPE_BASE = 10000.0
SM_SCALE = (QK_NOPE + QK_ROPE) ** -0.5
PAGE_SIZE = 128
W_SSM = 512
SSM_GROUP = 16
SSM_GROUPS = W_SSM // SSM_GROUP
SSM_STATE = 64
W_POOL = 512
POOL_WINDOWS = (2, 4, 8, 16)
POOL_GW = W_POOL // len(POOL_WINDOWS)
POOL_PAD = max(POOL_WINDOWS) - 1
N_BRANCH = 3
D_FF = 5632
N_EXPERTS = 8
TOP_K = 2
EPS = 1e-6

LANES = 128
SUBLANES = 8
VMEM_LIMIT_BYTES = 56 * 1024 * 1024

ROPE_LANES = LANES
Z_GATE = 0
Z_QC = Z_GATE + N_BRANCH * D_MODEL
Z_SSM = Z_QC + Q_LORA
Z_POOL = Z_SSM + W_SSM
Z_KV = Z_POOL + W_POOL
Z_KPE = Z_KV + KV_LORA
Z_COLS = Z_KPE + 2 * ROPE_LANES
QCAT = KV_LORA + ROPE_LANES
HEAD_GROUP = 4


def _cparams(sem, vmem=VMEM_LIMIT_BYTES):
    return pltpu.CompilerParams(dimension_semantics=sem, vmem_limit_bytes=vmem)


def _lane_tile(x, n):
    return x if n == 1 else jnp.concatenate([x] * n, axis=1)


def _rms(x, g):
    ms = jnp.mean(x * x, axis=-1, keepdims=True)
    return x * lax.rsqrt(ms + EPS) * g


def _packed_width(d):
    return d // 2


def _pack_rows(h):
    c = h.shape[1] // 2
    bits = pltpu.bitcast(h.astype(BF16).astype(F32), U32)
    return (bits[:, :c] >> 16) | (bits[:, c:] & jnp.uint32(0xFFFF0000))


def _unpack_rows(p):
    lo = pltpu.bitcast(p << 16, F32).astype(BF16)
    hi = pltpu.bitcast(p & jnp.uint32(0xFFFF0000), F32).astype(BF16)
    return jnp.concatenate([lo, hi], axis=1)


IN_SUB = 256


def _in_proj_kernel(x_ref, g_ref, w_ref, z_ref, h_ref, *, n_gate_tiles):
    j = pl.program_id(1)

    @pl.when(j == 0)
    def _():
        h_ref[...] = _rms(x_ref[...], g_ref[...]).astype(BF16)

    h = h_ref[...]
    is_gate = j < n_gate_tiles
    for c in range(z_ref.shape[1] // IN_SUB):
        sl = slice(c * IN_SUB, (c + 1) * IN_SUB)
        acc = jnp.dot(h, w_ref[0, :, sl], preferred_element_type=F32)
        z_ref[:, sl] = jnp.where(is_gate, jax.nn.sigmoid(acc), acc)


def in_proj(x, gain, w, *, layer, tm, tn):
    m, d = x.shape
    n = w.shape[2]
    assert m % tm == 0 and n % tn == 0 and (N_BRANCH * D_MODEL) % tn == 0 and tn % IN_SUB == 0
    return pl.pallas_call(
        functools.partial(_in_proj_kernel, n_gate_tiles=(N_BRANCH * D_MODEL) // tn),
        out_shape=jax.ShapeDtypeStruct((m, n), F32),
        grid=(m // tm, n // tn),
        in_specs=[pl.BlockSpec((tm, d), lambda i, j: (i, 0)),
                  pl.BlockSpec((1, d), lambda i, j: (0, 0)),
                  pl.BlockSpec((1, d, tn), lambda i, j: (layer, 0, j))],
        out_specs=pl.BlockSpec((tm, tn), lambda i, j: (i, j)),
        scratch_shapes=[pltpu.VMEM((tm, d), BF16)],
        compiler_params=_cparams(("parallel", "arbitrary")),
        name="in_proj",
    )(x, gain, w)


def _qkv_kernel(qc_ref, kv_ref, kpe_ref, cos_ref, sin_ref, qg_ref, kvg_ref, wq_ref, wuk_ref,
                qcat_ref, c_ref, kpe_out_ref, kcat_ref):
    nope_w = N_HEADS * QK_NOPE
    rope_w = N_HEADS * ROPE_LANES
    cos = cos_ref[...]
    sin = sin_ref[...]
    qn = _rms(qc_ref[...], qg_ref[...]).astype(BF16)
    q = jnp.dot(qn, wq_ref[0], preferred_element_type=F32)
    cos_h = jnp.concatenate([cos] * N_HEADS, axis=1)
    sin_h = jnp.concatenate([sin] * N_HEADS, axis=1)
    q_pe = (q[:, nope_w:nope_w + rope_w] * cos_h
            + q[:, nope_w + rope_w:nope_w + 2 * rope_w] * sin_h).astype(BF16)
    q_nope = q[:, :nope_w].astype(BF16)
    gw = HEAD_GROUP * QK_NOPE
    for g in range(N_HEADS // HEAD_GROUP):
        qa = jnp.dot(q_nope[:, g * gw:(g + 1) * gw], wuk_ref[0, g],
                     preferred_element_type=F32).astype(BF16)
        for hl in range(HEAD_GROUP):
            h = g * HEAD_GROUP + hl
            qcat_ref[:, h * QCAT:h * QCAT + KV_LORA] = qa[:, hl * KV_LORA:(hl + 1) * KV_LORA]
            qcat_ref[:, h * QCAT + KV_LORA:(h + 1) * QCAT] = q_pe[:, h * ROPE_LANES:(h + 1) * ROPE_LANES]
    c = _rms(kv_ref[...], kvg_ref[...])
    c_ref[...] = c
    kraw = kpe_ref[...]
    kr = kraw[:, :ROPE_LANES] * cos + kraw[:, ROPE_LANES:] * sin
    kpe_out_ref[...] = kr[:, :QK_ROPE]
    kcat_ref[:, :KV_LORA] = c.astype(BF16)
    kcat_ref[:, KV_LORA:] = kr.astype(BF16)


def qkv_post(z, cos_t, sin_t, q_g, kv_g, wq, wuk, *, layer, tm):
    m = z.shape[0]
    assert m % tm == 0
    nq = wq.shape[2]
    return pl.pallas_call(
        _qkv_kernel,
        out_shape=(jax.ShapeDtypeStruct((m, N_HEADS * QCAT), BF16),
                   jax.ShapeDtypeStruct((m, KV_LORA), F32),
                   jax.ShapeDtypeStruct((m, QK_ROPE), F32),
                   jax.ShapeDtypeStruct((m, QCAT), BF16)),
        grid=(m // tm,),
        in_specs=[pl.BlockSpec((tm, Q_LORA), lambda i: (i, Z_QC // Q_LORA)),
                  pl.BlockSpec((tm, KV_LORA), lambda i: (i, Z_KV // KV_LORA)),
                  pl.BlockSpec((tm, 2 * ROPE_LANES), lambda i: (i, Z_KPE // (2 * ROPE_LANES))),
                  pl.BlockSpec((tm, ROPE_LANES), lambda i: (i, 0)),
                  pl.BlockSpec((tm, ROPE_LANES), lambda i: (i, 0)),
                  pl.BlockSpec((1, Q_LORA), lambda i: (0, 0)),
                  pl.BlockSpec((1, KV_LORA), lambda i: (0, 0)),
                  pl.BlockSpec((1, Q_LORA, nq), lambda i: (layer, 0, 0)),
                  pl.BlockSpec((1,) + wuk.shape[1:], lambda i: (layer, 0, 0, 0))],
        out_specs=(pl.BlockSpec((tm, N_HEADS * QCAT), lambda i: (i, 0)),
                   pl.BlockSpec((tm, KV_LORA), lambda i: (i, 0)),
                   pl.BlockSpec((tm, QK_ROPE), lambda i: (i, 0)),
                   pl.BlockSpec((tm, QCAT), lambda i: (i, 0))),
        compiler_params=_cparams(("parallel",)),
        name="qkv_post",
    )(z, z, z, cos_t, sin_t, q_g, kv_g, wq, wuk)


def _attn_out_proj(o, wuv_ref, rows):
    ob = o.astype(BF16)
    outs = []
    for g in range(N_HEADS // HEAD_GROUP):
        wide = jnp.concatenate(
            [ob[(g * HEAD_GROUP + hl) * rows:(g * HEAD_GROUP + hl + 1) * rows, :]
             for hl in range(HEAD_GROUP)], axis=1)
        outs.append(jnp.dot(wide, wuv_ref[0, g], preferred_element_type=F32))
    return jnp.concatenate(outs, axis=1)


ATTN_ROW_GROUPS = 1


def _prompt_attn_kernel(q_ref, k_ref, wuv_ref, o_ref, m_ref, l_ref, acc_ref, *, tq, tk):
    i = pl.program_id(1)
    rows = N_HEADS * tq
    q = jnp.concatenate([q_ref[:, h * QCAT:(h + 1) * QCAT] for h in range(N_HEADS)], axis=0)
    m_ref[...] = jnp.full(m_ref.shape, -jnp.inf, F32)
    l_ref[...] = jnp.zeros(l_ref.shape, F32)
    acc_ref[...] = jnp.zeros(acc_ref.shape, F32)

    gr = rows // ATTN_ROW_GROUPS

    def chunk(j, masked):
        start = pl.multiple_of(j * tk, tk)
        k = k_ref[pl.ds(start, tk), :]
        if masked:
            qpos = i * tq + (lax.broadcasted_iota(jnp.int32, (gr, tk), 0) & (tq - 1))
            kpos = start + lax.broadcasted_iota(jnp.int32, (gr, tk), 1)
            visible = kpos <= qpos
        for g in range(ATTN_ROW_GROUPS):
            rs = slice(g * gr, (g + 1) * gr)
            s = lax.dot_general(q[rs], k, (((1,), (1,)), ((), ())), preferred_element_type=F32) * SM_SCALE
            if masked:
                s = jnp.where(visible, s, -jnp.inf)
            m_old = m_ref[rs, :]
            m_new = jnp.maximum(m_old, jnp.max(s, axis=1, keepdims=True))
            alpha = jnp.exp(m_old - m_new)
            p = jnp.exp(s - _lane_tile(m_new, tk // LANES))
            l_ref[rs, :] = alpha * l_ref[rs, :] + jnp.sum(p, axis=1, keepdims=True)
            acc_ref[rs, :] = (_lane_tile(alpha, KV_LORA // LANES) * acc_ref[rs, :]
                              + jnp.dot(p.astype(BF16), k[:, :KV_LORA], preferred_element_type=F32))
            m_ref[rs, :] = m_new

    n_full = (i * tq) // tk

    def full_body(j, carry):
        chunk(j, False)
        return carry

    lax.fori_loop(0, n_full, full_body, 0)
    chunk(n_full, True)
    o = acc_ref[...] / _lane_tile(l_ref[...], KV_LORA // LANES)
    o_ref[...] = _attn_out_proj(o, wuv_ref, tq).astype(o_ref.dtype)


def prompt_attention(qcat, kcat, wuv, *, layer, batch, seq, tq, tk):
    assert seq % tq == 0 and seq % tk == 0 and tk % tq == 0 and (tq & (tq - 1)) == 0
    nq = seq // tq
    rows = N_HEADS * tq
    return pl.pallas_call(
        functools.partial(_prompt_attn_kernel, tq=tq, tk=tk),
        out_shape=jax.ShapeDtypeStruct((batch * seq, W_ATTN), BF16),
        grid=(batch, nq),
        in_specs=[pl.BlockSpec((tq, N_HEADS * QCAT), lambda b, i: (b * nq + i, 0)),
                  pl.BlockSpec((seq, QCAT), lambda b, i: (b, 0)),
                  pl.BlockSpec((1,) + wuv.shape[1:], lambda b, i: (layer, 0, 0, 0))],
        out_specs=pl.BlockSpec((tq, W_ATTN), lambda b, i: (b * nq + i, 0)),
        scratch_shapes=[pltpu.VMEM((rows, LANES), F32), pltpu.VMEM((rows, LANES), F32),
                        pltpu.VMEM((rows, KV_LORA), F32)],
        compiler_params=_cparams(("parallel", "arbitrary")),
        name="prompt_attn",
    )(qcat, kcat, wuv)


NEW_KEYS = LANES
SAMPLE_KEY_BLOCKS = 1


def _sample_attn_kernel(pt_ref, q_ref, knew_ref, ckv_hbm, kpe_hbm, wuv_ref, o_ref,
                        ckv_buf, kpe_buf, sem, m_ref, l_ref, acc_ref,
                        *, layer, n_chunks, pc, t):
    b = pl.program_id(0)
    c = pl.program_id(1)
    total = pl.num_programs(0) * n_chunks
    step = b * n_chunks + c
    slot = step % 2
    rows = t * N_HEADS

    def copies(bb, cc, sl):
        out = []
        for p in range(pc):
            page = pt_ref[bb, cc * pc + p]
            out.append(pltpu.make_async_copy(
                ckv_hbm.at[layer, page], ckv_buf.at[sl, pl.ds(p * PAGE_SIZE, PAGE_SIZE), :],
                sem.at[sl, 0]))
            out.append(pltpu.make_async_copy(kpe_hbm.at[layer, page], kpe_buf.at[sl, p], sem.at[sl, 1]))
        return out

    @pl.when(step == 0)
    def _():
        for cp in copies(b, c, slot):
            cp.start()

    @pl.when(step + 1 < total)
    def _():
        last = c + 1 == n_chunks
        nb = jnp.where(last, b + 1, b)
        nc = jnp.where(last, 0, c + 1)
        for cp in copies(nb, nc, 1 - slot):
            cp.start()

    @pl.when(c == 0)
    def _():
        m_ref[...] = jnp.full(m_ref.shape, -jnp.inf, F32)
        l_ref[...] = jnp.zeros(l_ref.shape, F32)
        acc_ref[...] = jnp.zeros(acc_ref.shape, F32)

    pltpu.make_async_copy(ckv_buf.at[slot], ckv_buf.at[slot], sem.at[slot, 0]).wait()
    pltpu.make_async_copy(kpe_buf.at[slot], kpe_buf.at[slot], sem.at[slot, 1]).wait()

    q = q_ref[0]
    nt = (((1,), (1,)), ((), ()))

    def partial_softmax(s, v):
        m_blk = jnp.max(s, axis=1, keepdims=True)
        p = jnp.exp(s - m_blk)
        return m_blk, jnp.sum(p, axis=1, keepdims=True), jnp.dot(p.astype(BF16), v, preferred_element_type=F32)

    def merge(parts):
        m_old = m_ref[...]
        m_new = m_old
        for m_blk, _, _ in parts:
            m_new = jnp.maximum(m_new, m_blk)
        alpha = jnp.exp(m_old - m_new)
        l_new = alpha * l_ref[...]
        acc = _lane_tile(alpha, KV_LORA // LANES) * acc_ref[...]
        for m_blk, l_blk, o_blk in parts:
            w = jnp.exp(m_blk - m_new)
            l_new = l_new + w * l_blk
            acc = acc + _lane_tile(w, KV_LORA // LANES) * o_blk
        m_ref[...] = m_new
        l_ref[...] = l_new
        acc_ref[...] = acc

    sub_pages = pc // SAMPLE_KEY_BLOCKS
    sub_keys = sub_pages * PAGE_SIZE
    parts = []
    for sb in range(SAMPLE_KEY_BLOCKS):
        ck = ckv_buf[slot, sb * sub_keys:(sb + 1) * sub_keys, :].astype(BF16)
        kr_t = jnp.concatenate([kpe_buf[slot, sb * sub_pages + p] for p in range(sub_pages)], axis=1)
        kr_t = kr_t.astype(BF16)
        s = (lax.dot_general(q[:, :KV_LORA], ck, nt, preferred_element_type=F32)
             + jnp.dot(q[:, KV_LORA:KV_LORA + QK_ROPE], kr_t, preferred_element_type=F32)) * SM_SCALE
        parts.append(partial_softmax(s, ck))
    merge(parts)

    @pl.when(c == n_chunks - 1)
    def _():
        kn = knew_ref[0]
        sn = lax.dot_general(q, kn, nt, preferred_element_type=F32) * SM_SCALE
        qi = lax.broadcasted_iota(jnp.int32, (rows, NEW_KEYS), 0) // N_HEADS
        ki = lax.broadcasted_iota(jnp.int32, (rows, NEW_KEYS), 1)
        sn = jnp.where(ki <= qi, sn, -jnp.inf)
        merge([partial_softmax(sn, kn[:, :KV_LORA])])
        o = (acc_ref[...] / _lane_tile(l_ref[...], KV_LORA // LANES)).astype(BF16)
        full = jnp.dot(o, wuv_ref[0], preferred_element_type=F32)
        head = lax.broadcasted_iota(jnp.int32, (rows, W_ATTN), 0) % N_HEADS
        col_head = lax.broadcasted_iota(jnp.int32, (rows, W_ATTN), 1) // V_HEAD
        own = jnp.where(head == col_head, full, 0.0)
        o_ref[0] = jnp.concatenate(
            [jnp.sum(own[tt * N_HEADS:(tt + 1) * N_HEADS, :], axis=0, keepdims=True) for tt in range(t)],
            axis=0).astype(o_ref.dtype)


def sample_attention(page_table, q3, knew3, cache_ckv, cache_kpe_t, wuv_flat, *, layer, pc):
    db, n_pages = page_table.shape
    rows = q3.shape[1]
    t = rows // N_HEADS
    assert n_pages % pc == 0 and pc % SAMPLE_KEY_BLOCKS == 0 and t <= NEW_KEYS
    n_chunks = n_pages // pc
    n_keys = pc * PAGE_SIZE
    grid_spec = pltpu.PrefetchScalarGridSpec(
        num_scalar_prefetch=1,
        grid=(db, n_chunks),
        in_specs=[pl.BlockSpec((1, rows, QCAT), lambda b, c, pt: (b, 0, 0)),
                  pl.BlockSpec((1, NEW_KEYS, QCAT), lambda b, c, pt: (b, 0, 0)),
                  pl.BlockSpec(memory_space=pl.ANY),
                  pl.BlockSpec(memory_space=pl.ANY),
                  pl.BlockSpec((1, KV_LORA, W_ATTN), lambda b, c, pt: (layer, 0, 0))],
        out_specs=pl.BlockSpec((1, t, W_ATTN), lambda b, c, pt: (b, 0, 0)),
        scratch_shapes=[pltpu.VMEM((2, n_keys, KV_LORA), F32),
                        pltpu.VMEM((2, pc, QK_ROPE, PAGE_SIZE), F32),
                        pltpu.SemaphoreType.DMA((2, 2)),
                        pltpu.VMEM((rows, LANES), F32),
                        pltpu.VMEM((rows, LANES), F32),
                        pltpu.VMEM((rows, KV_LORA), F32)])
    return pl.pallas_call(
        functools.partial(_sample_attn_kernel, layer=layer, n_chunks=n_chunks, pc=pc, t=t),
        out_shape=jax.ShapeDtypeStruct((db, t, W_ATTN), BF16),
        grid_spec=grid_spec,
        compiler_params=_cparams(("arbitrary", "arbitrary")),
        name="sample_attn",
    )(page_table, q3, knew3, cache_ckv, cache_kpe_t, wuv_flat)


SSM_TC = 8
SSM_TILE_GROUPS = LANES // SSM_GROUP
SSM_TILES = W_SSM // LANES
SSM_HALF = SSM_TILE_GROUPS * SSM_STATE
SSM_COLS = 2 * SSM_HALF


def _ssm_expand(small, e_ref, row_div, col_div):
    full = jnp.dot(small, e_ref[...], preferred_element_type=F32)
    rg = (lax.broadcasted_iota(jnp.int32, full.shape, 0) // row_div) % SSM_TILE_GROUPS
    cg = (lax.broadcasted_iota(jnp.int32, full.shape, 1) // col_div) % SSM_TILE_GROUPS
    return jnp.where(rg == cg, full, 0.0).astype(BF16)


def _ssm_kernel(u_ref, toep_ref, wsum_ref, wout_ref, etm_ref, erp_ref, a_ref, h0_ref, y_ref, hfin_ref,
                s_scr, hp_scr, *, nseq, nk, steps):
    r = nseq * nk
    kw = steps * LANES
    u = jnp.concatenate([u_ref[pl.ds(s, r, stride=steps), :].astype(BF16) for s in range(steps)],
                        axis=1)
    toep = _ssm_expand(toep_ref[0, 0, :kw, :], etm_ref, SSM_GROUP, SSM_GROUP)
    wsum = _ssm_expand(wsum_ref[0, 0, :kw, :], erp_ref, SSM_GROUP, SSM_STATE)
    wout = _ssm_expand(wout_ref[0, 0], etm_ref, SSM_STATE, SSM_GROUP)
    sk_all = jnp.dot(u, wsum, preferred_element_type=F32)
    a = a_ref[0, 0]
    a_re = a[0:1, :]
    a_im = a[1:2, :]
    h0 = h0_ref[...]
    n_slab = SSM_COLS // LANES
    if nk == 1:
        hp = h0
        h_re, h_im = h0[:, :SSM_HALF], h0[:, SSM_HALF:]
        hfin_ref[...] = jnp.concatenate(
            [a_re * h_re - a_im * h_im + sk_all[:, :SSM_HALF],
             a_re * h_im + a_im * h_re + sk_all[:, SSM_HALF:]], axis=1)
    else:
        for j in range(n_slab):
            s_scr[j] = sk_all[:, j * LANES:(j + 1) * LANES]
        a_re_b = jnp.broadcast_to(a_re, (nseq, SSM_HALF))
        a_im_b = jnp.broadcast_to(a_im, (nseq, SSM_HALF))

        def body(k, h):
            for j in range(n_slab):
                hp_scr[j, pl.ds(k, nseq, stride=nk), :] = h[:, j * LANES:(j + 1) * LANES]
            sk = jnp.concatenate([s_scr[j, pl.ds(k, nseq, stride=nk), :] for j in range(n_slab)], axis=1)
            h_re, h_im = h[:, :SSM_HALF], h[:, SSM_HALF:]
            return jnp.concatenate(
                [a_re_b * h_re - a_im_b * h_im + sk[:, :SSM_HALF],
                 a_re_b * h_im + a_im_b * h_re + sk[:, SSM_HALF:]], axis=1)

        hfin_ref[...] = lax.fori_loop(0, nk, body, h0)
        hp = jnp.concatenate([hp_scr[j] for j in range(n_slab)], axis=1)
    y = (jnp.dot(u, toep, preferred_element_type=F32)
         + jnp.dot(hp.astype(BF16), wout, preferred_element_type=F32))
    for tt in range(steps):
        y_ref[pl.ds(tt, r, stride=steps), :] = y[:, tt * LANES:(tt + 1) * LANES]


def ssm_scan(z, toep, wsum, wout, e_tm, e_rp, a, h0, *, layer, row_block, nseq, nk, steps):
    rows = nseq * nk * steps
    r = nseq * nk
    w_spec = pl.BlockSpec((1, 1, SSM_COLS, LANES), lambda j: (layer, j, 0, 0))
    e_spec = pl.BlockSpec((LANES, SSM_COLS), lambda j: (0, 0))
    return pl.pallas_call(
        functools.partial(_ssm_kernel, nseq=nseq, nk=nk, steps=steps),
        out_shape=(jax.ShapeDtypeStruct((rows, W_SSM), F32),
                   jax.ShapeDtypeStruct((nseq, SSM_TILES * SSM_COLS), F32)),
        grid=(SSM_TILES,),
        in_specs=[pl.BlockSpec((rows, LANES), lambda j: (row_block, Z_SSM // LANES + j)),
                  w_spec, w_spec, w_spec, e_spec, e_spec,
                  pl.BlockSpec((1, 1, SUBLANES, SSM_HALF), lambda j: (layer, j, 0, 0)),
                  pl.BlockSpec((nseq, SSM_COLS), lambda j: (0, j))],
        out_specs=(pl.BlockSpec((rows, LANES), lambda j: (0, j)),
                   pl.BlockSpec((nseq, SSM_COLS), lambda j: (0, j))),
        scratch_shapes=[pltpu.VMEM((SSM_COLS // LANES, r, LANES), F32),
                        pltpu.VMEM((SSM_COLS // LANES, r, LANES), F32)],
        compiler_params=_cparams(("parallel",)),
        name="ssm_scan",
    )(z, toep, wsum, wout, e_tm, e_rp, a, h0)


def _gelu_tanh(x):
    return 0.5 * x * (1.0 + jnp.tanh(np.sqrt(2.0 / np.pi).astype(np.float32)
                                     * (x + np.float32(0.044715) * (x * x * x))))


def _ssm_post_kernel(y_ref, u_ref, d_ref, wg_ref, o_ref):
    y = y_ref[...] + d_ref[...] * u_ref[...]
    zz = _gelu_tanh(y)
    gate = jax.nn.sigmoid(jnp.dot(zz.astype(BF16), wg_ref[...], preferred_element_type=F32))
    o_ref[...] = (zz * gate).astype(o_ref.dtype)


def ssm_post(y_raw, z, d_skip, w_glu, *, tm):
    m = y_raw.shape[0]
    assert m % tm == 0
    return pl.pallas_call(
        _ssm_post_kernel,
        out_shape=jax.ShapeDtypeStruct((m, W_SSM), BF16),
        grid=(m // tm,),
        in_specs=[pl.BlockSpec((tm, W_SSM), lambda i: (i, 0)),
                  pl.BlockSpec((tm, W_SSM), lambda i: (i, Z_SSM // W_SSM)),
                  pl.BlockSpec((1, W_SSM), lambda i: (0, 0)),
                  pl.BlockSpec((W_SSM, W_SSM), lambda i: (0, 0))],
        out_specs=pl.BlockSpec((tm, W_SSM), lambda i: (i, 0)),
        compiler_params=_cparams(("parallel",)),
        name="ssm_post",
    )(y_raw, z, d_skip, w_glu)


POOL_HALO = 2 * SUBLANES


def _pool_mix(diff, pw_ref, ps_ref):
    outs = [jnp.dot(diff[:, g * POOL_GW:(g + 1) * POOL_GW].astype(BF16), pw_ref[g],
                    preferred_element_type=F32) for g in range(len(POOL_WINDOWS))]
    return jnp.concatenate(outs, axis=1) * ps_ref[...]


def _pool_prompt_kernel(u_ref, pw_ref, ps_ref, o_ref, ext):
    seq = u_ref.shape[0]
    ext[0:POOL_HALO, :] = jnp.zeros((POOL_HALO, W_POOL), F32)
    ext[POOL_HALO:, :] = u_ref[...]
    pos1 = lax.broadcasted_iota(jnp.int32, (seq, POOL_GW), 0) + 1
    means = []
    for g, w in enumerate(POOL_WINDOWS):
        tot = ext[POOL_HALO:POOL_HALO + seq, g * POOL_GW:(g + 1) * POOL_GW]
        for k in range(1, w):
            tot = tot + ext[POOL_HALO - k:POOL_HALO - k + seq, g * POOL_GW:(g + 1) * POOL_GW]
        means.append(tot / jnp.minimum(pos1, w).astype(F32))
    diff = jnp.concatenate(means, axis=1) - u_ref[...]
    o_ref[...] = _pool_mix(diff, pw_ref, ps_ref).astype(o_ref.dtype)


def pool_prompt(z, pool_w, pool_scale, *, batch, seq):
    return pl.pallas_call(
        _pool_prompt_kernel,
        out_shape=jax.ShapeDtypeStruct((batch * seq, W_POOL), BF16),
        grid=(batch,),
        in_specs=[pl.BlockSpec((seq, W_POOL), lambda b: (b, Z_POOL // W_POOL)),
                  pl.BlockSpec(pool_w.shape, lambda b: (0, 0, 0)),
                  pl.BlockSpec((1, W_POOL), lambda b: (0, 0))],
        out_specs=pl.BlockSpec((seq, W_POOL), lambda b: (b, 0)),
        scratch_shapes=[pltpu.VMEM((POOL_HALO + seq, W_POOL), F32)],
        compiler_params=_cparams(("parallel",)),
        name="pool_prompt",
    )(z, pool_w, pool_scale)


def _pool_sample_kernel(x_ref, cnt_ref, pw_ref, ps_ref, o_ref, *, t):
    for tt in range(t):
        means = []
        for g, w in enumerate(POOL_WINDOWS):
            sl = slice(g * POOL_GW, (g + 1) * POOL_GW)
            tot = x_ref[POOL_PAD + tt][:, sl]
            for k in range(1, w):
                tot = tot + x_ref[POOL_PAD + tt - k][:, sl]
            means.append(tot / cnt_ref[tt][:, sl])
        diff = jnp.concatenate(means, axis=1) - x_ref[POOL_PAD + tt]
        o_ref[tt] = _pool_mix(diff, pw_ref, ps_ref).astype(o_ref.dtype)


def pool_sample(x_tm, cnt, pool_w, pool_scale):
    rows, nseq, _ = x_tm.shape
    t = rows - POOL_PAD
    return pl.pallas_call(
        functools.partial(_pool_sample_kernel, t=t),
        out_shape=jax.ShapeDtypeStruct((t, nseq, W_POOL), BF16),
        grid=(1,),
        in_specs=[pl.BlockSpec(x_tm.shape, lambda i: (0, 0, 0)),
                  pl.BlockSpec(cnt.shape, lambda i: (0, 0, 0)),
                  pl.BlockSpec(pool_w.shape, lambda i: (0, 0, 0)),
                  pl.BlockSpec((1, W_POOL), lambda i: (0, 0))],
        out_specs=pl.BlockSpec((t, nseq, W_POOL), lambda i: (0, 0, 0)),
        compiler_params=_cparams(("arbitrary",)),
        name="pool_sample",
    )(x_tm, cnt, pool_w, pool_scale)


def _merge_kernel(oa_ref, ys_ref, yp_ref, g0_ref, g1_ref, g2_ref, wa_ref, ws_ref, wp_ref, m_ref):
    m = (g0_ref[...] * jnp.dot(oa_ref[...], wa_ref[0], preferred_element_type=F32)
         + g1_ref[...] * jnp.dot(ys_ref[...], ws_ref[0], preferred_element_type=F32)
         + g2_ref[...] * jnp.dot(yp_ref[...], wp_ref[0], preferred_element_type=F32))
    m_ref[...] = m.astype(m_ref.dtype)


def branch_merge(o_attn, y_ssm, y_pool, z, w_branch, *, layer, tm, tn):
    m = o_attn.shape[0]
    assert m % tm == 0 and D_MODEL % tn == 0
    nj = D_MODEL // tn
    return pl.pallas_call(
        _merge_kernel,
        out_shape=jax.ShapeDtypeStruct((m, D_MODEL), BF16),
        grid=(m // tm, nj),
        in_specs=[pl.BlockSpec((tm, W_ATTN), lambda i, j: (i, 0)),
                  pl.BlockSpec((tm, W_SSM), lambda i, j: (i, 0)),
                  pl.BlockSpec((tm, W_POOL), lambda i, j: (i, 0)),
                  pl.BlockSpec((tm, tn), lambda i, j: (i, j)),
                  pl.BlockSpec((tm, tn), lambda i, j: (i, nj + j)),
                  pl.BlockSpec((tm, tn), lambda i, j: (i, 2 * nj + j)),
                  pl.BlockSpec((1, W_ATTN, tn), lambda i, j: (layer, 0, j)),
                  pl.BlockSpec((1, W_SSM, tn), lambda i, j: (layer, W_ATTN // W_SSM, j)),
                  pl.BlockSpec((1, W_POOL, tn), lambda i, j: (layer, (W_ATTN + W_SSM) // W_POOL, j))],
        out_specs=pl.BlockSpec((tm, tn), lambda i, j: (i, j)),
        compiler_params=_cparams(("parallel", "arbitrary")),
        name="branch_merge",
    )(o_attn, y_ssm, y_pool, z, z, z, w_branch, w_branch, w_branch)


ROUTE_LANES = LANES


def _route(hf, wr_ref):
    wr = wr_ref[0]
    best1 = jnp.sum(hf * wr[0:1, :], axis=1, keepdims=True)
    idx1 = jnp.zeros(best1.shape, jnp.int32)
    best2 = jnp.full(best1.shape, -jnp.inf, F32)
    idx2 = jnp.zeros(best1.shape, jnp.int32)
    for e in range(1, N_EXPERTS):
        le = jnp.sum(hf * wr[e:e + 1, :], axis=1, keepdims=True)
        gt1 = le > best1
        gt2 = le > best2
        best2 = jnp.where(gt1, best1, jnp.where(gt2, le, best2))
        idx2 = jnp.where(gt1, idx1, jnp.where(gt2, e, idx2))
        best1 = jnp.where(gt1, le, best1)
        idx1 = jnp.where(gt1, e, idx1)
    ex = jnp.exp(best2 - best1)
    g1 = 1.0 / (1.0 + ex)
    g2 = ex / (1.0 + ex)
    lane = lax.broadcasted_iota(jnp.int32, (hf.shape[0], ROUTE_LANES), 1)
    return jnp.where(lane == 0, idx1.astype(F32),
                     jnp.where(lane == 1, idx2.astype(F32),
                               jnp.where(lane == 2, g1, jnp.where(lane == 3, g2, 0.0))))


def _out_proj_kernel(m_ref, w_ref, x_ref, g_ref, *rest, route):
    if route:
        wr_ref, xn_ref, hp_ref, rt_ref = rest
    else:
        xn_ref, hp_ref = rest
    xn = x_ref[...] + jnp.dot(m_ref[...], w_ref[0], preferred_element_type=F32)
    xn_ref[...] = xn
    hf = _rms(xn, g_ref[...])
    if route:
        hp_ref[...] = _pack_rows(hf)
        rt_ref[...] = _route(hf, wr_ref)
    else:
        hp_ref[...] = hf.astype(hp_ref.dtype)


def out_proj(mm, w_out, x, gain, w_router_t, *, layer, moe_layer, tm):
    m = x.shape[0]
    assert m % tm == 0
    route = moe_layer is not None
    row = lambda i: (i, 0)
    in_specs = [pl.BlockSpec((tm, D_MODEL), row),
                pl.BlockSpec((1, D_MODEL, D_MODEL), lambda i: (layer, 0, 0)),
                pl.BlockSpec((tm, D_MODEL), row),
                pl.BlockSpec((1, D_MODEL), lambda i: (0, 0))]
    args = [mm, w_out, x, gain]
    pw, pdt = (_packed_width(D_MODEL), U32) if route else (D_MODEL, BF16)
    out_shape = [jax.ShapeDtypeStruct((m, D_MODEL), F32), jax.ShapeDtypeStruct((m, pw), pdt)]
    out_specs = [pl.BlockSpec((tm, D_MODEL), row), pl.BlockSpec((tm, pw), row)]
    if route:
        in_specs.append(pl.BlockSpec((1, N_EXPERTS, D_MODEL), lambda i: (moe_layer, 0, 0)))
        args.append(w_router_t)
        out_shape.append(jax.ShapeDtypeStruct((m, ROUTE_LANES), F32))
        out_specs.append(pl.BlockSpec((tm, ROUTE_LANES), row))
    return pl.pallas_call(
        functools.partial(_out_proj_kernel, route=route),
        out_shape=tuple(out_shape),
        grid=(m // tm,),
        in_specs=in_specs,
        out_specs=tuple(out_specs),
        compiler_params=_cparams(("parallel",)),
        name="out_proj",
    )(*args)


def _tile_row(i, nu):
    return jnp.minimum(i, nu[0] - 1)


def _ffn_up_kernel(te_ref, nu_ref, x_ref, w1_ref, w3_ref, g_ref, w1b, w3b):
    i = pl.program_id(1)
    used = i < nu_ref[0]
    cur = te_ref[_tile_row(i, nu_ref)]
    prev = te_ref[_tile_row(jnp.maximum(i - 1, 0), nu_ref)]

    @pl.when(used & ((i == 0) | (cur != prev)))
    def _():
        w1b[...] = w1_ref[0, 0].astype(BF16)
        w3b[...] = w3_ref[0, 0].astype(BF16)

    @pl.when(used)
    def _():
        x = x_ref[...]
        for c in range(g_ref.shape[1] // IN_SUB):
            sl = slice(c * IN_SUB, (c + 1) * IN_SUB)
            a = jnp.dot(x, w1b[:, sl], preferred_element_type=F32)
            b = jnp.dot(x, w3b[:, sl], preferred_element_type=F32)
            g_ref[:, sl] = (a * jax.nn.sigmoid(a) * b).astype(g_ref.dtype)

    @pl.when(jnp.logical_not(used))
    def _():
        g_ref[...] = jnp.zeros(g_ref.shape, g_ref.dtype)


def ffn_up(tile_expert, n_used, xs, w1, w3, *, layer, tm, tf):
    r, xw = xs.shape
    assert r % tm == 0 and D_FF % tf == 0
    w_spec = pl.BlockSpec((1, 1, D_MODEL, tf),
                          lambda j, i, te, nu: (layer, te[_tile_row(i, nu)], 0, j))
    grid_spec = pltpu.PrefetchScalarGridSpec(
        num_scalar_prefetch=2,
        grid=(D_FF // tf, r // tm),
        in_specs=[pl.BlockSpec((tm, xw), lambda j, i, te, nu: (_tile_row(i, nu), 0)), w_spec, w_spec],
        out_specs=pl.BlockSpec((tm, tf), lambda j, i, te, nu: (i, j)),
        scratch_shapes=[pltpu.VMEM((D_MODEL, tf), BF16), pltpu.VMEM((D_MODEL, tf), BF16)])
    return pl.pallas_call(
        _ffn_up_kernel,
        out_shape=jax.ShapeDtypeStruct((r, D_FF), BF16),
        grid_spec=grid_spec,
        compiler_params=_cparams(("arbitrary", "arbitrary")),
        name="ffn_up",
    )(tile_expert, n_used, xs, w1, w3)


def _ffn_down_kernel(te_ref, nu_ref, g_ref, w2_ref, *rest, residual):
    i = pl.program_id(1)
    if residual:
        x_ref, y_ref = rest
    else:
        (y_ref,) = rest

    @pl.when(i < nu_ref[0])
    def _():
        y = jnp.dot(g_ref[...], w2_ref[0, 0], preferred_element_type=F32)
        y_ref[...] = x_ref[...] + y if residual else y

    @pl.when(i >= nu_ref[0])
    def _():
        y_ref[...] = jnp.zeros(y_ref.shape, y_ref.dtype)


def ffn_down(tile_expert, n_used, g, w2, x_res, *, layer, tm, tn):
    r = g.shape[0]
    assert r % tm == 0 and D_MODEL % tn == 0
    residual = x_res is not None
    in_specs = [pl.BlockSpec((tm, D_FF), lambda j, i, te, nu: (_tile_row(i, nu), 0)),
                pl.BlockSpec((1, 1, D_FF, tn), lambda j, i, te, nu: (layer, te[_tile_row(i, nu)], 0, j))]
    args = [tile_expert, n_used, g, w2]
    if residual:
        in_specs.append(pl.BlockSpec((tm, tn), lambda j, i, te, nu: (_tile_row(i, nu), j)))
        args.append(x_res)
    grid_spec = pltpu.PrefetchScalarGridSpec(
        num_scalar_prefetch=2,
        grid=(D_MODEL // tn, r // tm),
        in_specs=in_specs,
        out_specs=pl.BlockSpec((tm, tn), lambda j, i, te, nu: (i, j)))
    return pl.pallas_call(
        functools.partial(_ffn_down_kernel, residual=residual),
        out_shape=jax.ShapeDtypeStruct((r, D_MODEL), F32),
        grid_spec=grid_spec,
        compiler_params=_cparams(("arbitrary", "arbitrary")),
        name="ffn_down",
    )(*args)


DMA_UNROLL = 8
SCATTER_BATCH = 32


def _dispatch_kernel(dest_ref, nu_ref, hf_hbm, o_ref, row_tok, buf, sem, *, n_assign, tm):
    i = pl.program_id(0)
    slot = i % 2
    nu = nu_ref[0]

    @pl.when(i == 0)
    def _():
        def clear(r, carry):
            row_tok[r] = 0
            return carry

        def scatter(blk, carry):
            base = blk * SCATTER_BATCH
            tok0 = blk * (SCATTER_BATCH // TOP_K)
            ds = [dest_ref[base + k] for k in range(SCATTER_BATCH)]
            for k in range(SCATTER_BATCH):
                row_tok[ds[k]] = tok0 + k // TOP_K
            return carry

        lax.fori_loop(0, row_tok.shape[0], clear, 0, unroll=DMA_UNROLL)
        lax.fori_loop(0, n_assign // SCATTER_BATCH, scatter, 0)

    def row_copy(tok, sl, r):
        return pltpu.make_async_copy(hf_hbm.at[pl.ds(tok, 1), :], buf.at[sl, pl.ds(r, 1), :], sem.at[sl])

    def start_tile(tile, sl):
        def issue(r, carry):
            row_copy(row_tok[tile * tm + r], sl, r).start()
            return carry

        lax.fori_loop(0, tm, issue, 0, unroll=DMA_UNROLL)

    @pl.when((i == 0) & (nu > 0))
    def _():
        start_tile(i, slot)

    @pl.when(i + 1 < nu)
    def _():
        start_tile(i + 1, 1 - slot)

    @pl.when(i < nu)
    def _():
        pltpu.make_async_copy(buf.at[slot], buf.at[slot], sem.at[slot]).wait()
        o_ref[...] = _unpack_rows(buf[slot])

    @pl.when(i >= nu)
    def _():
        o_ref[...] = jnp.zeros(o_ref.shape, o_ref.dtype)


def dispatch(dest, n_used, hf, *, n_tiles, tm):
    n_assign = dest.shape[0]
    assert n_assign % SCATTER_BATCH == 0
    width = hf.shape[1]
    grid_spec = pltpu.PrefetchScalarGridSpec(
        num_scalar_prefetch=2,
        grid=(n_tiles,),
        in_specs=[pl.BlockSpec(memory_space=pl.ANY)],
        out_specs=pl.BlockSpec((tm, D_MODEL), lambda i, d, nu: (i, 0)),
        scratch_shapes=[pltpu.SMEM((n_tiles * tm,), jnp.int32),
                        pltpu.VMEM((2, tm, width), hf.dtype),
                        pltpu.SemaphoreType.DMA((2,))])
    return pl.pallas_call(
        functools.partial(_dispatch_kernel, n_assign=n_assign, tm=tm),
        out_shape=jax.ShapeDtypeStruct((n_tiles * tm, D_MODEL), BF16),
        grid_spec=grid_spec,
        compiler_params=_cparams(("arbitrary",)),
        name="moe_dispatch",
    )(dest, n_used, hf)


def _combine_kernel(dest_ref, x_ref, rt_ref, y_hbm, o_ref, buf, sem, *, tn):
    i = pl.program_id(0)
    n = pl.num_programs(0)
    slot = i % 2

    def row_copy(d, sl, k, rr):
        return pltpu.make_async_copy(y_hbm.at[pl.ds(d, 1), :], buf.at[sl, k, pl.ds(rr, 1), :],
                                     sem.at[sl])

    def start_tile(tile, sl):
        def issue(rr, carry):
            for k in range(TOP_K):
                row_copy(dest_ref[(tile * tn + rr) * TOP_K + k], sl, k, rr).start()
            return carry

        lax.fori_loop(0, tn, issue, 0, unroll=DMA_UNROLL)

    @pl.when(i == 0)
    def _():
        start_tile(i, slot)

    @pl.when(i + 1 < n)
    def _():
        start_tile(i + 1, 1 - slot)

    pltpu.make_async_copy(buf.at[slot], buf.at[slot], sem.at[slot]).wait()

    rt = rt_ref[...]
    o_ref[...] = (x_ref[...] + (rt[:, 2:3] * buf[slot, 0] + rt[:, 3:4] * buf[slot, 1]))


def combine(dest, x, route, y, *, tn):
    m = x.shape[0]
    assert m % tn == 0
    grid_spec = pltpu.PrefetchScalarGridSpec(
        num_scalar_prefetch=1,
        grid=(m // tn,),
        in_specs=[pl.BlockSpec((tn, D_MODEL), lambda i, d: (i, 0)),
                  pl.BlockSpec((tn, ROUTE_LANES), lambda i, d: (i, 0)),
                  pl.BlockSpec(memory_space=pl.ANY)],
        out_specs=pl.BlockSpec((tn, D_MODEL), lambda i, d: (i, 0)),
        scratch_shapes=[pltpu.VMEM((2, TOP_K, tn, D_MODEL), F32),
                        pltpu.SemaphoreType.DMA((2,))])
    return pl.pallas_call(
        functools.partial(_combine_kernel, tn=tn),
        out_shape=jax.ShapeDtypeStruct((m, D_MODEL), F32),
        grid_spec=grid_spec,
        compiler_params=_cparams(("arbitrary",)),
        name="moe_combine",
    )(dest, x, route, y)


def _norm_kernel(x_ref, g_ref, o_ref):
    o_ref[...] = _rms(x_ref[...], g_ref[...])


def final_norm(x, gain, *, tm):
    m = x.shape[0]
    assert m % tm == 0
    return pl.pallas_call(
        _norm_kernel,
        out_shape=jax.ShapeDtypeStruct((m, D_MODEL), F32),
        grid=(m // tm,),
        in_specs=[pl.BlockSpec((tm, D_MODEL), lambda i: (i, 0)),
                  pl.BlockSpec((1, D_MODEL), lambda i: (0, 0))],
        out_specs=pl.BlockSpec((tm, D_MODEL), lambda i: (i, 0)),
        compiler_params=_cparams(("parallel",)),
        name="final_norm",
    )(x, gain)


def _rot_half(w):
    half = QK_ROPE // 2
    return jnp.concatenate([-w[..., half:], w[..., :half]], axis=-1)


def _lane_pad(w):
    return jnp.pad(w, [(0, 0)] * (w.ndim - 1) + [(0, ROPE_LANES - w.shape[-1])])


def _prep_w_in(w_in):
    o1 = Q_LORA
    o2 = o1 + KV_LORA
    o3 = o2 + QK_ROPE
    o4 = o3 + W_SSM
    o5 = o4 + W_POOL
    kpe = w_in[..., o2:o3]
    parts = [w_in[..., o5:], w_in[..., :o1], w_in[..., o3:o4], w_in[..., o4:o5], w_in[..., o1:o2],
             _lane_pad(kpe), _lane_pad(_rot_half(kpe))]
    return jnp.concatenate(parts, axis=-1).astype(BF16)


def _prep_w_q(w_q_up):
    depth = w_q_up.shape[0]
    w = w_q_up.reshape(depth, Q_LORA, N_HEADS, QK_NOPE + QK_ROPE)
    nope = w[..., :QK_NOPE].reshape(depth, Q_LORA, N_HEADS * QK_NOPE)
    rope = w[..., QK_NOPE:]
    rope_p = _lane_pad(rope).reshape(depth, Q_LORA, N_HEADS * ROPE_LANES)
    rot_p = _lane_pad(_rot_half(rope)).reshape(depth, Q_LORA, N_HEADS * ROPE_LANES)
    return jnp.concatenate([nope, rope_p, rot_p], axis=-1).astype(BF16)


def _block_diag_heads(w):
    depth, _, r, c = w.shape
    ng = N_HEADS // HEAD_GROUP
    w5 = w.reshape(depth, ng, HEAD_GROUP, r, c)
    eye = jnp.eye(HEAD_GROUP, dtype=w.dtype)
    out = w5[:, :, :, :, None, :] * eye[None, None, :, None, :, None]
    return out.reshape(depth, ng, HEAD_GROUP * r, HEAD_GROUP * c)


def _rope_tables(pos):
    half = QK_ROPE // 2
    freqs = ROPE_BASE ** (-jnp.arange(half, dtype=F32) / half)
    ang = pos.astype(F32)[:, None] * freqs[None, :]
    cos, sin = jnp.cos(ang), jnp.sin(ang)
    return (_lane_pad(jnp.concatenate([cos, cos], axis=-1)),
            _lane_pad(jnp.concatenate([sin, sin], axis=-1)))


def _prep_ssm(lam_re, lam_im, log_dt, b_re, b_im, c_re, c_im, steps_list):
    hi = lax.Precision.HIGHEST
    depth = lam_re.shape[0]
    g, p, n, tc = SSM_GROUPS, SSM_STATE, SSM_GROUP, SSM_TC
    lam = lax.complex(lam_re.astype(F32), lam_im.astype(F32))
    dt = jnp.exp(log_dt.astype(F32))[..., None]
    ldt = lam * dt
    pw = jnp.exp(ldt[..., None] * jnp.arange(tc + 1, dtype=F32))
    lam_bar = pw[..., 1]
    b_bar = ((lam_bar - 1.0) / lam)[..., None] * lax.complex(b_re.astype(F32), b_im.astype(F32))
    c_c = lax.complex(c_re.astype(F32), c_im.astype(F32))
    kt = jnp.real(jnp.einsum("dgmp,dgpt,dgpn->dgtmn", c_c, pw[..., :tc], b_bar, precision=hi))
    s_idx = jnp.arange(tc)[:, None]
    t_idx = jnp.arange(tc)[None, :]
    lag = jnp.clip(t_idx - s_idx, 0, tc - 1)
    toep = jnp.where((t_idx >= s_idx)[None, None, :, :, None, None], kt[:, :, lag], 0.0)
    toep = toep.reshape(depth, SSM_TILES, SSM_TILE_GROUPS, tc, tc, n, n)
    toep = toep.transpose(0, 1, 3, 2, 6, 4, 5)
    toep = toep.astype(BF16).reshape(depth, SSM_TILES, SSM_COLS, LANES)

    def wsum_for(steps):
        ex = jnp.clip(steps - 1 - jnp.arange(tc), 0, tc)
        v = pw[..., ex][..., None] * b_bar[:, :, :, None, :]
        v = jnp.where((jnp.arange(tc) < steps)[None, None, None, :, None], v, 0.0)
        v = v.transpose(0, 1, 3, 4, 2)
        w = jnp.stack([jnp.real(v), jnp.imag(v)], axis=-2)
        w = w.reshape(depth, SSM_TILES, SSM_TILE_GROUPS, tc, n, 2, p)
        w = w.transpose(0, 1, 3, 2, 4, 5, 6)
        return w.astype(BF16).reshape(depth, SSM_TILES, SSM_COLS, LANES)

    def a_for(steps):
        a = pw[..., steps].reshape(depth, SSM_TILES, SSM_HALF)
        a = jnp.stack([jnp.real(a), jnp.imag(a)], axis=2)
        return jnp.pad(a, ((0, 0), (0, 0), (0, SUBLANES - 2), (0, 0)))

    ac = c_c.transpose(0, 1, 3, 2)[:, :, :, None, :] * pw[..., 1:tc + 1][..., None]
    wout = jnp.stack([jnp.real(ac), -jnp.imag(ac)], axis=2)
    wout = wout.reshape(depth, SSM_TILES, SSM_TILE_GROUPS, 2, p, tc, n)
    wout = wout.transpose(0, 1, 3, 2, 4, 5, 6)
    wout = wout.astype(BF16).reshape(depth, SSM_TILES, SSM_COLS, LANES)
    e_tm = np.zeros((tc, n, tc, SSM_TILE_GROUPS, n), np.float32)
    e_rp = np.zeros((2, p, 2, SSM_TILE_GROUPS, p), np.float32)
    for gl in range(SSM_TILE_GROUPS):
        e_tm[np.arange(tc)[:, None], np.arange(n)[None, :], np.arange(tc)[:, None], gl, np.arange(n)[None, :]] = 1.0
        e_rp[np.arange(2)[:, None], np.arange(p)[None, :], np.arange(2)[:, None], gl, np.arange(p)[None, :]] = 1.0
    out = dict(toep=toep, wout=wout,
               e_tm=jnp.asarray(e_tm.reshape(LANES, SSM_COLS), BF16),
               e_rp=jnp.asarray(e_rp.reshape(LANES, SSM_COLS), BF16))
    for steps in steps_list:
        out[("wsum", steps)] = wsum_for(steps)
        out[("a", steps)] = a_for(steps)
    return out


def _state_pack(h):
    nb = h.shape[0]
    h5 = h.astype(F32).reshape(nb, SSM_TILES, SSM_TILE_GROUPS, SSM_STATE, 2)
    return h5.transpose(0, 1, 4, 2, 3).reshape(nb, SSM_TILES * SSM_COLS)


def _state_unpack(hp, dtype):
    nb = hp.shape[0]
    h5 = hp.reshape(nb, SSM_TILES, 2, SSM_TILE_GROUPS, SSM_STATE).transpose(0, 1, 3, 4, 2)
    return h5.reshape(nb, SSM_GROUPS, SSM_STATE, 2).astype(dtype)


def _routing_plan(route, tm, n_tiles):
    flat_e = route[:, :TOP_K].astype(jnp.int32).reshape(-1)
    onehot = (flat_e[:, None] == jnp.arange(N_EXPERTS, dtype=jnp.int32)[None, :]).astype(jnp.int32)
    csum = jnp.cumsum(onehot, axis=0)
    rank = jnp.sum((csum - onehot) * onehot, axis=1)
    counts = csum[-1]
    padded = (counts + tm - 1) // tm * tm
    pad_end = jnp.cumsum(padded)
    pad_start = pad_end - padded
    dest = (jnp.sum(onehot * pad_start[None, :], axis=1) + rank).astype(jnp.int32)
    n_used = (pad_end[-1:] // tm).astype(jnp.int32)
    tile_start = jnp.arange(n_tiles, dtype=jnp.int32) * tm
    tile_expert = jnp.minimum(jnp.sum((tile_start[:, None] >= pad_end[None, :]).astype(jnp.int32), axis=1),
                              N_EXPERTS - 1).astype(jnp.int32)
    return dest, tile_expert, n_used


TM_BIG_OPTIONS = (1088, 544, 272, 136)
TM_MID_OPTIONS = (544, 272, 136)
TN_COMBINE_OPTIONS = (128, 64, 32, 16, 8)
TM_MOE = 512
TN_IN = 1024
TN_MERGE = 512
TF_UP = 512
TN_DOWN = 1024
TQ_ATTN = 256
TK_ATTN = 512
PAGES_PER_CHUNK = 32


def _largest_divisor(n, options):
    for o in options:
        if n % o == 0:
            return o
    raise ValueError(f"no tile in {options} divides {n}")


def kernel(x_prompt, x_sample, cache_ckv, cache_kpe, state_ssm, state_pool, page_table, attn_norm_g, w_in, q_norm_g, w_q_up, kv_norm_g, w_uk, w_uv, ssm_lam_re, ssm_lam_im, ssm_log_dt, ssm_b_re, ssm_b_im, ssm_c_re, ssm_c_im, ssm_d, ssm_w_glu, pool_w, pool_scale, w_branch, w_out, ffn_norm_g, ffn_w1, ffn_w3, ffn_w2, moe_router, moe_w1, moe_w3, moe_w2, final_norm_g):
    bsz, seq, _ = x_prompt.shape
    db, t, _ = x_sample.shape
    depth = w_in.shape[0]
    past = page_table.shape[1] * PAGE_SIZE
    m_p = bsz * seq
    m_s = db * t
    m = m_p + m_s
    assert seq % SSM_TC == 0 and t <= SSM_TC and m_p % m_s == 0

    w_in2 = _prep_w_in(w_in)
    wq2 = _prep_w_q(w_q_up)
    wukbd = _block_diag_heads(w_uk.transpose(0, 2, 3, 1)).astype(BF16)
    wuvbd = _block_diag_heads(w_uv.transpose(0, 2, 1, 3)).astype(BF16)
    wuv_flat = w_uv.reshape(depth, KV_LORA, W_ATTN).astype(BF16)
    ssm_w = _prep_ssm(ssm_lam_re, ssm_lam_im, ssm_log_dt, ssm_b_re, ssm_b_im, ssm_c_re, ssm_c_im,
                      (SSM_TC, t))
    w_glu = ssm_w_glu.astype(BF16)
    pool_wb = pool_w.astype(BF16)
    wb = w_branch.astype(BF16)
    wo = w_out.astype(BF16)
    fw1, fw3 = ffn_w1[:, None], ffn_w3[:, None]
    fw2 = ffn_w2.astype(BF16)[:, None]
    mw2 = moe_w2.astype(BF16)
    w_router_t = moe_router.astype(F32).transpose(0, 2, 1)
    cache_kpe_t = cache_kpe.transpose(0, 1, 3, 2)

    pos = jnp.concatenate([jnp.tile(jnp.arange(seq), bsz), past + jnp.tile(jnp.arange(t), db)])
    cos_t, sin_t = _rope_tables(pos)
    pos_s = past + jnp.arange(t)
    cnt_s = jnp.concatenate(
        [jnp.broadcast_to(jnp.minimum(pos_s + 1, w).astype(F32)[:, None, None], (t, 1, POOL_GW))
         for w in POOL_WINDOWS], axis=-1)

    tm_big = _largest_divisor(m, TM_BIG_OPTIONS)
    tm_mid = _largest_divisor(m, TM_MID_OPTIONS)
    tn_combine = _largest_divisor(m, TN_COMBINE_OPTIONS)
    tk_attn = min(TK_ATTN, seq)
    pages_per_chunk = min(PAGES_PER_CHUNK, page_table.shape[1])
    def dense_tiles(tm):
        return jnp.zeros((m // tm,), jnp.int32), jnp.full((1,), m // tm, jnp.int32)

    te_up, nu_up = dense_tiles(tm_big)
    te_down, nu_down = dense_tiles(tm_mid)
    n_moe_tiles = (m * TOP_K) // TM_MOE + N_EXPERTS

    x = jnp.concatenate([x_prompt.reshape(m_p, D_MODEL), x_sample.reshape(m_s, D_MODEL)], axis=0)
    outs = [[] for _ in range(8)]
    for l in range(depth):
        z = in_proj(x, attn_norm_g[l][None], w_in2, layer=l, tm=tm_big, tn=TN_IN)
        qcat, c_lat, k_pe, kcat = qkv_post(z, cos_t, sin_t, q_norm_g[l][None], kv_norm_g[l][None],
                                           wq2, wukbd, layer=l, tm=tm_mid)
        oa_p = prompt_attention(qcat, kcat, wuvbd, layer=l, batch=bsz, seq=seq, tq=TQ_ATTN, tk=tk_attn)
        q3 = qcat[m_p:].reshape(db, t * N_HEADS, QCAT)
        knew3 = jnp.pad(kcat[m_p:].reshape(db, t, QCAT), ((0, 0), (0, NEW_KEYS - t), (0, 0)))
        oa_s = sample_attention(page_table, q3, knew3, cache_ckv, cache_kpe_t, wuv_flat,
                                layer=l, pc=pages_per_chunk)
        o_attn = jnp.concatenate([oa_p, oa_s.reshape(m_s, W_ATTN)], axis=0)
        ssm_ops = (ssm_w["toep"], ssm_w["wout"], ssm_w["e_tm"], ssm_w["e_rp"])
        y_p, hfin_p = ssm_scan(z, ssm_ops[0], ssm_w[("wsum", SSM_TC)], *ssm_ops[1:], ssm_w[("a", SSM_TC)],
                               jnp.zeros((bsz, SSM_TILES * SSM_COLS), F32),
                               layer=l, row_block=0, nseq=bsz, nk=seq // SSM_TC, steps=SSM_TC)
        y_s, hfin_s = ssm_scan(z, ssm_ops[0], ssm_w[("wsum", t)], *ssm_ops[1:], ssm_w[("a", t)],
                               _state_pack(state_ssm[l]),
                               layer=l, row_block=m_p // m_s, nseq=db, nk=1, steps=t)
        y_ssm = ssm_post(jnp.concatenate([y_p, y_s], axis=0), z, ssm_d[l][None], w_glu[l], tm=tm_mid)
        yp_p = pool_prompt(z, pool_wb[l], pool_scale[l][None], batch=bsz, seq=seq)
        u_pool = z[:, Z_POOL:Z_POOL + W_POOL]
        u_ext = jnp.concatenate([state_pool[l].astype(F32), u_pool[m_p:].reshape(db, t, W_POOL)], axis=1)
        yp_s = pool_sample(u_ext.transpose(1, 0, 2), cnt_s, pool_wb[l], pool_scale[l][None])
        y_pool = jnp.concatenate([yp_p, yp_s.transpose(1, 0, 2).reshape(m_s, W_POOL)], axis=0)
        mm = branch_merge(o_attn, y_ssm, y_pool, z, wb, layer=l, tm=tm_big, tn=TN_MERGE)
        j = l // 2
        if l % 2 == 0:
            xn, hfp = out_proj(mm, wo, x, ffn_norm_g[l][None], None, layer=l, moe_layer=None, tm=tm_mid)
            g_act = ffn_up(te_up, nu_up, hfp, fw1, fw3, layer=j, tm=tm_big, tf=TF_UP)
            x = ffn_down(te_down, nu_down, g_act, fw2, xn, layer=j, tm=tm_mid, tn=TN_DOWN)
        else:
            xn, hfp, route = out_proj(mm, wo, x, ffn_norm_g[l][None], w_router_t, layer=l, moe_layer=j,
                                      tm=tm_mid)
            dest, tile_expert, n_used = _routing_plan(route, TM_MOE, n_moe_tiles)
            xs = dispatch(dest, n_used, hfp, n_tiles=n_moe_tiles, tm=TM_MOE)
            g_act = ffn_up(tile_expert, n_used, xs, moe_w1, moe_w3, layer=j, tm=TM_MOE, tf=TF_UP)
            y_sorted = ffn_down(tile_expert, n_used, g_act, mw2, None, layer=j, tm=TM_MOE, tn=TN_DOWN)
            x = combine(dest, xn, route, y_sorted, tn=tn_combine)
        outs[0].append(c_lat[:m_p].reshape(bsz, seq, KV_LORA))
        outs[1].append(k_pe[:m_p].reshape(bsz, seq, QK_ROPE))
        outs[2].append(_state_unpack(hfin_p, state_ssm.dtype))
        outs[3].append(u_pool[:m_p].reshape(bsz, seq, W_POOL)[:, seq - POOL_PAD:])
        outs[4].append(c_lat[m_p:].reshape(db, t, KV_LORA))
        outs[5].append(k_pe[m_p:].reshape(db, t, QK_ROPE))
        outs[6].append(_state_unpack(hfin_s, state_ssm.dtype))
        outs[7].append(u_ext[:, t:])
    y = final_norm(x, final_norm_g[None], tm=tm_mid)
    return (y[:m_p].reshape(bsz, seq, D_MODEL), y[m_p:].reshape(db, t, D_MODEL),
            *[jnp.stack(o) for o in outs])
```

```python
import functools

import jax
import jax.numpy as jnp
import numpy as np
from jax import lax
from jax.experimental import pallas as pl
from jax.experimental.pallas import tpu as pltpu

F32 = jnp.float32
BF16 = jnp.bfloat16
U32 = jnp.uint32

D_MODEL = 2048
N_HEADS = 16
QK_NOPE = 64
QK_ROPE = 32
V_HEAD = 64
W_ATTN = N_HEADS * V_HEAD
Q_LORA = 512
KV_LORA = 256
ROPE_BASE = 10000.0
SM_SCALE = (QK_NOPE + QK_ROPE) ** -0.5
PAGE_SIZE = 128
W_SSM = 512
SSM_GROUP = 16
SSM_GROUPS = W_SSM // SSM_GROUP
SSM_STATE = 64
W_POOL = 512
POOL_WINDOWS = (2, 4, 8, 16)
POOL_GW = W_POOL // len(POOL_WINDOWS)
POOL_PAD = max(POOL_WINDOWS) - 1
N_BRANCH = 3
D_FF = 5632
N_EXPERTS = 8
TOP_K = 2
EPS = 1e-6

LANES = 128
SUBLANES = 8
VMEM_LIMIT_BYTES = 56 * 1024 * 1024

ROPE_LANES = LANES
Z_GATE = 0
Z_QC = Z_GATE + N_BRANCH * D_MODEL
Z_SSM = Z_QC + Q_LORA
Z_POOL = Z_SSM + W_SSM
Z_KV = Z_POOL + W_POOL
Z_KPE = Z_KV + KV_LORA
Z_COLS = Z_KPE + 2 * ROPE_LANES
QCAT = KV_LORA + ROPE_LANES
HEAD_GROUP = 4


def _cparams(sem, vmem=VMEM_LIMIT_BYTES):
    return pltpu.CompilerParams(dimension_semantics=sem, vmem_limit_bytes=vmem)


def _lane_tile(x, n):
    return x if n == 1 else jnp.concatenate([x] * n, axis=1)


def _rms(x, g):
    ms = jnp.mean(x * x, axis=-1, keepdims=True)
    return x * lax.rsqrt(ms + EPS) * g


def _packed_width(d):
    return d // 2


def _pack_rows(h):
    c = h.shape[1] // 2
    bits = pltpu.bitcast(h.astype(BF16).astype(F32), U32)
    return (bits[:, :c] >> 16) | (bits[:, c:] & jnp.uint32(0xFFFF0000))


def _unpack_rows(p):
    lo = pltpu.bitcast(p << 16, F32).astype(BF16)
    hi = pltpu.bitcast(p & jnp.uint32(0xFFFF0000), F32).astype(BF16)
    return jnp.concatenate([lo, hi], axis=1)


IN_SUB = 256


def _in_proj_kernel(x_ref, g_ref, w_ref, z_ref, h_ref, *, n_gate_tiles):
    j = pl.program_id(1)

    @pl.when(j == 0)
    def _():
        h_ref[...] = _rms(x_ref[...], g_ref[...]).astype(BF16)

    h = h_ref[...]
    is_gate = j < n_gate_tiles
    for c in range(z_ref.shape[1] // IN_SUB):
        sl = slice(c * IN_SUB, (c + 1) * IN_SUB)
        acc = jnp.dot(h, w_ref[0, :, sl], preferred_element_type=F32)
        z_ref[:, sl] = jnp.where(is_gate, jax.nn.sigmoid(acc), acc)


def in_proj(x, gain, w, *, layer, tm, tn):
    m, d = x.shape
    n = w.shape[2]
    assert m % tm == 0 and n % tn == 0 and (N_BRANCH * D_MODEL) % tn == 0 and tn % IN_SUB == 0
    return pl.pallas_call(
        functools.partial(_in_proj_kernel, n_gate_tiles=(N_BRANCH * D_MODEL) // tn),
        out_shape=jax.ShapeDtypeStruct((m, n), F32),
        grid=(m // tm, n // tn),
        in_specs=[pl.BlockSpec((tm, d), lambda i, j: (i, 0)),
                  pl.BlockSpec((1, d), lambda i, j: (0, 0)),
                  pl.BlockSpec((1, d, tn), lambda i, j: (layer, 0, j))],
        out_specs=pl.BlockSpec((tm, tn), lambda i, j: (i, j)),
        scratch_shapes=[pltpu.VMEM((tm, d), BF16)],
        compiler_params=_cparams(("parallel", "arbitrary")),
        name="in_proj",
    )(x, gain, w)


def _qkv_kernel(qc_ref, kv_ref, kpe_ref, cos_ref, sin_ref, qg_ref, kvg_ref, wq_ref, wuk_ref,
                qcat_ref, c_ref, kpe_out_ref, kcat_ref):
    nope_w = N_HEADS * QK_NOPE
    rope_w = N_HEADS * ROPE_LANES
    cos = cos_ref[...]
    sin = sin_ref[...]
    qn = _rms(qc_ref[...], qg_ref[...]).astype(BF16)
    q = jnp.dot(qn, wq_ref[0], preferred_element_type=F32)
    cos_h = jnp.concatenate([cos] * N_HEADS, axis=1)
    sin_h = jnp.concatenate([sin] * N_HEADS, axis=1)
    q_pe = (q[:, nope_w:nope_w + rope_w] * cos_h
            + q[:, nope_w + rope_w:nope_w + 2 * rope_w] * sin_h).astype(BF16)
    q_nope = q[:, :nope_w].astype(BF16)
    gw = HEAD_GROUP * QK_NOPE
    for g in range(N_HEADS // HEAD_GROUP):
        qa = jnp.dot(q_nope[:, g * gw:(g + 1) * gw], wuk_ref[0, g],
                     preferred_element_type=F32).astype(BF16)
        for hl in range(HEAD_GROUP):
            h = g * HEAD_GROUP + hl
            qcat_ref[:, h * QCAT:h * QCAT + KV_LORA] = qa[:, hl * KV_LORA:(hl + 1) * KV_LORA]
            qcat_ref[:, h * QCAT + KV_LORA:(h + 1) * QCAT] = q_pe[:, h * ROPE_LANES:(h + 1) * ROPE_LANES]
    c = _rms(kv_ref[...], kvg_ref[...])
    c_ref[...] = c
    kraw = kpe_ref[...]
    kr = kraw[:, :ROPE_LANES] * cos + kraw[:, ROPE_LANES:] * sin
    kpe_out_ref[...] = kr[:, :QK_ROPE]
    kcat_ref[:, :KV_LORA] = c.astype(BF16)
    kcat_ref[:, KV_LORA:] = kr.astype(BF16)


def qkv_post(z, cos_t, sin_t, q_g, kv_g, wq, wuk, *, layer, tm):
    m = z.shape[0]
    assert m % tm == 0
    nq = wq.shape[2]
    return pl.pallas_call(
        _qkv_kernel,
        out_shape=(jax.ShapeDtypeStruct((m, N_HEADS * QCAT), BF16),
                   jax.ShapeDtypeStruct((m, KV_LORA), F32),
                   jax.ShapeDtypeStruct((m, QK_ROPE), F32),
                   jax.ShapeDtypeStruct((m, QCAT), BF16)),
        grid=(m // tm,),
        in_specs=[pl.BlockSpec((tm, Q_LORA), lambda i: (i, Z_QC // Q_LORA)),
                  pl.BlockSpec((tm, KV_LORA), lambda i: (i, Z_KV // KV_LORA)),
                  pl.BlockSpec((tm, 2 * ROPE_LANES), lambda i: (i, Z_KPE // (2 * ROPE_LANES))),
                  pl.BlockSpec((tm, ROPE_LANES), lambda i: (i, 0)),
                  pl.BlockSpec((tm, ROPE_LANES), lambda i: (i, 0)),
                  pl.BlockSpec((1, Q_LORA), lambda i: (0, 0)),
                  pl.BlockSpec((1, KV_LORA), lambda i: (0, 0)),
                  pl.BlockSpec((1, Q_LORA, nq), lambda i: (layer, 0, 0)),
                  pl.BlockSpec((1,) + wuk.shape[1:], lambda i: (layer, 0, 0, 0))],
        out_specs=(pl.BlockSpec((tm, N_HEADS * QCAT), lambda i: (i, 0)),
                   pl.BlockSpec((tm, KV_LORA), lambda i: (i, 0)),
                   pl.BlockSpec((tm, QK_ROPE), lambda i: (i, 0)),
                   pl.BlockSpec((tm, QCAT), lambda i: (i, 0))),
        compiler_params=_cparams(("parallel",)),
        name="qkv_post",
    )(z, z, z, cos_t, sin_t, q_g, kv_g, wq, wuk)


def _attn_out_proj(o, wuv_ref, rows):
    ob = o.astype(BF16)
    outs = []
    for g in range(N_HEADS // HEAD_GROUP):
        wide = jnp.concatenate(
            [ob[(g * HEAD_GROUP + hl) * rows:(g * HEAD_GROUP + hl + 1) * rows, :]
             for hl in range(HEAD_GROUP)], axis=1)
        outs.append(jnp.dot(wide, wuv_ref[0, g], preferred_element_type=F32))
    return jnp.concatenate(outs, axis=1)


ATTN_ROW_GROUPS = 1


def _prompt_attn_kernel(q_ref, k_ref, wuv_ref, o_ref, m_ref, l_ref, acc_ref, *, tq, tk):
    i = pl.program_id(1)
    rows = N_HEADS * tq
    q = jnp.concatenate([q_ref[:, h * QCAT:(h + 1) * QCAT] for h in range(N_HEADS)], axis=0)
    m_ref[...] = jnp.full(m_ref.shape, -jnp.inf, F32)
    l_ref[...] = jnp.zeros(l_ref.shape, F32)
    acc_ref[...] = jnp.zeros(acc_ref.shape, F32)

    gr = rows // ATTN_ROW_GROUPS

    def chunk(j, masked):
        start = pl.multiple_of(j * tk, tk)
        k = k_ref[pl.ds(start, tk), :]
        if masked:
            qpos = i * tq + (lax.broadcasted_iota(jnp.int32, (gr, tk), 0) & (tq - 1))
            kpos = start + lax.broadcasted_iota(jnp.int32, (gr, tk), 1)
            visible = kpos <= qpos
        for g in range(ATTN_ROW_GROUPS):
            rs = slice(g * gr, (g + 1) * gr)
            s = lax.dot_general(q[rs], k, (((1,), (1,)), ((), ())), preferred_element_type=F32) * SM_SCALE
            if masked:
                s = jnp.where(visible, s, -jnp.inf)
            m_old = m_ref[rs, :]
            m_new = jnp.maximum(m_old, jnp.max(s, axis=1, keepdims=True))
            alpha = jnp.exp(m_old - m_new)
            p = jnp.exp(s - _lane_tile(m_new, tk // LANES))
            l_ref[rs, :] = alpha * l_ref[rs, :] + jnp.sum(p, axis=1, keepdims=True)
            acc_ref[rs, :] = (_lane_tile(alpha, KV_LORA // LANES) * acc_ref[rs, :]
                              + jnp.dot(p.astype(BF16), k[:, :KV_LORA], preferred_element_type=F32))
            m_ref[rs, :] = m_new

    n_full = (i * tq) // tk

    def full_body(j, carry):
        chunk(j, False)
        return carry

    lax.fori_loop(0, n_full, full_body, 0)
    chunk(n_full, True)
    o = acc_ref[...] / _lane_tile(l_ref[...], KV_LORA // LANES)
    o_ref[...] = _attn_out_proj(o, wuv_ref, tq).astype(o_ref.dtype)


def prompt_attention(qcat, kcat, wuv, *, layer, batch, seq, tq, tk):
    assert seq % tq == 0 and seq % tk == 0 and tk % tq == 0 and (tq & (tq - 1)) == 0
    nq = seq // tq
    rows = N_HEADS * tq
    return pl.pallas_call(
        functools.partial(_prompt_attn_kernel, tq=tq, tk=tk),
        out_shape=jax.ShapeDtypeStruct((batch * seq, W_ATTN), BF16),
        grid=(batch, nq),
        in_specs=[pl.BlockSpec((tq, N_HEADS * QCAT), lambda b, i: (b * nq + i, 0)),
                  pl.BlockSpec((seq, QCAT), lambda b, i: (b, 0)),
                  pl.BlockSpec((1,) + wuv.shape[1:], lambda b, i: (layer, 0, 0, 0))],
        out_specs=pl.BlockSpec((tq, W_ATTN), lambda b, i: (b * nq + i, 0)),
        scratch_shapes=[pltpu.VMEM((rows, LANES), F32), pltpu.VMEM((rows, LANES), F32),
                        pltpu.VMEM((rows, KV_LORA), F32)],
        compiler_params=_cparams(("parallel", "arbitrary")),
        name="prompt_attn",
    )(qcat, kcat, wuv)


NEW_KEYS = LANES
SAMPLE_KEY_BLOCKS = 1


def _sample_attn_kernel(pt_ref, q_ref, knew_ref, ckv_hbm, kpe_hbm, wuv_ref, o_ref,
                        ckv_buf, kpe_buf, sem, m_ref, l_ref, acc_ref,
                        *, layer, n_chunks, pc, t):
    b = pl.program_id(0)
    c = pl.program_id(1)
    total = pl.num_programs(0) * n_chunks
    step = b * n_chunks + c
    slot = step % 2
    rows = t * N_HEADS

    def copies(bb, cc, sl):
        out = []
        for p in range(pc):
            page = pt_ref[bb, cc * pc + p]
            out.append(pltpu.make_async_copy(
                ckv_hbm.at[layer, page], ckv_buf.at[sl, pl.ds(p * PAGE_SIZE, PAGE_SIZE), :],
                sem.at[sl, 0]))
            out.append(pltpu.make_async_copy(kpe_hbm.at[layer, page], kpe_buf.at[sl, p], sem.at[sl, 1]))
        return out

    @pl.when(step == 0)
    def _():
        for cp in copies(b, c, slot):
            cp.start()

    @pl.when(step + 1 < total)
    def _():
        last = c + 1 == n_chunks
        nb = jnp.where(last, b + 1, b)
        nc = jnp.where(last, 0, c + 1)
        for cp in copies(nb, nc, 1 - slot):
            cp.start()

    @pl.when(c == 0)
    def _():
        m_ref[...] = jnp.full(m_ref.shape, -jnp.inf, F32)
        l_ref[...] = jnp.zeros(l_ref.shape, F32)
        acc_ref[...] = jnp.zeros(acc_ref.shape, F32)

    pltpu.make_async_copy(ckv_buf.at[slot], ckv_buf.at[slot], sem.at[slot, 0]).wait()
    pltpu.make_async_copy(kpe_buf.at[slot], kpe_buf.at[slot], sem.at[slot, 1]).wait()

    q = q_ref[0]
    nt = (((1,), (1,)), ((), ()))

    def partial_softmax(s, v):
        m_blk = jnp.max(s, axis=1, keepdims=True)
        p = jnp.exp(s - m_blk)
        return m_blk, jnp.sum(p, axis=1, keepdims=True), jnp.dot(p.astype(BF16), v, preferred_element_type=F32)

    def merge(parts):
        m_old = m_ref[...]
        m_new = m_old
        for m_blk, _, _ in parts:
            m_new = jnp.maximum(m_new, m_blk)
        alpha = jnp.exp(m_old - m_new)
        l_new = alpha * l_ref[...]
        acc = _lane_tile(alpha, KV_LORA // LANES) * acc_ref[...]
        for m_blk, l_blk, o_blk in parts:
            w = jnp.exp(m_blk - m_new)
            l_new = l_new + w * l_blk
            acc = acc + _lane_tile(w, KV_LORA // LANES) * o_blk
        m_ref[...] = m_new
        l_ref[...] = l_new
        acc_ref[...] = acc

    sub_pages = pc // SAMPLE_KEY_BLOCKS
    sub_keys = sub_pages * PAGE_SIZE
    parts = []
    for sb in range(SAMPLE_KEY_BLOCKS):
        ck = ckv_buf[slot, sb * sub_keys:(sb + 1) * sub_keys, :].astype(BF16)
        kr_t = jnp.concatenate([kpe_buf[slot, sb * sub_pages + p] for p in range(sub_pages)], axis=1)
        kr_t = kr_t.astype(BF16)
        s = (lax.dot_general(q[:, :KV_LORA], ck, nt, preferred_element_type=F32)
             + jnp.dot(q[:, KV_LORA:KV_LORA + QK_ROPE], kr_t, preferred_element_type=F32)) * SM_SCALE
        parts.append(partial_softmax(s, ck))
    merge(parts)

    @pl.when(c == n_chunks - 1)
    def _():
        kn = knew_ref[0]
        sn = lax.dot_general(q, kn, nt, preferred_element_type=F32) * SM_SCALE
        qi = lax.broadcasted_iota(jnp.int32, (rows, NEW_KEYS), 0) // N_HEADS
        ki = lax.broadcasted_iota(jnp.int32, (rows, NEW_KEYS), 1)
        sn = jnp.where(ki <= qi, sn, -jnp.inf)
        merge([partial_softmax(sn, kn[:, :KV_LORA])])
        o = (acc_ref[...] / _lane_tile(l_ref[...], KV_LORA // LANES)).astype(BF16)
        full = jnp.dot(o, wuv_ref[0], preferred_element_type=F32)
        head = lax.broadcasted_iota(jnp.int32, (rows, W_ATTN), 0) % N_HEADS
        col_head = lax.broadcasted_iota(jnp.int32, (rows, W_ATTN), 1) // V_HEAD
        own = jnp.where(head == col_head, full, 0.0)
        o_ref[0] = jnp.concatenate(
            [jnp.sum(own[tt * N_HEADS:(tt + 1) * N_HEADS, :], axis=0, keepdims=True) for tt in range(t)],
            axis=0).astype(o_ref.dtype)


def sample_attention(page_table, q3, knew3, cache_ckv, cache_kpe_t, wuv_flat, *, layer, pc):
    db, n_pages = page_table.shape
    rows = q3.shape[1]
    t = rows // N_HEADS
    assert n_pages % pc == 0 and pc % SAMPLE_KEY_BLOCKS == 0 and t <= NEW_KEYS
    n_chunks = n_pages // pc
    n_keys = pc * PAGE_SIZE
    grid_spec = pltpu.PrefetchScalarGridSpec(
        num_scalar_prefetch=1,
        grid=(db, n_chunks),
        in_specs=[pl.BlockSpec((1, rows, QCAT), lambda b, c, pt: (b, 0, 0)),
                  pl.BlockSpec((1, NEW_KEYS, QCAT), lambda b, c, pt: (b, 0, 0)),
                  pl.BlockSpec(memory_space=pl.ANY),
                  pl.BlockSpec(memory_space=pl.ANY),
                  pl.BlockSpec((1, KV_LORA, W_ATTN), lambda b, c, pt: (layer, 0, 0))],
        out_specs=pl.BlockSpec((1, t, W_ATTN), lambda b, c, pt: (b, 0, 0)),
        scratch_shapes=[pltpu.VMEM((2, n_keys, KV_LORA), F32),
                        pltpu.VMEM((2, pc, QK_ROPE, PAGE_SIZE), F32),
                        pltpu.SemaphoreType.DMA((2, 2)),
                        pltpu.VMEM((rows, LANES), F32),
                        pltpu.VMEM((rows, LANES), F32),
                        pltpu.VMEM((rows, KV_LORA), F32)])
    return pl.pallas_call(
        functools.partial(_sample_attn_kernel, layer=layer, n_chunks=n_chunks, pc=pc, t=t),
        out_shape=jax.ShapeDtypeStruct((db, t, W_ATTN), BF16),
        grid_spec=grid_spec,
        compiler_params=_cparams(("arbitrary", "arbitrary")),
        name="sample_attn",
    )(page_table, q3, knew3, cache_ckv, cache_kpe_t, wuv_flat)


SSM_TC = 8
SSM_TILE_GROUPS = LANES // SSM_GROUP
SSM_TILES = W_SSM // LANES
SSM_HALF = SSM_TILE_GROUPS * SSM_STATE
SSM_COLS = 2 * SSM_HALF


def _ssm_expand(small, e_ref, row_div, col_div):
    full = jnp.dot(small, e_ref[...], preferred_element_type=F32)
    rg = (lax.broadcasted_iota(jnp.int32, full.shape, 0) // row_div) % SSM_TILE_GROUPS
    cg = (lax.broadcasted_iota(jnp.int32, full.shape, 1) // col_div) % SSM_TILE_GROUPS
    return jnp.where(rg == cg, full, 0.0).astype(BF16)


def _ssm_kernel(u_ref, toep_ref, wsum_ref, wout_ref, etm_ref, erp_ref, a_ref, h0_ref, y_ref, hfin_ref,
                s_scr, hp_scr, *, nseq, nk, steps):
    r = nseq * nk
    kw = steps * LANES
    u = jnp.concatenate([u_ref[pl.ds(s, r, stride=steps), :].astype(BF16) for s in range(steps)],
                        axis=1)
    toep = _ssm_expand(toep_ref[0, 0, :kw, :], etm_ref, SSM_GROUP, SSM_GROUP)
    wsum = _ssm_expand(wsum_ref[0, 0, :kw, :], erp_ref, SSM_GROUP, SSM_STATE)
    wout = _ssm_expand(wout_ref[0, 0], etm_ref, SSM_STATE, SSM_GROUP)
    sk_all = jnp.dot(u, wsum, preferred_element_type=F32)
    a = a_ref[0, 0]
    a_re = a[0:1, :]
    a_im = a[1:2, :]
    h0 = h0_ref[...]
    n_slab = SSM_COLS // LANES
    if nk == 1:
        hp = h0
        h_re, h_im = h0[:, :SSM_HALF], h0[:, SSM_HALF:]
        hfin_ref[...] = jnp.concatenate(
            [a_re * h_re - a_im * h_im + sk_all[:, :SSM_HALF],
             a_re * h_im + a_im * h_re + sk_all[:, SSM_HALF:]], axis=1)
    else:
        for j in range(n_slab):
            s_scr[j] = sk_all[:, j * LANES:(j + 1) * LANES]
        a_re_b = jnp.broadcast_to(a_re, (nseq, SSM_HALF))
        a_im_b = jnp.broadcast_to(a_im, (nseq, SSM_HALF))

        def body(k, h):
            for j in range(n_slab):
                hp_scr[j, pl.ds(k, nseq, stride=nk), :] = h[:, j * LANES:(j + 1) * LANES]
            sk = jnp.concatenate([s_scr[j, pl.ds(k, nseq, stride=nk), :] for j in range(n_slab)], axis=1)
            h_re, h_im = h[:, :SSM_HALF], h[:, SSM_HALF:]
            return jnp.concatenate(
                [a_re_b * h_re - a_im_b * h_im + sk[:, :SSM_HALF],
                 a_re_b * h_im + a_im_b * h_re + sk[:, SSM_HALF:]], axis=1)

        hfin_ref[...] = lax.fori_loop(0, nk, body, h0)
        hp = jnp.concatenate([hp_scr[j] for j in range(n_slab)], axis=1)
    y = (jnp.dot(u, toep, preferred_element_type=F32)
         + jnp.dot(hp.astype(BF16), wout, preferred_element_type=F32))
    for tt in range(steps):
        y_ref[pl.ds(tt, r, stride=steps), :] = y[:, tt * LANES:(tt + 1) * LANES]


def ssm_scan(z, toep, wsum, wout, e_tm, e_rp, a, h0, *, layer, row_block, nseq, nk, steps):
    rows = nseq * nk * steps
    r = nseq * nk
    w_spec = pl.BlockSpec((1, 1, SSM_COLS, LANES), lambda j: (layer, j, 0, 0))
    e_spec = pl.BlockSpec((LANES, SSM_COLS), lambda j: (0, 0))
    return pl.pallas_call(
        functools.partial(_ssm_kernel, nseq=nseq, nk=nk, steps=steps),
        out_shape=(jax.ShapeDtypeStruct((rows, W_SSM), F32),
                   jax.ShapeDtypeStruct((nseq, SSM_TILES * SSM_COLS), F32)),
        grid=(SSM_TILES,),
        in_specs=[pl.BlockSpec((rows, LANES), lambda j: (row_block, Z_SSM // LANES + j)),
                  w_spec, w_spec, w_spec, e_spec, e_spec,
                  pl.BlockSpec((1, 1, SUBLANES, SSM_HALF), lambda j: (layer, j, 0, 0)),
                  pl.BlockSpec((nseq, SSM_COLS), lambda j: (0, j))],
        out_specs=(pl.BlockSpec((rows, LANES), lambda j: (0, j)),
                   pl.BlockSpec((nseq, SSM_COLS), lambda j: (0, j))),
        scratch_shapes=[pltpu.VMEM((SSM_COLS // LANES, r, LANES), F32),
                        pltpu.VMEM((SSM_COLS // LANES, r, LANES), F32)],
        compiler_params=_cparams(("parallel",)),
        name="ssm_scan",
    )(z, toep, wsum, wout, e_tm, e_rp, a, h0)


def _gelu_tanh(x):
    return 0.5 * x * (1.0 + jnp.tanh(np.sqrt(2.0 / np.pi).astype(np.float32)
                                     * (x + np.float32(0.044715) * (x * x * x))))


def _ssm_post_kernel(y_ref, u_ref, d_ref, wg_ref, o_ref):
    y = y_ref[...] + d_ref[...] * u_ref[...]
    zz = _gelu_tanh(y)
    gate = jax.nn.sigmoid(jnp.dot(zz.astype(BF16), wg_ref[...], preferred_element_type=F32))
    o_ref[...] = (zz * gate).astype(o_ref.dtype)


def ssm_post(y_raw, z, d_skip, w_glu, *, tm):
    m = y_raw.shape[0]
    assert m % tm == 0
    return pl.pallas_call(
        _ssm_post_kernel,
        out_shape=jax.ShapeDtypeStruct((m, W_SSM), BF16),
        grid=(m // tm,),
        in_specs=[pl.BlockSpec((tm, W_SSM), lambda i: (i, 0)),
                  pl.BlockSpec((tm, W_SSM), lambda i: (i, Z_SSM // W_SSM)),
                  pl.BlockSpec((1, W_SSM), lambda i: (0, 0)),
                  pl.BlockSpec((W_SSM, W_SSM), lambda i: (0, 0))],
        out_specs=pl.BlockSpec((tm, W_SSM), lambda i: (i, 0)),
        compiler_params=_cparams(("parallel",)),
        name="ssm_post",
    )(y_raw, z, d_skip, w_glu)


POOL_HALO = 2 * SUBLANES


def _pool_mix(diff, pw_ref, ps_ref):
    outs = [jnp.dot(diff[:, g * POOL_GW:(g + 1) * POOL_GW].astype(BF16), pw_ref[g],
                    preferred_element_type=F32) for g in range(len(POOL_WINDOWS))]
    return jnp.concatenate(outs, axis=1) * ps_ref[...]


def _pool_prompt_kernel(u_ref, pw_ref, ps_ref, o_ref, ext):
    seq = u_ref.shape[0]
    ext[0:POOL_HALO, :] = jnp.zeros((POOL_HALO, W_POOL), F32)
    ext[POOL_HALO:, :] = u_ref[...]
    pos1 = lax.broadcasted_iota(jnp.int32, (seq, POOL_GW), 0) + 1
    means = []
    for g, w in enumerate(POOL_WINDOWS):
        tot = ext[POOL_HALO:POOL_HALO + seq, g * POOL_GW:(g + 1) * POOL_GW]
        for k in range(1, w):
            tot = tot + ext[POOL_HALO - k:POOL_HALO - k + seq, g * POOL_GW:(g + 1) * POOL_GW]
        means.append(tot / jnp.minimum(pos1, w).astype(F32))
    diff = jnp.concatenate(means, axis=1) - u_ref[...]
    o_ref[...] = _pool_mix(diff, pw_ref, ps_ref).astype(o_ref.dtype)


def pool_prompt(z, pool_w, pool_scale, *, batch, seq):
    return pl.pallas_call(
        _pool_prompt_kernel,
        out_shape=jax.ShapeDtypeStruct((batch * seq, W_POOL), BF16),
        grid=(batch,),
        in_specs=[pl.BlockSpec((seq, W_POOL), lambda b: (b, Z_POOL // W_POOL)),
                  pl.BlockSpec(pool_w.shape, lambda b: (0, 0, 0)),
                  pl.BlockSpec((1, W_POOL), lambda b: (0, 0))],
        out_specs=pl.BlockSpec((seq, W_POOL), lambda b: (b, 0)),
        scratch_shapes=[pltpu.VMEM((POOL_HALO + seq, W_POOL), F32)],
        compiler_params=_cparams(("parallel",)),
        name="pool_prompt",
    )(z, pool_w, pool_scale)


def _pool_sample_kernel(x_ref, cnt_ref, pw_ref, ps_ref, o_ref, *, t):
    for tt in range(t):
        means = []
        for g, w in enumerate(POOL_WINDOWS):
            sl = slice(g * POOL_GW, (g + 1) * POOL_GW)
            tot = x_ref[POOL_PAD + tt][:, sl]
            for k in range(1, w):
                tot = tot + x_ref[POOL_PAD + tt - k][:, sl]
            means.append(tot / cnt_ref[tt][:, sl])
        diff = jnp.concatenate(means, axis=1) - x_ref[POOL_PAD + tt]
        o_ref[tt] = _pool_mix(diff, pw_ref, ps_ref).astype(o_ref.dtype)


def pool_sample(x_tm, cnt, pool_w, pool_scale):
    rows, nseq, _ = x_tm.shape
    t = rows - POOL_PAD
    return pl.pallas_call(
        functools.partial(_pool_sample_kernel, t=t),
        out_shape=jax.ShapeDtypeStruct((t, nseq, W_POOL), BF16),
        grid=(1,),
        in_specs=[pl.BlockSpec(x_tm.shape, lambda i: (0, 0, 0)),
                  pl.BlockSpec(cnt.shape, lambda i: (0, 0, 0)),
                  pl.BlockSpec(pool_w.shape, lambda i: (0, 0, 0)),
                  pl.BlockSpec((1, W_POOL), lambda i: (0, 0))],
        out_specs=pl.BlockSpec((t, nseq, W_POOL), lambda i: (0, 0, 0)),
        compiler_params=_cparams(("arbitrary",)),
        name="pool_sample",
    )(x_tm, cnt, pool_w, pool_scale)


def _merge_kernel(oa_ref, ys_ref, yp_ref, g0_ref, g1_ref, g2_ref, wa_ref, ws_ref, wp_ref, m_ref):
    m = (g0_ref[...] * jnp.dot(oa_ref[...], wa_ref[0], preferred_element_type=F32)
         + g1_ref[...] * jnp.dot(ys_ref[...], ws_ref[0], preferred_element_type=F32)
         + g2_ref[...] * jnp.dot(yp_ref[...], wp_ref[0], preferred_element_type=F32))
    m_ref[...] = m.astype(m_ref.dtype)


def branch_merge(o_attn, y_ssm, y_pool, z, w_branch, *, layer, tm, tn):
    m = o_attn.shape[0]
    assert m % tm == 0 and D_MODEL % tn == 0
    nj = D_MODEL // tn
    return pl.pallas_call(
        _merge_kernel,
        out_shape=jax.ShapeDtypeStruct((m, D_MODEL), BF16),
        grid=(m // tm, nj),
        in_specs=[pl.BlockSpec((tm, W_ATTN), lambda i, j: (i, 0)),
                  pl.BlockSpec((tm, W_SSM), lambda i, j: (i, 0)),
                  pl.BlockSpec((tm, W_POOL), lambda i, j: (i, 0)),
                  pl.BlockSpec((tm, tn), lambda i, j: (i, j)),
                  pl.BlockSpec((tm, tn), lambda i, j: (i, nj + j)),
                  pl.BlockSpec((tm, tn), lambda i, j: (i, 2 * nj + j)),
                  pl.BlockSpec((1, W_ATTN, tn), lambda i, j: (layer, 0, j)),
                  pl.BlockSpec((1, W_SSM, tn), lambda i, j: (layer, W_ATTN // W_SSM, j)),
                  pl.BlockSpec((1, W_POOL, tn), lambda i, j: (layer, (W_ATTN + W_SSM) // W_POOL, j))],
        out_specs=pl.BlockSpec((tm, tn), lambda i, j: (i, j)),
        compiler_params=_cparams(("parallel", "arbitrary")),
        name="branch_merge",
    )(o_attn, y_ssm, y_pool, z, z, z, w_branch, w_branch, w_branch)


ROUTE_LANES = LANES


def _route(hf, wr_ref):
    wr = wr_ref[0]
    best1 = jnp.sum(hf * wr[0:1, :], axis=1, keepdims=True)
    idx1 = jnp.zeros(best1.shape, jnp.int32)
    best2 = jnp.full(best1.shape, -jnp.inf, F32)
    idx2 = jnp.zeros(best1.shape, jnp.int32)
    for e in range(1, N_EXPERTS):
        le = jnp.sum(hf * wr[e:e + 1, :], axis=1, keepdims=True)
        gt1 = le > best1
        gt2 = le > best2
        best2 = jnp.where(gt1, best1, jnp.where(gt2, le, best2))
        idx2 = jnp.where(gt1, idx1, jnp.where(gt2, e, idx2))
        best1 = jnp.where(gt1, le, best1)
        idx1 = jnp.where(gt1, e, idx1)
    ex = jnp.exp(best2 - best1)
    g1 = 1.0 / (1.0 + ex)
    g2 = ex / (1.0 + ex)
    lane = lax.broadcasted_iota(jnp.int32, (hf.shape[0], ROUTE_LANES), 1)
    return jnp.where(lane == 0, idx1.astype(F32),
                     jnp.where(lane == 1, idx2.astype(F32),
                               jnp.where(lane == 2, g1, jnp.where(lane == 3, g2, 0.0))))


def _out_proj_kernel(m_ref, w_ref, x_ref, g_ref, *rest, route):
    if route:
        wr_ref, xn_ref, hp_ref, rt_ref = rest
    else:
        xn_ref, hp_ref = rest
    xn = x_ref[...] + jnp.dot(m_ref[...], w_ref[0], preferred_element_type=F32)
    xn_ref[...] = xn
    hf = _rms(xn, g_ref[...])
    if route:
        hp_ref[...] = _pack_rows(hf)
        rt_ref[...] = _route(hf, wr_ref)
    else:
        hp_ref[...] = hf.astype(hp_ref.dtype)


def out_proj(mm, w_out, x, gain, w_router_t, *, layer, moe_layer, tm):
    m = x.shape[0]
    assert m % tm == 0
    route = moe_layer is not None
    row = lambda i: (i, 0)
    in_specs = [pl.BlockSpec((tm, D_MODEL), row),
                pl.BlockSpec((1, D_MODEL, D_MODEL), lambda i: (layer, 0, 0)),
                pl.BlockSpec((tm, D_MODEL), row),
                pl.BlockSpec((1, D_MODEL), lambda i: (0, 0))]
    args = [mm, w_out, x, gain]
    pw, pdt = (_packed_width(D_MODEL), U32) if route else (D_MODEL, BF16)
    out_shape = [jax.ShapeDtypeStruct((m, D_MODEL), F32), jax.ShapeDtypeStruct((m, pw), pdt)]
    out_specs = [pl.BlockSpec((tm, D_MODEL), row), pl.BlockSpec((tm, pw), row)]
    if route:
        in_specs.append(pl.BlockSpec((1, N_EXPERTS, D_MODEL), lambda i: (moe_layer, 0, 0)))
        args.append(w_router_t)
        out_shape.append(jax.ShapeDtypeStruct((m, ROUTE_LANES), F32))
        out_specs.append(pl.BlockSpec((tm, ROUTE_LANES), row))
    return pl.pallas_call(
        functools.partial(_out_proj_kernel, route=route),
        out_shape=tuple(out_shape),
        grid=(m // tm,),
        in_specs=in_specs,
        out_specs=tuple(out_specs),
        compiler_params=_cparams(("parallel",)),
        name="out_proj",
    )(*args)


def _tile_row(i, nu):
    return jnp.minimum(i, nu[0] - 1)


def _ffn_up_kernel(te_ref, nu_ref, x_ref, w1_ref, w3_ref, g_ref, w1b, w3b):
    i = pl.program_id(1)
    used = i < nu_ref[0]
    cur = te_ref[_tile_row(i, nu_ref)]
    prev = te_ref[_tile_row(jnp.maximum(i - 1, 0), nu_ref)]

    @pl.when(used & ((i == 0) | (cur != prev)))
    def _():
        w1b[...] = w1_ref[0, 0].astype(BF16)
        w3b[...] = w3_ref[0, 0].astype(BF16)

    @pl.when(used)
    def _():
        x = x_ref[...]
        for c in range(g_ref.shape[1] // IN_SUB):
            sl = slice(c * IN_SUB, (c + 1) * IN_SUB)
            a = jnp.dot(x, w1b[:, sl], preferred_element_type=F32)
            b = jnp.dot(x, w3b[:, sl], preferred_element_type=F32)
            g_ref[:, sl] = (a * jax.nn.sigmoid(a) * b).astype(g_ref.dtype)

    @pl.when(jnp.logical_not(used))
    def _():
        g_ref[...] = jnp.zeros(g_ref.shape, g_ref.dtype)


def ffn_up(tile_expert, n_used, xs, w1, w3, *, layer, tm, tf):
    r, xw = xs.shape
    assert r % tm == 0 and D_FF % tf == 0
    w_spec = pl.BlockSpec((1, 1, D_MODEL, tf),
                          lambda j, i, te, nu: (layer, te[_tile_row(i, nu)], 0, j))
    grid_spec = pltpu.PrefetchScalarGridSpec(
        num_scalar_prefetch=2,
        grid=(D_FF // tf, r // tm),
        in_specs=[pl.BlockSpec((tm, xw), lambda j, i, te, nu: (_tile_row(i, nu), 0)), w_spec, w_spec],
        out_specs=pl.BlockSpec((tm, tf), lambda j, i, te, nu: (i, j)),
        scratch_shapes=[pltpu.VMEM((D_MODEL, tf), BF16), pltpu.VMEM((D_MODEL, tf), BF16)])
    return pl.pallas_call(
        _ffn_up_kernel,
        out_shape=jax.ShapeDtypeStruct((r, D_FF), BF16),
        grid_spec=grid_spec,
        compiler_params=_cparams(("arbitrary", "arbitrary")),
        name="ffn_up",
    )(tile_expert, n_used, xs, w1, w3)


def _ffn_down_kernel(te_ref, nu_ref, g_ref, w2_ref, *rest, residual):
    i = pl.program_id(1)
    if residual:
        x_ref, y_ref = rest
    else:
        (y_ref,) = rest

    @pl.when(i < nu_ref[0])
    def _():
        y = jnp.dot(g_ref[...], w2_ref[0, 0], preferred_element_type=F32)
        y_ref[...] = x_ref[...] + y if residual else y

    @pl.when(i >= nu_ref[0])
    def _():
        y_ref[...] = jnp.zeros(y_ref.shape, y_ref.dtype)


def ffn_down(tile_expert, n_used, g, w2, x_res, *, layer, tm, tn):
    r = g.shape[0]
    assert r % tm == 0 and D_MODEL % tn == 0
    residual = x_res is not None
    in_specs = [pl.BlockSpec((tm, D_FF), lambda j, i, te, nu: (_tile_row(i, nu), 0)),
                pl.BlockSpec((1, 1, D_FF, tn), lambda j, i, te, nu: (layer, te[_tile_row(i, nu)], 0, j))]
    args = [tile_expert, n_used, g, w2]
    if residual:
        in_specs.append(pl.BlockSpec((tm, tn), lambda j, i, te, nu: (_tile_row(i, nu), j)))
        args.append(x_res)
    grid_spec = pltpu.PrefetchScalarGridSpec(
        num_scalar_prefetch=2,
        grid=(D_MODEL // tn, r // tm),
        in_specs=in_specs,
        out_specs=pl.BlockSpec((tm, tn), lambda j, i, te, nu: (i, j)))
    return pl.pallas_call(
        functools.partial(_ffn_down_kernel, residual=residual),
        out_shape=jax.ShapeDtypeStruct((r, D_MODEL), F32),
        grid_spec=grid_spec,
        compiler_params=_cparams(("arbitrary", "arbitrary")),
        name="ffn_down",
    )(*args)


DMA_UNROLL = 8
SCATTER_BATCH = 32


def _dispatch_kernel(dest_ref, nu_ref, hf_hbm, o_ref, row_tok, buf, sem, *, n_assign, tm):
    i = pl.program_id(0)
    slot = i % 2
    nu = nu_ref[0]

    @pl.when(i == 0)
    def _():
        def clear(r, carry):
            row_tok[r] = 0
            return carry

        def scatter(blk, carry):
            base = blk * SCATTER_BATCH
            tok0 = blk * (SCATTER_BATCH // TOP_K)
            ds = [dest_ref[base + k] for k in range(SCATTER_BATCH)]
            for k in range(SCATTER_BATCH):
                row_tok[ds[k]] = tok0 + k // TOP_K
            return carry

        lax.fori_loop(0, row_tok.shape[0], clear, 0, unroll=DMA_UNROLL)
        lax.fori_loop(0, n_assign // SCATTER_BATCH, scatter, 0)

    def row_copy(tok, sl, r):
        return pltpu.make_async_copy(hf_hbm.at[pl.ds(tok, 1), :], buf.at[sl, pl.ds(r, 1), :], sem.at[sl])

    def start_tile(tile, sl):
        def issue(rp, carry):
            for pr in range(2):
                r = rp * 2 + pr
                row_copy(row_tok[tile * tm + r], sl, r).start(priority=pr)
            return carry

        lax.fori_loop(0, tm // 2, issue, 0, unroll=DMA_UNROLL // 2)

    @pl.when((i == 0) & (nu > 0))
    def _():
        start_tile(i, slot)

    @pl.when(i + 1 < nu)
    def _():
        start_tile(i + 1, 1 - slot)

    @pl.when(i < nu)
    def _():
        pltpu.make_async_copy(buf.at[slot], buf.at[slot], sem.at[slot]).wait()
        o_ref[...] = _unpack_rows(buf[slot])

    @pl.when(i >= nu)
    def _():
        o_ref[...] = jnp.zeros(o_ref.shape, o_ref.dtype)


def dispatch(dest, n_used, hf, *, n_tiles, tm):
    n_assign = dest.shape[0]
    assert n_assign % SCATTER_BATCH == 0
    width = hf.shape[1]
    grid_spec = pltpu.PrefetchScalarGridSpec(
        num_scalar_prefetch=2,
        grid=(n_tiles,),
        in_specs=[pl.BlockSpec(memory_space=pl.ANY)],
        out_specs=pl.BlockSpec((tm, D_MODEL), lambda i, d, nu: (i, 0)),
        scratch_shapes=[pltpu.SMEM((n_tiles * tm,), jnp.int32),
                        pltpu.VMEM((2, tm, width), hf.dtype),
                        pltpu.SemaphoreType.DMA((2,))])
    return pl.pallas_call(
        functools.partial(_dispatch_kernel, n_assign=n_assign, tm=tm),
        out_shape=jax.ShapeDtypeStruct((n_tiles * tm, D_MODEL), BF16),
        grid_spec=grid_spec,
        compiler_params=_cparams(("arbitrary",)),
        name="moe_dispatch",
    )(dest, n_used, hf)


def _combine_kernel(dest_ref, x_ref, rt_ref, y_hbm, o_ref, buf, sem, *, tn):
    i = pl.program_id(0)
    n = pl.num_programs(0)
    slot = i % 2

    def row_copy(d, sl, k, rr):
        return pltpu.make_async_copy(y_hbm.at[pl.ds(d, 1), :], buf.at[sl, k, pl.ds(rr, 1), :],
                                     sem.at[sl])

    def start_tile(tile, sl):
        def issue(rr, carry):
            for k in range(TOP_K):
                row_copy(dest_ref[(tile * tn + rr) * TOP_K + k], sl, k, rr).start(priority=k)
            return carry

        lax.fori_loop(0, tn, issue, 0, unroll=DMA_UNROLL)

    @pl.when(i == 0)
    def _():
        start_tile(i, slot)

    @pl.when(i + 1 < n)
    def _():
        start_tile(i + 1, 1 - slot)

    pltpu.make_async_copy(buf.at[slot], buf.at[slot], sem.at[slot]).wait()

    rt = rt_ref[...]
    o_ref[...] = (x_ref[...] + (rt[:, 2:3] * buf[slot, 0] + rt[:, 3:4] * buf[slot, 1]))


def combine(dest, x, route, y, *, tn):
    m = x.shape[0]
    assert m % tn == 0
    grid_spec = pltpu.PrefetchScalarGridSpec(
        num_scalar_prefetch=1,
        grid=(m // tn,),
        in_specs=[pl.BlockSpec((tn, D_MODEL), lambda i, d: (i, 0)),
                  pl.BlockSpec((tn, ROUTE_LANES), lambda i, d: (i, 0)),
                  pl.BlockSpec(memory_space=pl.ANY)],
        out_specs=pl.BlockSpec((tn, D_MODEL), lambda i, d: (i, 0)),
        scratch_shapes=[pltpu.VMEM((2, TOP_K, tn, D_MODEL), F32),
                        pltpu.SemaphoreType.DMA((2,))])
    return pl.pallas_call(
        functools.partial(_combine_kernel, tn=tn),
        out_shape=jax.ShapeDtypeStruct((m, D_MODEL), F32),
        grid_spec=grid_spec,
        compiler_params=_cparams(("arbitrary",)),
        name="moe_combine",
    )(dest, x, route, y)


def _norm_kernel(x_ref, g_ref, o_ref):
    o_ref[...] = _rms(x_ref[...], g_ref[...])


def final_norm(x, gain, *, tm):
    m = x.shape[0]
    assert m % tm == 0
    return pl.pallas_call(
        _norm_kernel,
        out_shape=jax.ShapeDtypeStruct((m, D_MODEL), F32),
        grid=(m // tm,),
        in_specs=[pl.BlockSpec((tm, D_MODEL), lambda i: (i, 0)),
                  pl.BlockSpec((1, D_MODEL), lambda i: (0, 0))],
        out_specs=pl.BlockSpec((tm, D_MODEL), lambda i: (i, 0)),
        compiler_params=_cparams(("parallel",)),
        name="final_norm",
    )(x, gain)


def _rot_half(w):
    half = QK_ROPE // 2
    return jnp.concatenate([-w[..., half:], w[..., :half]], axis=-1)


def _lane_pad(w):
    return jnp.pad(w, [(0, 0)] * (w.ndim - 1) + [(0, ROPE_LANES - w.shape[-1])])


def _prep_w_in(w_in):
    o1 = Q_LORA
    o2 = o1 + KV_LORA
    o3 = o2 + QK_ROPE
    o4 = o3 + W_SSM
    o5 = o4 + W_POOL
    kpe = w_in[..., o2:o3]
    parts = [w_in[..., o5:], w_in[..., :o1], w_in[..., o3:o4], w_in[..., o4:o5], w_in[..., o1:o2],
             _lane_pad(kpe), _lane_pad(_rot_half(kpe))]
    return jnp.concatenate(parts, axis=-1).astype(BF16)


def _prep_w_q(w_q_up):
    depth = w_q_up.shape[0]
    w = w_q_up.reshape(depth, Q_LORA, N_HEADS, QK_NOPE + QK_ROPE)
    nope = w[..., :QK_NOPE].reshape(depth, Q_LORA, N_HEADS * QK_NOPE)
    rope = w[..., QK_NOPE:]
    rope_p = _lane_pad(rope).reshape(depth, Q_LORA, N_HEADS * ROPE_LANES)
    rot_p = _lane_pad(_rot_half(rope)).reshape(depth, Q_LORA, N_HEADS * ROPE_LANES)
    return jnp.concatenate([nope, rope_p, rot_p], axis=-1).astype(BF16)


def _block_diag_heads(w):
    depth, _, r, c = w.shape
    ng = N_HEADS // HEAD_GROUP
    w5 = w.reshape(depth, ng, HEAD_GROUP, r, c)
    eye = jnp.eye(HEAD_GROUP, dtype=w.dtype)
    out = w5[:, :, :, :, None, :] * eye[None, None, :, None, :, None]
    return out.reshape(depth, ng, HEAD_GROUP * r, HEAD_GROUP * c)


def _rope_tables(pos):
    half = QK_ROPE // 2
    freqs = ROPE_BASE ** (-jnp.arange(half, dtype=F32) / half)
    ang = pos.astype(F32)[:, None] * freqs[None, :]
    cos, sin = jnp.cos(ang), jnp.sin(ang)
    return (_lane_pad(jnp.concatenate([cos, cos], axis=-1)),
            _lane_pad(jnp.concatenate([sin, sin], axis=-1)))


def _prep_ssm(lam_re, lam_im, log_dt, b_re, b_im, c_re, c_im, steps_list):
    hi = lax.Precision.HIGHEST
    depth = lam_re.shape[0]
    g, p, n, tc = SSM_GROUPS, SSM_STATE, SSM_GROUP, SSM_TC
    lam = lax.complex(lam_re.astype(F32), lam_im.astype(F32))
    dt = jnp.exp(log_dt.astype(F32))[..., None]
    ldt = lam * dt
    pw = jnp.exp(ldt[..., None] * jnp.arange(tc + 1, dtype=F32))
    lam_bar = pw[..., 1]
    b_bar = ((lam_bar - 1.0) / lam)[..., None] * lax.complex(b_re.astype(F32), b_im.astype(F32))
    c_c = lax.complex(c_re.astype(F32), c_im.astype(F32))
    kt = jnp.real(jnp.einsum("dgmp,dgpt,dgpn->dgtmn", c_c, pw[..., :tc], b_bar, precision=hi))
    s_idx = jnp.arange(tc)[:, None]
    t_idx = jnp.arange(tc)[None, :]
    lag = jnp.clip(t_idx - s_idx, 0, tc - 1)
    toep = jnp.where((t_idx >= s_idx)[None, None, :, :, None, None], kt[:, :, lag], 0.0)
    toep = toep.reshape(depth, SSM_TILES, SSM_TILE_GROUPS, tc, tc, n, n)
    toep = toep.transpose(0, 1, 3, 2, 6, 4, 5)
    toep = toep.astype(BF16).reshape(depth, SSM_TILES, SSM_COLS, LANES)

    def wsum_for(steps):
        ex = jnp.clip(steps - 1 - jnp.arange(tc), 0, tc)
        v = pw[..., ex][..., None] * b_bar[:, :, :, None, :]
        v = jnp.where((jnp.arange(tc) < steps)[None, None, None, :, None], v, 0.0)
        v = v.transpose(0, 1, 3, 4, 2)
        w = jnp.stack([jnp.real(v), jnp.imag(v)], axis=-2)
        w = w.reshape(depth, SSM_TILES, SSM_TILE_GROUPS, tc, n, 2, p)
        w = w.transpose(0, 1, 3, 2, 4, 5, 6)
        return w.astype(BF16).reshape(depth, SSM_TILES, SSM_COLS, LANES)

    def a_for(steps):
        a = pw[..., steps].reshape(depth, SSM_TILES, SSM_HALF)
        a = jnp.stack([jnp.real(a), jnp.imag(a)], axis=2)
        return jnp.pad(a, ((0, 0), (0, 0), (0, SUBLANES - 2), (0, 0)))

    ac = c_c.transpose(0, 1, 3, 2)[:, :, :, None, :] * pw[..., 1:tc + 1][..., None]
    wout = jnp.stack([jnp.real(ac), -jnp.imag(ac)], axis=2)
    wout = wout.reshape(depth, SSM_TILES, SSM_TILE_GROUPS, 2, p, tc, n)
    wout = wout.transpose(0, 1, 3, 2, 4, 5, 6)
    wout = wout.astype(BF16).reshape(depth, SSM_TILES, SSM_COLS, LANES)
    e_tm = np.zeros((tc, n, tc, SSM_TILE_GROUPS, n), np.float32)
    e_rp = np.zeros((2, p, 2, SSM_TILE_GROUPS, p), np.float32)
    for gl in range(SSM_TILE_GROUPS):
        e_tm[np.arange(tc)[:, None], np.arange(n)[None, :], np.arange(tc)[:, None], gl, np.arange(n)[None, :]] = 1.0
        e_rp[np.arange(2)[:, None], np.arange(p)[None, :], np.arange(2)[:, None], gl, np.arange(p)[None, :]] = 1.0
    out = dict(toep=toep, wout=wout,
               e_tm=jnp.asarray(e_tm.reshape(LANES, SSM_COLS), BF16),
               e_rp=jnp.asarray(e_rp.reshape(LANES, SSM_COLS), BF16))
    for steps in steps_list:
        out[("wsum", steps)] = wsum_for(steps)
        out[("a", steps)] = a_for(steps)
    return out


def _state_pack(h):
    nb = h.shape[0]
    h5 = h.astype(F32).reshape(nb, SSM_TILES, SSM_TILE_GROUPS, SSM_STATE, 2)
    return h5.transpose(0, 1, 4, 2, 3).reshape(nb, SSM_TILES * SSM_COLS)


def _state_unpack(hp, dtype):
    nb = hp.shape[0]
    h5 = hp.reshape(nb, SSM_TILES, 2, SSM_TILE_GROUPS, SSM_STATE).transpose(0, 1, 3, 4, 2)
    return h5.reshape(nb, SSM_GROUPS, SSM_STATE, 2).astype(dtype)


def _routing_plan(route, tm, n_tiles):
    flat_e = route[:, :TOP_K].astype(jnp.int32).reshape(-1)
    onehot = (flat_e[:, None] == jnp.arange(N_EXPERTS, dtype=jnp.int32)[None, :]).astype(jnp.int32)
    csum = jnp.cumsum(onehot, axis=0)
    rank = jnp.sum((csum - onehot) * onehot, axis=1)
    counts = csum[-1]
    padded = (counts + tm - 1) // tm * tm
    pad_end = jnp.cumsum(padded)
    pad_start = pad_end - padded
    dest = (jnp.sum(onehot * pad_start[None, :], axis=1) + rank).astype(jnp.int32)
    n_used = (pad_end[-1:] // tm).astype(jnp.int32)
    tile_start = jnp.arange(n_tiles, dtype=jnp.int32) * tm
    tile_expert = jnp.minimum(jnp.sum((tile_start[:, None] >= pad_end[None, :]).astype(jnp.int32), axis=1),
                              N_EXPERTS - 1).astype(jnp.int32)
    return dest, tile_expert, n_used


TM_BIG_OPTIONS = (1088, 544, 272, 136)
TM_MID_OPTIONS = (544, 272, 136)
TN_COMBINE_OPTIONS = (128, 64, 32, 16, 8)
TM_MOE = 512
TN_IN = 2048
TN_MERGE = 512
TF_UP = 512
TN_DOWN = 1024
TQ_ATTN = 256
TK_ATTN = 512
PAGES_PER_CHUNK = 32


def _largest_divisor(n, options):
    for o in options:
        if n % o == 0:
            return o
    raise ValueError(f"no tile in {options} divides {n}")


def kernel(x_prompt, x_sample, cache_ckv, cache_kpe, state_ssm, state_pool, page_table, attn_norm_g, w_in, q_norm_g, w_q_up, kv_norm_g, w_uk, w_uv, ssm_lam_re, ssm_lam_im, ssm_log_dt, ssm_b_re, ssm_b_im, ssm_c_re, ssm_c_im, ssm_d, ssm_w_glu, pool_w, pool_scale, w_branch, w_out, ffn_norm_g, ffn_w1, ffn_w3, ffn_w2, moe_router, moe_w1, moe_w3, moe_w2, final_norm_g):
    bsz, seq, _ = x_prompt.shape
    db, t, _ = x_sample.shape
    depth = w_in.shape[0]
    past = page_table.shape[1] * PAGE_SIZE
    m_p = bsz * seq
    m_s = db * t
    m = m_p + m_s
    assert seq % SSM_TC == 0 and t <= SSM_TC and m_p % m_s == 0

    w_in2 = _prep_w_in(w_in)
    wq2 = _prep_w_q(w_q_up)
    wukbd = _block_diag_heads(w_uk.transpose(0, 2, 3, 1)).astype(BF16)
    wuvbd = _block_diag_heads(w_uv.transpose(0, 2, 1, 3)).astype(BF16)
    wuv_flat = w_uv.reshape(depth, KV_LORA, W_ATTN).astype(BF16)
    ssm_w = _prep_ssm(ssm_lam_re, ssm_lam_im, ssm_log_dt, ssm_b_re, ssm_b_im, ssm_c_re, ssm_c_im,
                      (SSM_TC, t))
    w_glu = ssm_w_glu.astype(BF16)
    pool_wb = pool_w.astype(BF16)
    wb = w_branch.astype(BF16)
    wo = w_out.astype(BF16)
    fw1, fw3 = ffn_w1[:, None], ffn_w3[:, None]
    fw2 = ffn_w2.astype(BF16)[:, None]
    mw2 = moe_w2.astype(BF16)
    w_router_t = moe_router.astype(F32).transpose(0, 2, 1)
    cache_kpe_t = cache_kpe.transpose(0, 1, 3, 2)

    pos = jnp.concatenate([jnp.tile(jnp.arange(seq), bsz), past + jnp.tile(jnp.arange(t), db)])
    cos_t, sin_t = _rope_tables(pos)
    pos_s = past + jnp.arange(t)
    cnt_s = jnp.concatenate(
        [jnp.broadcast_to(jnp.minimum(pos_s + 1, w).astype(F32)[:, None, None], (t, 1, POOL_GW))
         for w in POOL_WINDOWS], axis=-1)

    tm_big = _largest_divisor(m, TM_BIG_OPTIONS)
    tm_mid = _largest_divisor(m, TM_MID_OPTIONS)
    tn_combine = _largest_divisor(m, TN_COMBINE_OPTIONS)
    tk_attn = min(TK_ATTN, seq)
    pages_per_chunk = min(PAGES_PER_CHUNK, page_table.shape[1])
    def dense_tiles(tm):
        return jnp.zeros((m // tm,), jnp.int32), jnp.full((1,), m // tm, jnp.int32)

    te_up, nu_up = dense_tiles(tm_big)
    te_down, nu_down = dense_tiles(tm_mid)
    n_moe_tiles = (m * TOP_K) // TM_MOE + N_EXPERTS

    x = jnp.concatenate([x_prompt.reshape(m_p, D_MODEL), x_sample.reshape(m_s, D_MODEL)], axis=0)
    outs = [[] for _ in range(8)]
    for l in range(depth):
        z = in_proj(x, attn_norm_g[l][None], w_in2, layer=l, tm=tm_mid, tn=TN_IN)
        qcat, c_lat, k_pe, kcat = qkv_post(z, cos_t, sin_t, q_norm_g[l][None], kv_norm_g[l][None],
                                           wq2, wukbd, layer=l, tm=tm_mid)
        oa_p = prompt_attention(qcat, kcat, wuvbd, layer=l, batch=bsz, seq=seq, tq=TQ_ATTN, tk=tk_attn)
        q3 = qcat[m_p:].reshape(db, t * N_HEADS, QCAT)
        knew3 = jnp.pad(kcat[m_p:].reshape(db, t, QCAT), ((0, 0), (0, NEW_KEYS - t), (0, 0)))
        oa_s = sample_attention(page_table, q3, knew3, cache_ckv, cache_kpe_t, wuv_flat,
                                layer=l, pc=pages_per_chunk)
        o_attn = jnp.concatenate([oa_p, oa_s.reshape(m_s, W_ATTN)], axis=0)
        ssm_ops = (ssm_w["toep"], ssm_w["wout"], ssm_w["e_tm"], ssm_w["e_rp"])
        y_p, hfin_p = ssm_scan(z, ssm_ops[0], ssm_w[("wsum", SSM_TC)], *ssm_ops[1:], ssm_w[("a", SSM_TC)],
                               jnp.zeros((bsz, SSM_TILES * SSM_COLS), F32),
                               layer=l, row_block=0, nseq=bsz, nk=seq // SSM_TC, steps=SSM_TC)
        y_s, hfin_s = ssm_scan(z, ssm_ops[0], ssm_w[("wsum", t)], *ssm_ops[1:], ssm_w[("a", t)],
                               _state_pack(state_ssm[l]),
                               layer=l, row_block=m_p // m_s, nseq=db, nk=1, steps=t)
        y_ssm = ssm_post(jnp.concatenate([y_p, y_s], axis=0), z, ssm_d[l][None], w_glu[l], tm=tm_mid)
        yp_p = pool_prompt(z, pool_wb[l], pool_scale[l][None], batch=bsz, seq=seq)
        u_pool = z[:, Z_POOL:Z_POOL + W_POOL]
        u_ext = jnp.concatenate([state_pool[l].astype(F32), u_pool[m_p:].reshape(db, t, W_POOL)], axis=1)
        yp_s = pool_sample(u_ext.transpose(1, 0, 2), cnt_s, pool_wb[l], pool_scale[l][None])
        y_pool = jnp.concatenate([yp_p, yp_s.transpose(1, 0, 2).reshape(m_s, W_POOL)], axis=0)
        mm = branch_merge(o_attn, y_ssm, y_pool, z, wb, layer=l, tm=tm_big, tn=TN_MERGE)
        j = l // 2
        if l % 2 == 0:
            xn, hfp = out_proj(mm, wo, x, ffn_norm_g[l][None], None, layer=l, moe_layer=None, tm=tm_mid)
            g_act = ffn_up(te_up, nu_up, hfp, fw1, fw3, layer=j, tm=tm_big, tf=TF_UP)
            x = ffn_down(te_down, nu_down, g_act, fw2, xn, layer=j, tm=tm_mid, tn=TN_DOWN)
        else:
            xn, hfp, route = out_proj(mm, wo, x, ffn_norm_g[l][None], w_router_t, layer=l, moe_layer=j,
                                      tm=tm_mid)
            dest, tile_expert, n_used = _routing_plan(route, TM_MOE, n_moe_tiles)
            xs = dispatch(dest, n_used, hfp, n_tiles=n_moe_tiles, tm=TM_MOE)
            g_act = ffn_up(tile_expert, n_used, xs, moe_w1, moe_w3, layer=j, tm=TM_MOE, tf=TF_UP)
            y_sorted = ffn_down(tile_expert, n_used, g_act, mw2, None, layer=j, tm=TM_MOE, tn=TN_DOWN)
            x = combine(dest, xn, route, y_sorted, tn=tn_combine)
        outs[0].append(c_lat[:m_p].reshape(bsz, seq, KV_LORA))
        outs[1].append(k_pe[:m_p].reshape(bsz, seq, QK_ROPE))
        outs[2].append(_state_unpack(hfin_p, state_ssm.dtype))
        outs[3].append(u_pool[:m_p].reshape(bsz, seq, W_POOL)[:, seq - POOL_PAD:])
        outs[4].append(c_lat[m_p:].reshape(db, t, KV_LORA))
        outs[5].append(k_pe[m_p:].reshape(db, t, QK_ROPE))
        outs[6].append(_state_unpack(hfin_s, state_ssm.dtype))
        outs[7].append(u_ext[:, t:])
    y = final_norm(x, final_norm_g[None], tm=tm_mid)
    return (y[:m_p].reshape(bsz, seq, D_MODEL), y[m_p:].reshape(db, t, D_MODEL),
            *[jnp.stack(o) for o in outs])
```
